```python
import math
import jax, jax.numpy as jnp
from jax import lax
import numpy as np

D_MODEL = 2048
BATCH = 2
SEQ = 8192
DEPTH = 1

N_META = 16
HEAD_DIM = 128
A_HEADS = 8
IDX_HEADS = 16
IDX_DIM = 64
TOPK_MAX = 256
B_HEADS = 4
N_EXPERTS = 32
TOP_K = 4
D_FF = 2048
SWIGLU_LIMIT = 7.0
SWIGLU_ALPHA = 1.702

ROPE_THETA = 10000.0
RMS_EPS = 1e-5
Q_BLOCK = 128
MOE_BLOCK = 128

A_WIDTH = A_HEADS * HEAD_DIM
B_QK_WIDTH = B_HEADS * 2 * HEAD_DIM
B_V_WIDTH = B_HEADS * 2 * HEAD_DIM
SPLIT_SIZES = (A_WIDTH, A_WIDTH, A_WIDTH,
               IDX_HEADS * IDX_DIM, IDX_DIM, IDX_HEADS,
               B_QK_WIDTH, B_QK_WIDTH, B_V_WIDTH,
               D_MODEL, D_MODEL)
D_IN = 3 * A_WIDTH + IDX_HEADS * IDX_DIM + IDX_DIM + IDX_HEADS + 2 * B_QK_WIDTH + B_V_WIDTH + 2 * D_MODEL

kernel_name = 'hybrid_dsa_diffattn_moe_block'


def rmsnorm(x, g):
    xf = x.astype(jnp.float32)
    y = xf * lax.rsqrt(jnp.mean(xf * xf, axis=-1, keepdims=True) + RMS_EPS)
    return (y * g.astype(jnp.float32)).astype(x.dtype)


def rope_tables(n_pos, dim):
    inv = 1.0 / (ROPE_THETA ** (jnp.arange(0, dim, 2, dtype=jnp.float32) / dim))
    ang = jnp.arange(n_pos, dtype=jnp.float32)[:, None] * inv[None, :]
    ang = jnp.concatenate([ang, ang], axis=-1)
    return jnp.cos(ang), jnp.sin(ang)


def apply_rope(x, cos, sin):
    xf = x.astype(jnp.float32)
    half = xf.shape[-1] // 2
    rot = jnp.concatenate([-xf[..., half:], xf[..., :half]], axis=-1)
    return (xf * cos[None, :, None, :] + rot * sin[None, :, None, :]).astype(x.dtype)


def _split_columns(proj):
    outs = []
    off = 0
    for size in SPLIT_SIZES:
        outs.append(proj[..., off:off + size])
        off += size
    return outs


def dsa_attention(q, k, v, q_idx, k_idx, w_idx, topk):
    bsz, tp, n_h, d = q.shape
    nb = tp // Q_BLOCK
    key_pos = jnp.arange(tp)
    scale = HEAD_DIM ** -0.5
    w_scaled = w_idx * (IDX_HEADS ** -0.5 * IDX_DIM ** -0.5)

    def block(i):
        start = i * Q_BLOCK
        qi = lax.dynamic_slice_in_dim(q_idx, start, Q_BLOCK, axis=1)
        wi = lax.dynamic_slice_in_dim(w_scaled, start, Q_BLOCK, axis=1)
        qa = lax.dynamic_slice_in_dim(q, start, Q_BLOCK, axis=1)
        q_pos = start + jnp.arange(Q_BLOCK)
        causal = key_pos[None, :] <= q_pos[:, None]
        rel = jax.nn.relu(jnp.einsum('bqhd,bkd->bqhk', qi, k_idx))
        score = jnp.einsum('bqhk,bqh->bqk', rel, wi).astype(jnp.float32)
        score = jnp.where(causal[None], score, -jnp.inf)
        _, sel = lax.top_k(score, topk)
        valid = sel <= q_pos[None, :, None]
        k_sel = jax.vmap(lambda kk, ii: kk[ii])(k, sel)
        v_sel = jax.vmap(lambda vv, ii: vv[ii])(v, sel)
        s = jnp.einsum('bqhd,bqkhd->bhqk', qa, k_sel).astype(jnp.float32) * scale
        s = jnp.where(valid[:, None], s, -jnp.inf)
        p = jax.nn.softmax(s, axis=-1).astype(v.dtype)
        return jnp.einsum('bhqk,bqkhd->bqhd', p, v_sel)

    out = lax.map(block, jnp.arange(nb))
    return jnp.moveaxis(out, 0, 1).reshape(bsz, tp, n_h, d)


def diff_attention(q1, q2, k1, k2, v, lam):
    bsz, tp, n_h, dv = v.shape
    nb = tp // Q_BLOCK
    key_pos = jnp.arange(tp)
    scale = HEAD_DIM ** -0.5

    def block(i):
        start = i * Q_BLOCK
        q1b = lax.dynamic_slice_in_dim(q1, start, Q_BLOCK, axis=1)
        q2b = lax.dynamic_slice_in_dim(q2, start, Q_BLOCK, axis=1)
        q_pos = start + jnp.arange(Q_BLOCK)
        causal = (key_pos[None, :] <= q_pos[:, None])[None, None]
        s1 = jnp.einsum('bqhd,bkhd->bhqk', q1b, k1).astype(jnp.float32) * scale
        s2 = jnp.einsum('bqhd,bkhd->bhqk', q2b, k2).astype(jnp.float32) * scale
        p1 = jax.nn.softmax(jnp.where(causal, s1, -jnp.inf), axis=-1)
        p2 = jax.nn.softmax(jnp.where(causal, s2, -jnp.inf), axis=-1)
        a = (p1 - lam * p2).astype(v.dtype)
        return jnp.einsum('bhqk,bkhe->bqhe', a, v)

    out = lax.map(block, jnp.arange(nb))
    return jnp.moveaxis(out, 0, 1).reshape(bsz, tp, n_h, dv)


def hybrid_mixer(h, w_in, lambda_q1, lambda_k1, lambda_q2, lambda_k2, subln_g, w_up_a, w_up_b, w_out,
                 cos, sin, cos_i, sin_i, topk, lambda_init):
    bsz, tp, _ = h.shape
    proj = h @ w_in
    qa, ka, va, qi, ki, wi, qb, kb, vb, ga, gb = _split_columns(proj)

    qa = apply_rope(qa.reshape(bsz, tp, A_HEADS, HEAD_DIM), cos, sin)
    ka = apply_rope(ka.reshape(bsz, tp, A_HEADS, HEAD_DIM), cos, sin)
    va = va.reshape(bsz, tp, A_HEADS, HEAD_DIM)
    qi = apply_rope(qi.reshape(bsz, tp, IDX_HEADS, IDX_DIM), cos_i, sin_i)
    ki = apply_rope(ki[:, :, None, :], cos_i, sin_i)[:, :, 0, :]
    o_a = dsa_attention(qa, ka, va, qi, ki, wi, topk).reshape(bsz, tp, A_WIDTH)

    qb = qb.reshape(bsz, tp, B_HEADS, 2, HEAD_DIM)
    kb = kb.reshape(bsz, tp, B_HEADS, 2, HEAD_DIM)
    q1 = apply_rope(qb[..., 0, :], cos, sin)
    q2 = apply_rope(qb[..., 1, :], cos, sin)
    k1 = apply_rope(kb[..., 0, :], cos, sin)
    k2 = apply_rope(kb[..., 1, :], cos, sin)
    vb = vb.reshape(bsz, tp, B_HEADS, 2 * HEAD_DIM)
    lam = (jnp.exp(jnp.sum(lambda_q1.astype(jnp.float32) * lambda_k1.astype(jnp.float32)))
           - jnp.exp(jnp.sum(lambda_q2.astype(jnp.float32) * lambda_k2.astype(jnp.float32)))
           + lambda_init)
    o_b = diff_attention(q1, q2, k1, k2, vb, lam)
    o_b = (rmsnorm(o_b, subln_g) * (1.0 - lambda_init)).reshape(bsz, tp, B_V_WIDTH)

    merged = jax.nn.sigmoid(ga) * (o_a @ w_up_a) + jax.nn.sigmoid(gb) * (o_b @ w_up_b)
    return merged @ w_out


def clamped_swiglu(u):
    x_glu = jnp.minimum(u[..., ::2], SWIGLU_LIMIT)
    x_lin = jnp.clip(u[..., 1::2], -SWIGLU_LIMIT, SWIGLU_LIMIT)
    return x_glu * jax.nn.sigmoid(SWIGLU_ALPHA * x_glu) * (x_lin + 1.0)


def moe_ffn(h, w_router, b_router, w1, b1, w2, b2):
    n_tok, d = h.shape
    logits = (h @ w_router + b_router).astype(jnp.float32)
    top_val, top_idx = lax.top_k(logits, TOP_K)
    gate = jax.nn.softmax(top_val, axis=-1)
    n_assign = n_tok * TOP_K
    flat_e = top_idx.reshape(-1)
    flat_tok = jnp.arange(n_assign, dtype=jnp.int32) // TOP_K
    order = jnp.argsort(flat_e, stable=True)
    e_sorted = flat_e[order]
    tok_sorted = flat_tok[order]
    gate_sorted = gate.reshape(-1)[order]
    counts = jnp.bincount(flat_e, length=N_EXPERTS)
    padded = (counts + MOE_BLOCK - 1) // MOE_BLOCK * MOE_BLOCK
    pad_end = jnp.cumsum(padded)
    pad_start = pad_end - padded
    start = jnp.cumsum(counts) - counts
    dest = pad_start[e_sorted] + jnp.arange(n_assign) - start[e_sorted]
    n_blocks = (n_assign + N_EXPERTS * (MOE_BLOCK - 1) + MOE_BLOCK - 1) // MOE_BLOCK
    n_rows = n_blocks * MOE_BLOCK
    row_tok = jnp.full((n_rows,), n_tok, dtype=jnp.int32).at[dest].set(tok_sorted)
    h_ext = jnp.concatenate([h, jnp.zeros((1, d), h.dtype)], axis=0)
    xb = h_ext[row_tok].reshape(n_blocks, MOE_BLOCK, d)
    block_e = jnp.minimum(jnp.searchsorted(pad_end, jnp.arange(n_blocks) * MOE_BLOCK, side='right'),
                          N_EXPERTS - 1)

    def expert_block(args):
        x_blk, e = args
        u = x_blk @ w1[e] + b1[e]
        return clamped_swiglu(u) @ w2[e] + b2[e]

    yb = lax.map(expert_block, (xb, block_e)).reshape(n_rows, d)
    y_assign = yb[dest] * gate_sorted[:, None].astype(yb.dtype)
    return jax.ops.segment_sum(y_assign, tok_sorted, num_segments=n_tok)


def setup_inputs(seed: int = 0) -> dict:
    key = jax.random.key(seed)
    ks = jax.random.split(key, 20)
    f32 = jnp.float32
    nrm = lambda k, shape, s: jax.random.normal(k, shape, f32) * s
    return {
        'x': nrm(ks[0], (BATCH, SEQ, D_MODEL), 1.0),
        'meta_tokens': nrm(ks[1], (N_META, D_MODEL), 1.0),
        'norm_mix_g': 1.0 + nrm(ks[2], (DEPTH, D_MODEL), 0.01),
        'w_in': nrm(ks[3], (DEPTH, D_MODEL, D_IN), D_MODEL ** -0.5),
        'lambda_q1': nrm(ks[4], (DEPTH, HEAD_DIM), 0.1),
        'lambda_k1': nrm(ks[5], (DEPTH, HEAD_DIM), 0.1),
        'lambda_q2': nrm(ks[6], (DEPTH, HEAD_DIM), 0.1),
        'lambda_k2': nrm(ks[7], (DEPTH, HEAD_DIM), 0.1),
        'subln_g': 1.0 + nrm(ks[8], (DEPTH, 2 * HEAD_DIM), 0.01),
        'w_up_a': nrm(ks[9], (DEPTH, A_WIDTH, D_MODEL), A_WIDTH ** -0.5),
        'w_up_b': nrm(ks[10], (DEPTH, B_V_WIDTH, D_MODEL), B_V_WIDTH ** -0.5),
        'w_out': nrm(ks[11], (DEPTH, D_MODEL, D_MODEL), D_MODEL ** -0.5),
        'norm_ffn_g': 1.0 + nrm(ks[12], (DEPTH, D_MODEL), 0.01),
        'w_router': nrm(ks[13], (DEPTH, D_MODEL, N_EXPERTS), D_MODEL ** -0.5),
        'b_router': nrm(ks[14], (DEPTH, N_EXPERTS), 0.01),
        'w1': nrm(ks[15], (DEPTH, N_EXPERTS, D_MODEL, 2 * D_FF), D_MODEL ** -0.5),
        'b1': nrm(ks[16], (DEPTH, N_EXPERTS, 2 * D_FF), 0.01),
        'w2': nrm(ks[17], (DEPTH, N_EXPERTS, D_FF, D_MODEL), D_FF ** -0.5),
        'b2': nrm(ks[18], (DEPTH, N_EXPERTS, D_MODEL), 0.01),
        'norm_final_g': 1.0 + nrm(ks[19], (D_MODEL,), 0.01),
    }


def reference(x, meta_tokens, norm_mix_g, w_in, lambda_q1, lambda_k1, lambda_q2, lambda_k2, subln_g,
              w_up_a, w_up_b, w_out, norm_ffn_g, w_router, b_router, w1, b1, w2, b2, norm_final_g):
    bsz, s_len, _ = x.shape
    meta = jnp.broadcast_to(meta_tokens[None].astype(x.dtype), (bsz, N_META, D_MODEL))
    h = jnp.concatenate([meta, x], axis=1)
    t_len = s_len + N_META
    t_pad = -(-t_len // Q_BLOCK) * Q_BLOCK
    topk = min(TOPK_MAX, t_len // 4)
    cos, sin = rope_tables(t_pad, HEAD_DIM)
    cos_i, sin_i = rope_tables(t_pad, IDX_DIM)
    for l in range(DEPTH):
        lambda_init = 0.8 - 0.6 * math.exp(-0.3 * l)
        hn = jnp.pad(rmsnorm(h, norm_mix_g[l]), ((0, 0), (0, t_pad - t_len), (0, 0)))
        mix = hybrid_mixer(hn, w_in[l], lambda_q1[l], lambda_k1[l], lambda_q2[l], lambda_k2[l], subln_g[l],
                           w_up_a[l], w_up_b[l], w_out[l], cos, sin, cos_i, sin_i, topk, lambda_init)
        h = h + mix[:, :t_len]
        hf = rmsnorm(h, norm_ffn_g[l]).reshape(bsz * t_len, D_MODEL)
        y = moe_ffn(hf, w_router[l], b_router[l], w1[l], b1[l], w2[l], b2[l])
        h = h + y.reshape(bsz, t_len, D_MODEL)
    h = rmsnorm(h, norm_final_g)
    return h[:, N_META:]
```

```python
import functools
import math

import jax
import jax.numpy as jnp
import numpy as np
from jax import lax
from jax.experimental import pallas as pl
from jax.experimental.pallas import tpu as pltpu

HEAD_DIM = 128
A_HEADS = 8
IDX_HEADS = 16
IDX_DIM = 64
TOPK_MAX = 256
B_HEADS = 4
TOP_K = 4
SWIGLU_LIMIT = 7.0
SWIGLU_ALPHA = 1.702
ROPE_THETA = 10000.0
RMS_EPS = 1e-5
ROW_BLOCK = 128
LANES = 128
NEG_BIG = -1e30
VMEM_LIMIT = 56 * 1024 * 1024

A_WIDTH = A_HEADS * HEAD_DIM
B_QK_WIDTH = B_HEADS * 2 * HEAD_DIM
B_V_WIDTH = B_HEADS * 2 * HEAD_DIM

F32 = jnp.float32
BF16 = jnp.bfloat16
I32 = jnp.int32

KEY_NEG_INF = int(np.array(0xFF800000, np.uint32).view(np.int32)) ^ 0x7FFFFFFF
INT_MIN = -(2 ** 31)


def _pick_tile(n, candidates):
    for c in candidates:
        if n % c == 0:
            return c
    raise ValueError(f"no tile for {n}")


def _params(sem, vmem=VMEM_LIMIT):
    return pltpu.CompilerParams(dimension_semantics=sem, vmem_limit_bytes=vmem)


def _rmsnorm_kernel(x_ref, g_ref, o_ref):
    x = x_ref[...]
    ms = jnp.mean(x * x, axis=-1, keepdims=True)
    o_ref[...] = (x * lax.rsqrt(ms + RMS_EPS) * g_ref[...]).astype(o_ref.dtype)


def _rmsnorm_call(x2d, g, tm):
    rows, d = x2d.shape
    return pl.pallas_call(
        _rmsnorm_kernel,
        grid=(rows // tm,),
        in_specs=[pl.BlockSpec((tm, d), lambda i: (i, 0)),
                  pl.BlockSpec((1, d), lambda i: (0, 0))],
        out_specs=pl.BlockSpec((tm, d), lambda i: (i, 0)),
        out_shape=jax.ShapeDtypeStruct((rows, d), BF16),
        compiler_params=_params(("parallel",)),
    )(x2d, g.reshape(1, d))


def _proj_kernel(*refs, mode, rot_half, scale):
    if mode == "rope":
        x_ref, w_ref, cos_ref, sin_ref, o_ref = refs
    else:
        x_ref, w_ref, o_ref = refs
    acc = jnp.dot(x_ref[...], w_ref[...], preferred_element_type=F32)
    if mode == "rope":
        cos = cos_ref[...]
        sin = sin_ref[...]
        tn = acc.shape[1]
        lane = lax.broadcasted_iota(I32, (acc.shape[0], LANES), 1)
        outs = []
        for c in range(tn // LANES):
            xh = acc[:, c * LANES:(c + 1) * LANES]
            if rot_half == LANES // 2:
                rot = pltpu.roll(xh, LANES // 2, 1)
            else:
                fwd = pltpu.roll(xh, LANES - rot_half, 1)
                bwd = pltpu.roll(xh, rot_half, 1)
                rot = jnp.where((lane % (2 * rot_half)) < rot_half, fwd, bwd)
            outs.append(xh * cos + rot * sin)
        acc = outs[0] if len(outs) == 1 else jnp.concatenate(outs, axis=1)
    elif mode == "sigmoid":
        acc = jax.nn.sigmoid(acc)
    elif mode == "scale":
        acc = acc * scale
    o_ref[...] = acc.astype(o_ref.dtype)


def _proj_call(xn, w, out_dtype, mode, tm, tn, cos=None, sin=None, rot_half=0, scale=1.0, tp=None):
    rows, d = xn.shape
    n = w.shape[1]
    in_specs = [pl.BlockSpec((tm, d), lambda i, j: (i, 0)),
                pl.BlockSpec((d, tn), lambda i, j: (0, j))]
    args = [xn, w]
    if mode == "rope":
        nt = tp // tm
        in_specs += [pl.BlockSpec((tm, LANES), lambda i, j: (i % nt, 0)),
                     pl.BlockSpec((tm, LANES), lambda i, j: (i % nt, 0))]
        args += [cos, sin]
    return pl.pallas_call(
        functools.partial(_proj_kernel, mode=mode, rot_half=rot_half, scale=scale),
        grid=(rows // tm, n // tn),
        in_specs=in_specs,
        out_specs=pl.BlockSpec((tm, tn), lambda i, j: (i, j)),
        out_shape=jax.ShapeDtypeStruct((rows, n), out_dtype),
        compiler_params=_params(("parallel", "arbitrary")),
    )(*args)


def _indexer_kernel(qi_ref, ka_ref, kb_ref, wi_ref, mask_ref, keys_ref, wb_ref, p_ref,
                    *, front, topk, n_chunks_total):
    i = pl.program_id(1)
    n_chunk = i + 1
    tq = qi_ref.shape[0]
    q_idx = i * tq + lax.broadcasted_iota(I32, (tq, LANES), 0)
    lane = lax.broadcasted_iota(I32, (tq, LANES), 1)

    w = wi_ref[...]
    for h in range(IDX_HEADS):
        wb_ref[h] = jnp.broadcast_to(w[:, h:h + 1], (tq, LANES))

    nt_dims = (((1,), (1,)), ((), ()))

    def score_body(c, carry):
        off = pl.multiple_of(c * LANES, LANES)
        ka = ka_ref[pl.ds(off, LANES), :]
        kb = kb_ref[pl.ds(off, LANES), :]
        acc = jnp.zeros((tq, LANES), F32)
        for p in range(IDX_HEADS // 2):
            qp = qi_ref[:, p * LANES:(p + 1) * LANES]
            s0 = lax.dot_general(qp, ka, nt_dims, preferred_element_type=F32)
            s1 = lax.dot_general(qp, kb, nt_dims, preferred_element_type=F32)
            acc = acc + wb_ref[2 * p] * jnp.maximum(s0, 0.0)
            acc = acc + wb_ref[2 * p + 1] * jnp.maximum(s1, 0.0)
        k_idx = off + lane
        valid = (k_idx <= q_idx) & (k_idx >= front)
        acc = jnp.where(acc == 0.0, 0.0, acc)
        score = jnp.where(valid, acc, -jnp.inf)
        bits = pltpu.bitcast(score, I32)
        keys_ref[:, pl.ds(off, LANES)] = jnp.where(bits < 0, bits ^ 0x7FFFFFFF, bits)
        return carry

    lax.fori_loop(0, n_chunk, score_body, 0)

    def count(pred_fn):
        def body(c, cnt):
            off = pl.multiple_of(c * LANES, LANES)
            k = keys_ref[:, pl.ds(off, LANES)]
            return cnt + jnp.where(pred_fn(k, off + lane), 1, 0)
        cnt = lax.fori_loop(0, n_chunk, body, jnp.zeros((tq, LANES), I32))
        return jnp.sum(cnt, axis=1, keepdims=True)

    def bit_body(it, t):
        trial = t + jnp.left_shift(jnp.int32(1), 31 - it)
        cnt = count(lambda k, _: k >= trial)
        return jnp.where(cnt >= topk, trial, t)

    thr = lax.fori_loop(0, 32, bit_body, jnp.full((tq, 1), INT_MIN, I32))

    cnt_ge = count(lambda k, _: k >= thr)
    finite = thr > KEY_NEG_INF
    tie_rows = finite & (cnt_ge > topk)
    p_ref[...] = jnp.full((tq, LANES), n_chunks_total * LANES, I32)

    @pl.when(jnp.max(jnp.where(tie_rows, 1, 0)) > 0)
    def _():
        cnt_gt = count(lambda k, _: k > thr)
        need = topk - cnt_gt

        def idx_body(it, p):
            trial = p + jnp.left_shift(jnp.int32(1), 14 - it)
            cnt = count(lambda k, idx: (k == thr) & (idx < trial))
            return jnp.where(cnt < need, trial, p)

        p = lax.fori_loop(0, 15, idx_body, jnp.zeros((tq, 1), I32))
        p = jnp.where(tie_rows, p, n_chunks_total * LANES)
        p_ref[...] = jnp.broadcast_to(p, (tq, LANES))

    p_lim = p_ref[...]

    def write_body(c, carry):
        off = pl.multiple_of(c * LANES, LANES)
        k = keys_ref[:, pl.ds(off, LANES)]
        k_idx = off + lane
        sel = (k > thr) | ((k == thr) & (k_idx <= p_lim))
        valid = (k_idx <= q_idx) & (k_idx >= front)
        mask_ref[:, pl.ds(off, LANES)] = jnp.where(sel & valid, 1, 0).astype(jnp.int8)
        return carry

    lax.fori_loop(0, n_chunk, write_body, 0)

    def zero_body(c, carry):
        off = pl.multiple_of(c * LANES, LANES)
        mask_ref[:, pl.ds(off, LANES)] = jnp.zeros((tq, LANES), jnp.int8)
        return carry

    lax.fori_loop(n_chunk, n_chunks_total, zero_body, 0)


def _indexer_call(idx_qk, wi, bsz, tp, front, topk):
    tq = ROW_BLOCK
    nq = tp // tq
    n_qcols = IDX_HEADS * IDX_DIM
    ka_blk = n_qcols // LANES
    return pl.pallas_call(
        functools.partial(_indexer_kernel, front=front, topk=topk, n_chunks_total=nq),
        grid=(bsz, nq),
        in_specs=[pl.BlockSpec((tq, n_qcols), lambda b, i: (b * nq + i, 0)),
                  pl.BlockSpec((tp, LANES), lambda b, i: (b, ka_blk)),
                  pl.BlockSpec((tp, LANES), lambda b, i: (b, ka_blk + 1)),
                  pl.BlockSpec((tq, LANES), lambda b, i: (b * nq + i, 0))],
        out_specs=pl.BlockSpec((tq, tp), lambda b, i: (b * nq + i, 0)),
        out_shape=jax.ShapeDtypeStruct((bsz * tp, tp), jnp.int8),
        scratch_shapes=[pltpu.VMEM((tq, tp), I32),
                        pltpu.VMEM((IDX_HEADS, tq, LANES), F32),
                        pltpu.VMEM((tq, LANES), I32)],
        compiler_params=_params(("parallel", "arbitrary")),
    )(idx_qk, idx_qk, idx_qk, wi)


def _attn_a_kernel(q_ref, k_ref, v_ref, mask_ref, o_ref, m_ref, l_ref, acc_ref):
    i = pl.program_id(1)
    j = pl.program_id(2)
    scale = HEAD_DIM ** -0.5
    nt_dims = (((1,), (1,)), ((), ()))

    @pl.when(j == 0)
    def _():
        m_ref[...] = jnp.full(m_ref.shape, NEG_BIG, F32)
        l_ref[...] = jnp.zeros(l_ref.shape, F32)
        acc_ref[...] = jnp.zeros(acc_ref.shape, F32)

    @pl.when(j <= i)
    def _():
        keep = mask_ref[...].astype(I32) != 0
        for h in range(A_HEADS):
            sl = slice(h * HEAD_DIM, (h + 1) * HEAD_DIM)
            s = lax.dot_general(q_ref[:, sl], k_ref[:, sl], nt_dims, preferred_element_type=F32) * scale
            s = jnp.where(keep, s, NEG_BIG)
            m_prev = m_ref[h]
            m_new = jnp.maximum(m_prev, jnp.max(s, axis=1, keepdims=True))
            alpha = jnp.exp(m_prev - m_new)
            p = jnp.exp(s - m_new)
            l_ref[h] = alpha * l_ref[h] + jnp.sum(p, axis=1, keepdims=True)
            acc_ref[:, sl] = alpha * acc_ref[:, sl] + jnp.dot(p.astype(BF16), v_ref[:, sl],
                                                              preferred_element_type=F32)
            m_ref[h] = m_new

    @pl.when(j == i)
    def _():
        for h in range(A_HEADS):
            sl = slice(h * HEAD_DIM, (h + 1) * HEAD_DIM)
            o_ref[:, sl] = (acc_ref[:, sl] / l_ref[h]).astype(o_ref.dtype)


def _attn_a_call(qk, v, mask, bsz, tp, tt):
    nb = tp // tt
    w = A_WIDTH
    return pl.pallas_call(
        _attn_a_kernel,
        grid=(bsz, nb, nb),
        in_specs=[pl.BlockSpec((tt, w), lambda b, i, j: (b * nb + i, 0)),
                  pl.BlockSpec((tt, w), lambda b, i, j: (b * nb + jnp.minimum(i, j), 1)),
                  pl.BlockSpec((tt, w), lambda b, i, j: (b * nb + jnp.minimum(i, j), 0)),
                  pl.BlockSpec((tt, tt), lambda b, i, j: (b * nb + i, jnp.minimum(i, j)))],
        out_specs=pl.BlockSpec((tt, w), lambda b, i, j: (b * nb + i, 0)),
        out_shape=jax.ShapeDtypeStruct((bsz * tp, w), BF16),
        scratch_shapes=[pltpu.VMEM((A_HEADS, tt, 1), F32),
                        pltpu.VMEM((A_HEADS, tt, 1), F32),
                        pltpu.VMEM((tt, w), F32)],
        compiler_params=_params(("parallel", "parallel", "arbitrary")),
    )(qk, qk, v, mask)


def _attn_b_kernel(q_ref, k_ref, v_ref, lq1_ref, lk1_ref, lq2_ref, lk2_ref, g_ref, o_ref,
                   m_ref, l_ref, acc_ref, *, front, lambda_init):
    i = pl.program_id(1)
    j = pl.program_id(2)
    tt = q_ref.shape[0]
    scale = HEAD_DIM ** -0.5
    nt_dims = (((1,), (1,)), ((), ()))
    dv = 2 * HEAD_DIM

    @pl.when(j == 0)
    def _():
        m_ref[...] = jnp.full(m_ref.shape, NEG_BIG, F32)
        l_ref[...] = jnp.zeros(l_ref.shape, F32)
        acc_ref[...] = jnp.zeros(acc_ref.shape, F32)

    @pl.when(j <= i)
    def _():
        q_idx = i * tt + lax.broadcasted_iota(I32, (tt, tt), 0)
        k_idx = j * tt + lax.broadcasted_iota(I32, (tt, tt), 1)
        keep = (k_idx <= q_idx) & (k_idx >= front)
        for h in range(B_HEADS):
            vh = v_ref[:, h * dv:(h + 1) * dv]
            for part in range(2):
                c = 2 * h + part
                sl = slice(c * HEAD_DIM, (c + 1) * HEAD_DIM)
                s = lax.dot_general(q_ref[:, sl], k_ref[:, sl], nt_dims, preferred_element_type=F32) * scale
                s = jnp.where(keep, s, NEG_BIG)
                m_prev = m_ref[c]
                m_new = jnp.maximum(m_prev, jnp.max(s, axis=1, keepdims=True))
                alpha = jnp.exp(m_prev - m_new)
                p = jnp.exp(s - m_new)
                l_ref[c] = alpha * l_ref[c] + jnp.sum(p, axis=1, keepdims=True)
                acc_ref[c] = alpha * acc_ref[c] + jnp.dot(p.astype(BF16), vh, preferred_element_type=F32)
                m_ref[c] = m_new

    @pl.when(j == i)
    def _():
        lam = (jnp.exp(jnp.sum(lq1_ref[...] * lk1_ref[...], axis=1, keepdims=True))
               - jnp.exp(jnp.sum(lq2_ref[...] * lk2_ref[...], axis=1, keepdims=True))
               + lambda_init)
        g = g_ref[...]
        for h in range(B_HEADS):
            o = acc_ref[2 * h] / l_ref[2 * h] - lam * (acc_ref[2 * h + 1] / l_ref[2 * h + 1])
            ms = jnp.mean(o * o, axis=-1, keepdims=True)
            y = o * lax.rsqrt(ms + RMS_EPS) * g
            o_ref[:, h * dv:(h + 1) * dv] = (y * (1.0 - lambda_init)).astype(o_ref.dtype)


def _attn_b_call(qk, v, lq1, lk1, lq2, lk2, subln_g, bsz, tp, tt, front, lambda_init):
    nb = tp // tt
    w = B_QK_WIDTH
    dv = 2 * HEAD_DIM
    vec = lambda a: a.reshape(1, -1).astype(F32)
    small = lambda n: pl.BlockSpec((1, n), lambda b, i, j: (0, 0))
    return pl.pallas_call(
        functools.partial(_attn_b_kernel, front=front, lambda_init=lambda_init),
        grid=(bsz, nb, nb),
        in_specs=[pl.BlockSpec((tt, w), lambda b, i, j: (b * nb + i, 2)),
                  pl.BlockSpec((tt, w), lambda b, i, j: (b * nb + jnp.minimum(i, j), 3)),
                  pl.BlockSpec((tt, B_V_WIDTH), lambda b, i, j: (b * nb + jnp.minimum(i, j), 1)),
                  small(HEAD_DIM), small(HEAD_DIM), small(HEAD_DIM), small(HEAD_DIM), small(dv)],
        out_specs=pl.BlockSpec((tt, B_V_WIDTH), lambda b, i, j: (b * nb + i, 0)),
        out_shape=jax.ShapeDtypeStruct((bsz * tp, B_V_WIDTH), BF16),
        scratch_shapes=[pltpu.VMEM((2 * B_HEADS, tt, 1), F32),
                        pltpu.VMEM((2 * B_HEADS, tt, 1), F32),
                        pltpu.VMEM((2 * B_HEADS, tt, dv), F32)],
        compiler_params=_params(("parallel", "parallel", "arbitrary")),
    )(qk, qk, v, vec(lq1), vec(lk1), vec(lq2), vec(lk2), vec(subln_g))


def _merge_router_kernel(oa_ref, ob_ref, ga_ref, gb_ref, h_ref, wua_ref, wub_ref, wo_ref, g_ref,
                         wr_ref, br_ref, h1_ref, hfp_ref, route_ref, gate_ref, cnt_ref, carry_ref,
                         *, tp, front):
    i = pl.program_id(0)
    tm = oa_ref.shape[0]
    d = h_ref.shape[1]

    @pl.when(i == 0)
    def _():
        carry_ref[...] = jnp.zeros(carry_ref.shape, F32)

    ua = jnp.dot(oa_ref[...], wua_ref[...], preferred_element_type=F32)
    ub = jnp.dot(ob_ref[...], wub_ref[...], preferred_element_type=F32)
    merged = ga_ref[...].astype(F32) * ua + gb_ref[...].astype(F32) * ub
    mix = jnp.dot(merged.astype(BF16), wo_ref[...], preferred_element_type=F32)
    h1 = h_ref[...] + mix
    h1_ref[...] = h1

    ms = jnp.mean(h1 * h1, axis=-1, keepdims=True)
    hf = (h1 * lax.rsqrt(ms + RMS_EPS) * g_ref[...]).astype(BF16)

    hf_bits = pltpu.bitcast(hf.astype(F32), jnp.uint32)
    hfp_ref[...] = (hf_bits[:, :d // 2] & jnp.uint32(0xFFFF0000)) | (hf_bits[:, d // 2:] >> 16)

    logits = jnp.dot(hf, wr_ref[...], preferred_element_type=F32) + br_ref[...]
    lane = lax.broadcasted_iota(I32, (tm, LANES), 1)
    row = (i * tm + lax.broadcasted_iota(I32, (tm, 1), 0)) % tp
    routed = row >= front

    work = logits
    vals, idxs, onehots = [], [], []
    for _ in range(TOP_K):
        v = jnp.max(work, axis=1, keepdims=True)
        e = jnp.min(jnp.where(work == v, lane, LANES), axis=1, keepdims=True)
        hit = lane == e
        vals.append(v)
        idxs.append(e)
        onehots.append(hit)
        work = jnp.where(hit, -jnp.inf, work)

    exps = [jnp.exp(v - vals[0]) for v in vals]
    denom = exps[0] + exps[1] + exps[2] + exps[3]

    member = jnp.zeros((tm, LANES), F32)
    for hit in onehots:
        member = member + jnp.where(hit & routed, 1.0, 0.0)
    r_i = lax.broadcasted_iota(I32, (tm, tm), 0)
    c_i = lax.broadcasted_iota(I32, (tm, tm), 1)
    lower = jnp.where(c_i < r_i, 1.0, 0.0).astype(BF16)
    before = jnp.dot(lower, member.astype(BF16), preferred_element_type=F32) + carry_ref[...]
    carry_ref[...] = carry_ref[...] + jnp.sum(member, axis=0, keepdims=True)

    route = jnp.zeros((tm, LANES), I32)
    gates = jnp.zeros((tm, LANES), F32)
    for k in range(TOP_K):
        rank = jnp.sum(jnp.where(onehots[k], before, 0.0), axis=1, keepdims=True)
        route = jnp.where(lane == k, idxs[k], route)
        route = jnp.where(lane == TOP_K + k, rank.astype(I32), route)
        gates = jnp.where(lane == k, exps[k] / denom, gates)
    route_ref[...] = route
    gate_ref[...] = gates
    cnt_ref[...] = jnp.broadcast_to(carry_ref[...], cnt_ref.shape)


def _merge_router_call(o_a, o_b, gates, h0, wua, wub, wo, g_ffn, wr, br, tp, front, tm):
    rows, d = h0.shape
    nt = rows // tm
    once = pl.Buffered(1)
    const = lambda shape: pl.BlockSpec(shape, lambda i: (0, 0), pipeline_mode=once)
    return pl.pallas_call(
        functools.partial(_merge_router_kernel, tp=tp, front=front),
        grid=(nt,),
        in_specs=[pl.BlockSpec((tm, A_WIDTH), lambda i: (i, 0)),
                  pl.BlockSpec((tm, B_V_WIDTH), lambda i: (i, 0)),
                  pl.BlockSpec((tm, d), lambda i: (i, 0)),
                  pl.BlockSpec((tm, d), lambda i: (i, 1)),
                  pl.BlockSpec((tm, d), lambda i: (i, 0)),
                  const((A_WIDTH, d)), const((B_V_WIDTH, d)), const((d, d)),
                  const((1, d)), const((d, LANES)), const((1, LANES))],
        out_specs=[pl.BlockSpec((tm, d), lambda i: (i, 0)),
                   pl.BlockSpec((tm, d // 2), lambda i: (i, 0)),
                   pl.BlockSpec((tm, LANES), lambda i: (i, 0)),
                   pl.BlockSpec((tm, LANES), lambda i: (i, 0)),
                   pl.BlockSpec((8, LANES), lambda i: (0, 0))],
        out_shape=[jax.ShapeDtypeStruct((rows, d), F32),
                   jax.ShapeDtypeStruct((rows, d // 2), jnp.uint32),
                   jax.ShapeDtypeStruct((rows, LANES), I32),
                   jax.ShapeDtypeStruct((rows, LANES), F32),
                   jax.ShapeDtypeStruct((8, LANES), F32)],
        scratch_shapes=[pltpu.VMEM((1, LANES), F32)],
        compiler_params=_params(("arbitrary",)),
    )(o_a, o_b, gates, gates, h0, wua, wub, wo, g_ffn, wr, br)


def _dispatch_kernel(dest_ref, hfp_ref, xb_in_ref, xb_ref, sem):
    del xb_in_ref
    tm = hfp_ref.shape[0]

    def row_copy(r, k):
        d = dest_ref[r * TOP_K + k]
        return pltpu.make_async_copy(hfp_ref.at[pl.ds(r, 1), :], xb_ref.at[pl.ds(d, 1), :], sem)

    def issue(r, carry):
        for k in range(TOP_K):
            row_copy(r, k).start()
        return carry

    lax.fori_loop(0, tm, issue, 0)

    def drain(r, carry):
        for k in range(TOP_K):
            row_copy(r, k).wait()
        return carry

    lax.fori_loop(0, tm, drain, 0)


def _dispatch_call(dest_flat, hfp, n_rows_total, tm):
    rows, half = hfp.shape
    xb0 = jnp.zeros((n_rows_total, half), jnp.uint32)
    return pl.pallas_call(
        _dispatch_kernel,
        grid=(rows // tm,),
        in_specs=[pl.BlockSpec((tm * TOP_K,), lambda i: (i,), memory_space=pltpu.SMEM),
                  pl.BlockSpec((tm, half), lambda i: (i, 0)),
                  pl.BlockSpec(memory_space=pl.ANY)],
        out_specs=pl.BlockSpec(memory_space=pl.ANY),
        out_shape=jax.ShapeDtypeStruct((n_rows_total, half), jnp.uint32),
        scratch_shapes=[pltpu.SemaphoreType.DMA(())],
        input_output_aliases={2: 0},
        compiler_params=_params(("arbitrary",)),
    )(dest_flat, hfp, xb0)


def _expert_kernel(be_ref, nu_ref, xb_ref, w1_ref, b1_ref, w2_ref, b2_ref, y_ref):
    i = pl.program_id(0)
    d_ff = w2_ref.shape[1]

    @pl.when(i < nu_ref[0])
    def _():
        packed = xb_ref[...]
        hi = pltpu.bitcast(packed & jnp.uint32(0xFFFF0000), F32)
        lo = pltpu.bitcast(packed << 16, F32)
        x = jnp.concatenate([hi, lo], axis=1).astype(BF16)
        u = jnp.dot(x, w1_ref[0], preferred_element_type=F32) + b1_ref[0]
        x_glu = jnp.minimum(u[:, :d_ff], SWIGLU_LIMIT)
        x_lin = jnp.clip(u[:, d_ff:], -SWIGLU_LIMIT, SWIGLU_LIMIT)
        act = x_glu * jax.nn.sigmoid(SWIGLU_ALPHA * x_glu) * (x_lin + 1.0)
        y_ref[...] = jnp.dot(act.astype(BF16), w2_ref[0], preferred_element_type=F32) + b2_ref[0]

    @pl.when(i >= nu_ref[0])
    def _():
        y_ref[...] = jnp.zeros(y_ref.shape, F32)


def _expert_call(block_e, n_used, xb, w1p, b1p, w2b, b2, n_blocks):
    n_exp, d, two_ff = w1p.shape
    d_ff = two_ff // 2
    half = xb.shape[1]
    once = pl.Buffered(1)

    def blk(i, be, nu):
        return jnp.minimum(i, nu[0] - 1)

    grid_spec = pltpu.PrefetchScalarGridSpec(
        num_scalar_prefetch=2,
        grid=(n_blocks,),
        in_specs=[pl.BlockSpec((ROW_BLOCK, half), lambda i, be, nu: (blk(i, be, nu), 0)),
                  pl.BlockSpec((1, d, two_ff), lambda i, be, nu: (be[blk(i, be, nu)], 0, 0), pipeline_mode=once),
                  pl.BlockSpec((1, 1, two_ff), lambda i, be, nu: (be[blk(i, be, nu)], 0, 0)),
                  pl.BlockSpec((1, d_ff, d), lambda i, be, nu: (be[blk(i, be, nu)], 0, 0), pipeline_mode=once),
                  pl.BlockSpec((1, 1, d), lambda i, be, nu: (be[blk(i, be, nu)], 0, 0))],
        out_specs=pl.BlockSpec((ROW_BLOCK, d), lambda i, be, nu: (i, 0)),
    )
    return pl.pallas_call(
        _expert_kernel,
        grid_spec=grid_spec,
        out_shape=jax.ShapeDtypeStruct((n_blocks * ROW_BLOCK, d), F32),
        compiler_params=_params(("arbitrary",)),
    )(block_e, n_used, xb, w1p, b1p.reshape(n_exp, 1, two_ff), w2b, b2.reshape(n_exp, 1, d))


def _combine_kernel(dest_ref, h1_ref, gate_ref, g_ref, yb_ref, o_ref, ybuf_ref, sem):
    tm = h1_ref.shape[0]

    def row_copy(r, k):
        d = dest_ref[r * TOP_K + k]
        return pltpu.make_async_copy(yb_ref.at[pl.ds(d, 1), :], ybuf_ref.at[k, pl.ds(r, 1), :], sem)

    def issue(r, carry):
        for k in range(TOP_K):
            row_copy(r, k).start()
        return carry

    lax.fori_loop(0, tm, issue, 0)

    def drain(r, carry):
        for k in range(TOP_K):
            row_copy(r, k).wait()
        return carry

    lax.fori_loop(0, tm, drain, 0)

    gate = gate_ref[...]
    h2 = h1_ref[...]
    for k in range(TOP_K):
        h2 = h2 + gate[:, k:k + 1] * ybuf_ref[k]
    ms = jnp.mean(h2 * h2, axis=-1, keepdims=True)
    o_ref[...] = h2 * lax.rsqrt(ms + RMS_EPS) * g_ref[...]


def _combine_call(dest_flat, h1, gate, g_final, yb, bsz, s_len, tp):
    d = h1.shape[1]
    tm = ROW_BLOCK
    ns = s_len // tm
    nb = tp // tm
    skip = (tp - s_len) // tm

    def src(i):
        return (i // ns) * nb + skip + i % ns

    return pl.pallas_call(
        _combine_kernel,
        grid=(bsz * ns,),
        in_specs=[pl.BlockSpec((tm * TOP_K,), lambda i: (src(i),), memory_space=pltpu.SMEM),
                  pl.BlockSpec((tm, d), lambda i: (src(i), 0)),
                  pl.BlockSpec((tm, LANES), lambda i: (src(i), 0)),
                  pl.BlockSpec((1, d), lambda i: (0, 0)),
                  pl.BlockSpec(memory_space=pl.ANY)],
        out_specs=pl.BlockSpec((tm, d), lambda i: (i, 0)),
        out_shape=jax.ShapeDtypeStruct((bsz * s_len, d), F32),
        scratch_shapes=[pltpu.VMEM((TOP_K, tm, d), F32), pltpu.SemaphoreType.DMA(())],
        compiler_params=_params(("arbitrary",)),
    )(dest_flat, h1, gate, g_final.reshape(1, d), yb)


def _rope_tables(pos, dim, reps):
    inv = 1.0 / (ROPE_THETA ** (jnp.arange(0, dim, 2, dtype=F32) / dim))
    ang = pos[:, None] * inv[None, :]
    cos = jnp.concatenate([jnp.cos(ang), jnp.cos(ang)], axis=-1)
    sin = jnp.concatenate([-jnp.sin(ang), jnp.sin(ang)], axis=-1)
    return jnp.tile(cos, (1, reps)), jnp.tile(sin, (1, reps))


def kernel(x, meta_tokens, norm_mix_g, w_in, lambda_q1, lambda_k1, lambda_q2, lambda_k2, subln_g, w_up_a, w_up_b, w_out, norm_ffn_g, w_router, b_router, w1, b1, w2, b2, norm_final_g):
    bsz, s_len, d = x.shape
    n_meta = meta_tokens.shape[0]
    depth = w_in.shape[0]
    n_exp = w_router.shape[-1]
    d_ff = w2.shape[-2]
    t_len = s_len + n_meta
    tp = -(-t_len // ROW_BLOCK) * ROW_BLOCK
    front = tp - t_len
    topk = min(TOPK_MAX, t_len // 4)
    rows = bsz * tp
    assert s_len % ROW_BLOCK == 0 and n_exp <= LANES and d % (2 * LANES) == 0

    tm_proj = _pick_tile(tp, (640, 512, 256, 128))
    tt = _pick_tile(tp, (640, 512, 256, 128))
    tm_merge = _pick_tile(tp, (256, 128))

    pos = jnp.arange(tp, dtype=F32) - float(front)
    cos_h, sin_h = _rope_tables(pos, HEAD_DIM, 1)
    cos_i, sin_i = _rope_tables(pos, IDX_DIM, LANES // IDX_DIM)

    meta = jnp.broadcast_to(meta_tokens[None].astype(x.dtype), (bsz, n_meta, d))
    h = jnp.concatenate([jnp.zeros((bsz, front, d), x.dtype), meta, x], axis=1).reshape(rows, d)

    o_qa, o_ka, o_va = 0, A_WIDTH, 2 * A_WIDTH
    o_qi = 3 * A_WIDTH
    o_ki = o_qi + IDX_HEADS * IDX_DIM
    o_wi = o_ki + IDX_DIM
    o_qb = o_wi + IDX_HEADS
    o_kb = o_qb + B_QK_WIDTH
    o_vb = o_kb + B_QK_WIDTH
    o_ga = o_vb + B_V_WIDTH
    o_gb = o_ga + d

    for l in range(depth):
        lambda_init = 0.8 - 0.6 * math.exp(-0.3 * l)
        wl = w_in[l]
        col = lambda o, n: wl[:, o:o + n]
        zeros_half = jnp.zeros((d, LANES - IDX_DIM), wl.dtype)
        w_rope = jnp.concatenate([col(o_qa, A_WIDTH), col(o_ka, A_WIDTH),
                                  col(o_qb, B_QK_WIDTH), col(o_kb, B_QK_WIDTH)], axis=1).astype(BF16)
        w_val = jnp.concatenate([col(o_va, A_WIDTH), col(o_vb, B_V_WIDTH)], axis=1).astype(BF16)
        w_gate = jnp.concatenate([col(o_ga, d), col(o_gb, d)], axis=1).astype(BF16)
        w_idx = jnp.concatenate([col(o_qi, IDX_HEADS * IDX_DIM), col(o_ki, IDX_DIM), zeros_half,
                                 zeros_half, col(o_ki, IDX_DIM)], axis=1).astype(BF16)
        w_wi = jnp.concatenate([col(o_wi, IDX_HEADS), jnp.zeros((d, LANES - IDX_HEADS), wl.dtype)],
                               axis=1).astype(BF16)

        hn = _rmsnorm_call(h, norm_mix_g[l], tm_proj)
        qk = _proj_call(hn, w_rope, BF16, "rope", tm_proj, 512, cos_h, sin_h, HEAD_DIM // 2, tp=tp)
        val = _proj_call(hn, w_val, BF16, "none", tm_proj, 512)
        gates = _proj_call(hn, w_gate, BF16, "sigmoid", tm_proj, 512)
        idx_qk = _proj_call(hn, w_idx, BF16, "rope", tm_proj, 256, cos_i, sin_i, IDX_DIM // 2, tp=tp)
        wi = _proj_call(hn, w_wi, F32, "scale", tm_proj, LANES,
                        scale=IDX_HEADS ** -0.5 * IDX_DIM ** -0.5)

        mask = _indexer_call(idx_qk, wi, bsz, tp, front, topk)
        o_a = _attn_a_call(qk, val, mask, bsz, tp, tt)
        o_b = _attn_b_call(qk, val, lambda_q1[l], lambda_k1[l], lambda_q2[l], lambda_k2[l], subln_g[l],
                           bsz, tp, tt, front, lambda_init)

        wr = jnp.concatenate([w_router[l], jnp.zeros((d, LANES - n_exp), F32)], axis=1).astype(BF16)
        br = jnp.concatenate([b_router[l], jnp.full((LANES - n_exp,), NEG_BIG, F32)]).reshape(1, LANES)
        h1, hfp, route, gate, cnt = _merge_router_call(
            o_a, o_b, gates, h, w_up_a[l].astype(BF16), w_up_b[l].astype(BF16), w_out[l].astype(BF16),
            norm_ffn_g[l].reshape(1, d), wr, br, tp, front, tm_merge)

        n_assign = bsz * t_len * TOP_K
        n_blocks = (n_assign + n_exp * (ROW_BLOCK - 1) + ROW_BLOCK - 1) // ROW_BLOCK
        n_rows = n_blocks * ROW_BLOCK
        counts = cnt[0, :n_exp].astype(I32)
        padded = (counts + ROW_BLOCK - 1) // ROW_BLOCK * ROW_BLOCK
        pad_end = jnp.cumsum(padded)
        pad_start = pad_end - padded
        n_used = (pad_end[-1:] // ROW_BLOCK).astype(I32)
        block_e = jnp.minimum(jnp.searchsorted(pad_end, jnp.arange(n_blocks, dtype=I32) * ROW_BLOCK,
                                               side="right"), n_exp - 1).astype(I32)
        top_e = route[:, :TOP_K]
        rank = route[:, TOP_K:2 * TOP_K]
        is_pad = (jnp.arange(rows, dtype=I32) % tp) < front
        n_trash = bsz * front * TOP_K
        trash = n_rows + jnp.cumsum(jnp.broadcast_to(is_pad[:, None], (rows, TOP_K)).reshape(-1).astype(I32)) - 1
        dest = jnp.where(is_pad[:, None], trash.reshape(rows, TOP_K), pad_start[top_e] + rank)
        dest_flat = dest.reshape(-1).astype(I32)
        n_rows_total = n_rows + -(-max(n_trash, 1) // 8) * 8

        xb = _dispatch_call(dest_flat, hfp, n_rows_total, tm_merge)

        w1l = w1[l]
        w1p = jnp.concatenate([w1l[..., 0::2], w1l[..., 1::2]], axis=-1).astype(BF16)
        b1p = jnp.concatenate([b1[l][..., 0::2], b1[l][..., 1::2]], axis=-1)
        yb = _expert_call(block_e, n_used, xb, w1p, b1p, w2[l].astype(BF16), b2[l], n_blocks)

        if l + 1 < depth:
            raise NotImplementedError("only the single-layer configuration is implemented")
        out = _combine_call(dest_flat, h1, gate, norm_final_g, yb, bsz, s_len, tp)
    return out.reshape(bsz, s_len, d)
```

```python
import functools
import math

import jax
import jax.numpy as jnp
import numpy as np
from jax import lax
from jax.experimental import pallas as pl
from jax.experimental.pallas import tpu as pltpu

HEAD_DIM = 128
A_HEADS = 8
IDX_HEADS = 16
IDX_DIM = 64
TOPK_MAX = 256
B_HEADS = 4
TOP_K = 4
SWIGLU_LIMIT = 7.0
SWIGLU_ALPHA = 1.702
ROPE_THETA = 10000.0
RMS_EPS = 1e-5
ROW_BLOCK = 128
LANES = 128
NEG_BIG = -1e30
VMEM_LIMIT = 56 * 1024 * 1024

A_WIDTH = A_HEADS * HEAD_DIM
B_QK_WIDTH = B_HEADS * 2 * HEAD_DIM
B_V_WIDTH = B_HEADS * 2 * HEAD_DIM

F32 = jnp.float32
BF16 = jnp.bfloat16
I32 = jnp.int32

KEY_NEG_INF = int(np.array(0xFF800000, np.uint32).view(np.int32)) ^ 0x7FFFFFFF
INT_MIN = -(2 ** 31)


def _pick_tile(n, candidates):
    for c in candidates:
        if n % c == 0:
            return c
    raise ValueError(f"no tile for {n}")


def _params(sem, vmem=VMEM_LIMIT):
    return pltpu.CompilerParams(dimension_semantics=sem, vmem_limit_bytes=vmem)


def _rmsnorm_kernel(x_ref, g_ref, o_ref):
    x = x_ref[...]
    ms = jnp.mean(x * x, axis=-1, keepdims=True)
    o_ref[...] = (x * lax.rsqrt(ms + RMS_EPS) * g_ref[...]).astype(o_ref.dtype)


def _rmsnorm_call(x2d, g, tm):
    rows, d = x2d.shape
    return pl.pallas_call(
        _rmsnorm_kernel,
        grid=(rows // tm,),
        in_specs=[pl.BlockSpec((tm, d), lambda i: (i, 0)),
                  pl.BlockSpec((1, d), lambda i: (0, 0))],
        out_specs=pl.BlockSpec((tm, d), lambda i: (i, 0)),
        out_shape=jax.ShapeDtypeStruct((rows, d), BF16),
        compiler_params=_params(("parallel",)),
    )(x2d, g.reshape(1, d))


def _proj_kernel(*refs, mode, rot_half, scale):
    if mode == "rope":
        x_ref, w_ref, cos_ref, sin_ref, o_ref = refs
    else:
        x_ref, w_ref, o_ref = refs
    acc = jnp.dot(x_ref[...], w_ref[...], preferred_element_type=F32)
    if mode == "rope":
        cos = cos_ref[...]
        sin = sin_ref[...]
        tn = acc.shape[1]
        lane = lax.broadcasted_iota(I32, (acc.shape[0], LANES), 1)
        outs = []
        for c in range(tn // LANES):
            xh = acc[:, c * LANES:(c + 1) * LANES]
            if rot_half == LANES // 2:
                rot = pltpu.roll(xh, LANES // 2, 1)
            else:
                fwd = pltpu.roll(xh, LANES - rot_half, 1)
                bwd = pltpu.roll(xh, rot_half, 1)
                rot = jnp.where((lane % (2 * rot_half)) < rot_half, fwd, bwd)
            outs.append(xh * cos + rot * sin)
        acc = outs[0] if len(outs) == 1 else jnp.concatenate(outs, axis=1)
    elif mode == "sigmoid":
        acc = jax.nn.sigmoid(acc)
    elif mode == "scale":
        acc = acc * scale
    o_ref[...] = acc.astype(o_ref.dtype)


def _proj_call(xn, w, out_dtype, mode, tm, tn, cos=None, sin=None, rot_half=0, scale=1.0, tp=None):
    rows, d = xn.shape
    n = w.shape[1]
    in_specs = [pl.BlockSpec((tm, d), lambda i, j: (i, 0)),
                pl.BlockSpec((d, tn), lambda i, j: (0, j))]
    args = [xn, w]
    if mode == "rope":
        nt = tp // tm
        in_specs += [pl.BlockSpec((tm, LANES), lambda i, j: (i % nt, 0)),
                     pl.BlockSpec((tm, LANES), lambda i, j: (i % nt, 0))]
        args += [cos, sin]
    return pl.pallas_call(
        functools.partial(_proj_kernel, mode=mode, rot_half=rot_half, scale=scale),
        grid=(rows // tm, n // tn),
        in_specs=in_specs,
        out_specs=pl.BlockSpec((tm, tn), lambda i, j: (i, j)),
        out_shape=jax.ShapeDtypeStruct((rows, n), out_dtype),
        compiler_params=_params(("parallel", "arbitrary")),
    )(*args)


def _indexer_kernel(qi_ref, kc_ref, wi_ref, mask_ref, keys_ref, qs_ref, wb_ref, p_ref,
                    *, front, topk, n_chunks_total):
    i = pl.program_id(1)
    n_chunk = i + 1
    n_pair = (n_chunk + 1) // 2
    tq = qi_ref.shape[0]
    n_hp = IDX_HEADS // 2
    q_idx = i * tq + lax.broadcasted_iota(I32, (tq, LANES), 0)
    lane = lax.broadcasted_iota(I32, (tq, LANES), 1)

    w = wi_ref[...]
    for h in range(IDX_HEADS):
        wb_ref[h] = jnp.broadcast_to(w[:, h:h + 1], (tq, LANES))
    for p in range(n_hp):
        qs_ref[p * tq:(p + 1) * tq, :] = qi_ref[:, p * LANES:(p + 1) * LANES]

    nt_dims = (((1,), (1,)), ((), ()))

    def score_body(c, carry):
        off = pl.multiple_of(c * LANES, LANES)
        kc = kc_ref[pl.ds(pl.multiple_of(c * 2 * LANES, 2 * LANES), 2 * LANES), :]
        s = lax.dot_general(qs_ref[...], kc, nt_dims, preferred_element_type=F32)
        acc = jnp.zeros((tq, LANES), F32)
        for p in range(n_hp):
            sp = s[p * tq:(p + 1) * tq]
            acc = acc + wb_ref[2 * p] * jnp.maximum(sp[:, :LANES], 0.0)
            acc = acc + wb_ref[2 * p + 1] * jnp.maximum(sp[:, LANES:], 0.0)
        k_idx = off + lane
        valid = (k_idx <= q_idx) & (k_idx >= front)
        acc = jnp.where(acc == 0.0, 0.0, acc)
        score = jnp.where(valid, acc, -jnp.inf)
        bits = pltpu.bitcast(score, I32)
        keys_ref[:, pl.ds(off, LANES)] = jnp.where(bits < 0, bits ^ 0x7FFFFFFF, bits)
        return carry

    lax.fori_loop(0, n_chunk, score_body, 0)
    keys_ref[:, pl.ds(pl.multiple_of(n_chunk * LANES, LANES), LANES)] = jnp.full((tq, LANES), INT_MIN, I32)

    def count(pred_fn):
        def body(c2, cnt):
            off = pl.multiple_of(c2 * 2 * LANES, 2 * LANES)
            k0 = keys_ref[:, pl.ds(off, LANES)]
            k1 = keys_ref[:, pl.ds(off + LANES, LANES)]
            cnt = cnt + jnp.where(pred_fn(k0, off + lane), 1, 0)
            return cnt + jnp.where(pred_fn(k1, off + LANES + lane), 1, 0)
        cnt = lax.fori_loop(0, n_pair, body, jnp.zeros((tq, LANES), I32))
        return jnp.sum(cnt, axis=1, keepdims=True)

    def bit_body(it, t):
        trial = t + jnp.left_shift(jnp.int32(1), 31 - it)
        cnt = count(lambda k, _: k >= trial)
        return jnp.where(cnt >= topk, trial, t)

    thr = lax.fori_loop(0, 32, bit_body, jnp.full((tq, 1), INT_MIN, I32))

    cnt_ge = count(lambda k, _: k >= thr)
    finite = thr > KEY_NEG_INF
    tie_rows = finite & (cnt_ge > topk)
    p_ref[...] = jnp.full((tq, LANES), (n_chunks_total + 1) * LANES, I32)

    @pl.when(jnp.max(jnp.where(tie_rows, 1, 0)) > 0)
    def _():
        cnt_gt = count(lambda k, _: k > thr)
        need = topk - cnt_gt

        def idx_body(it, p):
            trial = p + jnp.left_shift(jnp.int32(1), 14 - it)
            cnt = count(lambda k, idx: (k == thr) & (idx < trial))
            return jnp.where(cnt < need, trial, p)

        p = lax.fori_loop(0, 15, idx_body, jnp.zeros((tq, 1), I32))
        p = jnp.where(tie_rows, p, (n_chunks_total + 1) * LANES)
        p_ref[...] = jnp.broadcast_to(p, (tq, LANES))

    p_lim = p_ref[...]

    def write_body(c, carry):
        off = pl.multiple_of(c * LANES, LANES)
        k = keys_ref[:, pl.ds(off, LANES)]
        k_idx = off + lane
        sel = (k > thr) | ((k == thr) & (k_idx <= p_lim))
        valid = (k_idx <= q_idx) & (k_idx >= front)
        mask_ref[:, pl.ds(off, LANES)] = jnp.where(sel & valid, 1, 0).astype(jnp.int8)
        return carry

    lax.fori_loop(0, n_chunk, write_body, 0)

    def zero_body(c, carry):
        off = pl.multiple_of(c * LANES, LANES)
        mask_ref[:, pl.ds(off, LANES)] = jnp.zeros((tq, LANES), jnp.int8)
        return carry

    lax.fori_loop(n_chunk, n_chunks_total, zero_body, 0)


def _indexer_call(idx_qk, k_cat, wi, bsz, tp, front, topk):
    tq = ROW_BLOCK
    nq = tp // tq
    n_qcols = IDX_HEADS * IDX_DIM
    return pl.pallas_call(
        functools.partial(_indexer_kernel, front=front, topk=topk, n_chunks_total=nq),
        grid=(bsz, nq),
        in_specs=[pl.BlockSpec((tq, n_qcols), lambda b, i: (b * nq + i, 0)),
                  pl.BlockSpec((2 * tp, LANES), lambda b, i: (b, 0)),
                  pl.BlockSpec((tq, LANES), lambda b, i: (b * nq + i, 0))],
        out_specs=pl.BlockSpec((tq, tp), lambda b, i: (b * nq + i, 0)),
        out_shape=jax.ShapeDtypeStruct((bsz * tp, tp), jnp.int8),
        scratch_shapes=[pltpu.VMEM((tq, tp + LANES), I32),
                        pltpu.VMEM((n_qcols // LANES * tq, LANES), BF16),
                        pltpu.VMEM((IDX_HEADS, tq, LANES), F32),
                        pltpu.VMEM((tq, LANES), I32)],
        compiler_params=_params(("parallel", "arbitrary")),
    )(idx_qk, k_cat, wi)


def _attn_a_kernel(q_ref, k_ref, v_ref, mask_ref, o_ref, m_ref, l_ref, acc_ref):
    i = pl.program_id(1)
    j = pl.program_id(2)
    scale = HEAD_DIM ** -0.5
    nt_dims = (((1,), (1,)), ((), ()))

    @pl.when(j == 0)
    def _():
        m_ref[...] = jnp.full(m_ref.shape, NEG_BIG, F32)
        l_ref[...] = jnp.zeros(l_ref.shape, F32)
        acc_ref[...] = jnp.zeros(acc_ref.shape, F32)

    @pl.when(j <= i)
    def _():
        keep = mask_ref[...].astype(I32) != 0
        for h in range(A_HEADS):
            sl = slice(h * HEAD_DIM, (h + 1) * HEAD_DIM)
            s = lax.dot_general(q_ref[:, sl], k_ref[:, sl], nt_dims, preferred_element_type=F32) * scale
            s = jnp.where(keep, s, NEG_BIG)
            m_prev = m_ref[h]
            m_new = jnp.maximum(m_prev, jnp.max(s, axis=1, keepdims=True))
            alpha = jnp.exp(m_prev - m_new)
            p = jnp.exp(s - m_new)
            l_ref[h] = alpha * l_ref[h] + jnp.sum(p, axis=1, keepdims=True)
            acc_ref[:, sl] = alpha * acc_ref[:, sl] + jnp.dot(p.astype(BF16), v_ref[:, sl],
                                                              preferred_element_type=F32)
            m_ref[h] = m_new

    @pl.when(j == i)
    def _():
        for h in range(A_HEADS):
            sl = slice(h * HEAD_DIM, (h + 1) * HEAD_DIM)
            o_ref[:, sl] = (acc_ref[:, sl] / l_ref[h]).astype(o_ref.dtype)


def _attn_a_call(qk, v, mask, bsz, tp, tt):
    nb = tp // tt
    w = A_WIDTH
    return pl.pallas_call(
        _attn_a_kernel,
        grid=(bsz, nb, nb),
        in_specs=[pl.BlockSpec((tt, w), lambda b, i, j: (b * nb + i, 0)),
                  pl.BlockSpec((tt, w), lambda b, i, j: (b * nb + jnp.minimum(i, j), 1)),
                  pl.BlockSpec((tt, w), lambda b, i, j: (b * nb + jnp.minimum(i, j), 0)),
                  pl.BlockSpec((tt, tt), lambda b, i, j: (b * nb + i, jnp.minimum(i, j)))],
        out_specs=pl.BlockSpec((tt, w), lambda b, i, j: (b * nb + i, 0)),
        out_shape=jax.ShapeDtypeStruct((bsz * tp, w), BF16),
        scratch_shapes=[pltpu.VMEM((A_HEADS, tt, 1), F32),
                        pltpu.VMEM((A_HEADS, tt, 1), F32),
                        pltpu.VMEM((tt, w), F32)],
        compiler_params=_params(("parallel", "parallel", "arbitrary")),
    )(qk, qk, v, mask)


def _attn_b_kernel(q_ref, k_ref, v_ref, lq1_ref, lk1_ref, lq2_ref, lk2_ref, g_ref, o_ref,
                   m_ref, l_ref, acc_ref, *, front, lambda_init):
    i = pl.program_id(1)
    j = pl.program_id(2)
    tt = q_ref.shape[0]
    scale = HEAD_DIM ** -0.5
    nt_dims = (((1,), (1,)), ((), ()))
    dv = 2 * HEAD_DIM

    @pl.when(j == 0)
    def _():
        m_ref[...] = jnp.full(m_ref.shape, NEG_BIG, F32)
        l_ref[...] = jnp.zeros(l_ref.shape, F32)
        acc_ref[...] = jnp.zeros(acc_ref.shape, F32)

    @pl.when(j <= i)
    def _():
        q_idx = i * tt + lax.broadcasted_iota(I32, (tt, tt), 0)
        k_idx = j * tt + lax.broadcasted_iota(I32, (tt, tt), 1)
        keep = (k_idx <= q_idx) & (k_idx >= front)
        for h in range(B_HEADS):
            vh = v_ref[:, h * dv:(h + 1) * dv]
            for part in range(2):
                c = 2 * h + part
                sl = slice(c * HEAD_DIM, (c + 1) * HEAD_DIM)
                s = lax.dot_general(q_ref[:, sl], k_ref[:, sl], nt_dims, preferred_element_type=F32) * scale
                s = jnp.where(keep, s, NEG_BIG)
                m_prev = m_ref[c]
                m_new = jnp.maximum(m_prev, jnp.max(s, axis=1, keepdims=True))
                alpha = jnp.exp(m_prev - m_new)
                p = jnp.exp(s - m_new)
                l_ref[c] = alpha * l_ref[c] + jnp.sum(p, axis=1, keepdims=True)
                acc_ref[c] = alpha * acc_ref[c] + jnp.dot(p.astype(BF16), vh, preferred_element_type=F32)
                m_ref[c] = m_new

    @pl.when(j == i)
    def _():
        lam = (jnp.exp(jnp.sum(lq1_ref[...] * lk1_ref[...], axis=1, keepdims=True))
               - jnp.exp(jnp.sum(lq2_ref[...] * lk2_ref[...], axis=1, keepdims=True))
               + lambda_init)
        g = g_ref[...]
        for h in range(B_HEADS):
            o = acc_ref[2 * h] / l_ref[2 * h] - lam * (acc_ref[2 * h + 1] / l_ref[2 * h + 1])
            ms = jnp.mean(o * o, axis=-1, keepdims=True)
            y = o * lax.rsqrt(ms + RMS_EPS) * g
            o_ref[:, h * dv:(h + 1) * dv] = (y * (1.0 - lambda_init)).astype(o_ref.dtype)


def _attn_b_call(qk, v, lq1, lk1, lq2, lk2, subln_g, bsz, tp, tt, front, lambda_init):
    nb = tp // tt
    w = B_QK_WIDTH
    dv = 2 * HEAD_DIM
    vec = lambda a: a.reshape(1, -1).astype(F32)
    small = lambda n: pl.BlockSpec((1, n), lambda b, i, j: (0, 0))
    return pl.pallas_call(
        functools.partial(_attn_b_kernel, front=front, lambda_init=lambda_init),
        grid=(bsz, nb, nb),
        in_specs=[pl.BlockSpec((tt, w), lambda b, i, j: (b * nb + i, 2)),
                  pl.BlockSpec((tt, w), lambda b, i, j: (b * nb + jnp.minimum(i, j), 3)),
                  pl.BlockSpec((tt, B_V_WIDTH), lambda b, i, j: (b * nb + jnp.minimum(i, j), 1)),
                  small(HEAD_DIM), small(HEAD_DIM), small(HEAD_DIM), small(HEAD_DIM), small(dv)],
        out_specs=pl.BlockSpec((tt, B_V_WIDTH), lambda b, i, j: (b * nb + i, 0)),
        out_shape=jax.ShapeDtypeStruct((bsz * tp, B_V_WIDTH), BF16),
        scratch_shapes=[pltpu.VMEM((2 * B_HEADS, tt, 1), F32),
                        pltpu.VMEM((2 * B_HEADS, tt, 1), F32),
                        pltpu.VMEM((2 * B_HEADS, tt, dv), F32)],
        compiler_params=_params(("parallel", "parallel", "arbitrary")),
    )(qk, qk, v, vec(lq1), vec(lk1), vec(lq2), vec(lk2), vec(subln_g))


def _merge_router_kernel(oa_ref, ob_ref, ga_ref, gb_ref, h_ref, wua_ref, wub_ref, wo_ref, g_ref,
                         wr_ref, br_ref, h1_ref, hfp_ref, route_ref, gate_ref, cnt_ref, carry_ref,
                         *, tp, front):
    i = pl.program_id(0)
    tm = oa_ref.shape[0]
    d = h_ref.shape[1]

    @pl.when(i == 0)
    def _():
        carry_ref[...] = jnp.zeros(carry_ref.shape, F32)

    ua = jnp.dot(oa_ref[...], wua_ref[...], preferred_element_type=F32)
    ub = jnp.dot(ob_ref[...], wub_ref[...], preferred_element_type=F32)
    merged = ga_ref[...].astype(F32) * ua + gb_ref[...].astype(F32) * ub
    mix = jnp.dot(merged.astype(BF16), wo_ref[...], preferred_element_type=F32)
    h1 = h_ref[...] + mix
    h1_ref[...] = h1

    ms = jnp.mean(h1 * h1, axis=-1, keepdims=True)
    hf = (h1 * lax.rsqrt(ms + RMS_EPS) * g_ref[...]).astype(BF16)

    hf_bits = pltpu.bitcast(hf.astype(F32), jnp.uint32)
    hfp_ref[...] = (hf_bits[:, :d // 2] & jnp.uint32(0xFFFF0000)) | (hf_bits[:, d // 2:] >> 16)

    logits = jnp.dot(hf, wr_ref[...], preferred_element_type=F32) + br_ref[...]
    lane = lax.broadcasted_iota(I32, (tm, LANES), 1)
    row = (i * tm + lax.broadcasted_iota(I32, (tm, 1), 0)) % tp
    routed = row >= front

    work = logits
    vals, idxs, onehots = [], [], []
    for _ in range(TOP_K):
        v = jnp.max(work, axis=1, keepdims=True)
        e = jnp.min(jnp.where(work == v, lane, LANES), axis=1, keepdims=True)
        hit = lane == e
        vals.append(v)
        idxs.append(e)
        onehots.append(hit)
        work = jnp.where(hit, -jnp.inf, work)

    exps = [jnp.exp(v - vals[0]) for v in vals]
    denom = exps[0] + exps[1] + exps[2] + exps[3]

    member = jnp.zeros((tm, LANES), F32)
    for hit in onehots:
        member = member + jnp.where(hit & routed, 1.0, 0.0)
    r_i = lax.broadcasted_iota(I32, (tm, tm), 0)
    c_i = lax.broadcasted_iota(I32, (tm, tm), 1)
    lower = jnp.where(c_i < r_i, 1.0, 0.0).astype(BF16)
    before = jnp.dot(lower, member.astype(BF16), preferred_element_type=F32) + carry_ref[...]
    carry_ref[...] = carry_ref[...] + jnp.sum(member, axis=0, keepdims=True)

    route = jnp.zeros((tm, LANES), I32)
    gates = jnp.zeros((tm, LANES), F32)
    for k in range(TOP_K):
        rank = jnp.sum(jnp.where(onehots[k], before, 0.0), axis=1, keepdims=True)
        route = jnp.where(lane == k, idxs[k], route)
        route = jnp.where(lane == TOP_K + k, rank.astype(I32), route)
        gates = jnp.where(lane == k, exps[k] / denom, gates)
    route_ref[...] = route
    gate_ref[...] = gates
    cnt_ref[...] = jnp.broadcast_to(carry_ref[...], cnt_ref.shape)


def _merge_router_call(o_a, o_b, gates, h0, wua, wub, wo, g_ffn, wr, br, tp, front, tm):
    rows, d = h0.shape
    nt = rows // tm
    once = pl.Buffered(1)
    const = lambda shape: pl.BlockSpec(shape, lambda i: (0, 0), pipeline_mode=once)
    return pl.pallas_call(
        functools.partial(_merge_router_kernel, tp=tp, front=front),
        grid=(nt,),
        in_specs=[pl.BlockSpec((tm, A_WIDTH), lambda i: (i, 0)),
                  pl.BlockSpec((tm, B_V_WIDTH), lambda i: (i, 0)),
                  pl.BlockSpec((tm, d), lambda i: (i, 0)),
                  pl.BlockSpec((tm, d), lambda i: (i, 1)),
                  pl.BlockSpec((tm, d), lambda i: (i, 0)),
                  const((A_WIDTH, d)), const((B_V_WIDTH, d)), const((d, d)),
                  const((1, d)), const((d, LANES)), const((1, LANES))],
        out_specs=[pl.BlockSpec((tm, d), lambda i: (i, 0)),
                   pl.BlockSpec((tm, d // 2), lambda i: (i, 0)),
                   pl.BlockSpec((tm, LANES), lambda i: (i, 0)),
                   pl.BlockSpec((tm, LANES), lambda i: (i, 0)),
                   pl.BlockSpec((8, LANES), lambda i: (0, 0))],
        out_shape=[jax.ShapeDtypeStruct((rows, d), F32),
                   jax.ShapeDtypeStruct((rows, d // 2), jnp.uint32),
                   jax.ShapeDtypeStruct((rows, LANES), I32),
                   jax.ShapeDtypeStruct((rows, LANES), F32),
                   jax.ShapeDtypeStruct((8, LANES), F32)],
        scratch_shapes=[pltpu.VMEM((1, LANES), F32)],
        compiler_params=_params(("arbitrary",)),
    )(o_a, o_b, gates, gates, h0, wua, wub, wo, g_ffn, wr, br)


def _dispatch_kernel(dest_ref, hfp_ref, xb_in_ref, xb_ref, sem):
    del xb_in_ref
    tm = hfp_ref.shape[0]

    def row_copy(r, k):
        d = dest_ref[r * TOP_K + k]
        return pltpu.make_async_copy(hfp_ref.at[pl.ds(r, 1), :], xb_ref.at[pl.ds(d, 1), :], sem)

    def issue(r, carry):
        for k in range(TOP_K):
            row_copy(r, k).start()
        return carry

    lax.fori_loop(0, tm, issue, 0)

    def drain(r, carry):
        for k in range(TOP_K):
            row_copy(r, k).wait()
        return carry

    lax.fori_loop(0, tm, drain, 0)


def _dispatch_call(dest_flat, hfp, n_rows_total, tm):
    rows, half = hfp.shape
    xb0 = jnp.zeros((n_rows_total, half), jnp.uint32)
    return pl.pallas_call(
        _dispatch_kernel,
        grid=(rows // tm,),
        in_specs=[pl.BlockSpec((tm * TOP_K,), lambda i: (i,), memory_space=pltpu.SMEM),
                  pl.BlockSpec((tm, half), lambda i: (i, 0)),
                  pl.BlockSpec(memory_space=pl.ANY)],
        out_specs=pl.BlockSpec(memory_space=pl.ANY),
        out_shape=jax.ShapeDtypeStruct((n_rows_total, half), jnp.uint32),
        scratch_shapes=[pltpu.SemaphoreType.DMA(())],
        input_output_aliases={2: 0},
        compiler_params=_params(("arbitrary",)),
    )(dest_flat, hfp, xb0)


def _w1_prep_kernel(w_ref, o_ref):
    n = 2 * LANES
    s_i = lax.broadcasted_iota(I32, (n, n), 0)
    j_i = lax.broadcasted_iota(I32, (n, n), 1)
    src = jnp.where(j_i < LANES, 2 * j_i, 2 * (j_i - LANES) + 1)
    perm = jnp.where(s_i == src, 1.0, 0.0).astype(BF16)
    for c in range(w_ref.shape[2] // n):
        blk = w_ref[0, :, c * n:(c + 1) * n].astype(BF16)
        o_ref[0, :, c * n:(c + 1) * n] = jnp.dot(blk, perm, preferred_element_type=F32).astype(BF16)


def _w1_prep_call(w1l, tk):
    n_exp, d, two_ff = w1l.shape
    return pl.pallas_call(
        _w1_prep_kernel,
        grid=(n_exp, d // tk),
        in_specs=[pl.BlockSpec((1, tk, two_ff), lambda e, k: (e, k, 0))],
        out_specs=pl.BlockSpec((1, tk, two_ff), lambda e, k: (e, k, 0)),
        out_shape=jax.ShapeDtypeStruct((n_exp, d, two_ff), BF16),
        compiler_params=_params(("parallel", "parallel")),
    )(w1l)


def _expert_kernel(be_ref, nu_ref, xb_ref, w1_ref, b1_ref, w2_ref, b2_ref, y_ref):
    i = pl.program_id(0)
    d_ff = w2_ref.shape[1]

    @pl.when(i < nu_ref[0])
    def _():
        packed = xb_ref[...]
        hi = pltpu.bitcast(packed & jnp.uint32(0xFFFF0000), F32)
        lo = pltpu.bitcast(packed << 16, F32)
        x = jnp.concatenate([hi, lo], axis=1).astype(BF16)
        u = jnp.dot(x, w1_ref[0], preferred_element_type=F32) + b1_ref[0]
        n_grp = d_ff // LANES
        glu = jnp.concatenate([u[:, 2 * c * LANES:(2 * c + 1) * LANES] for c in range(n_grp)], axis=1)
        lin = jnp.concatenate([u[:, (2 * c + 1) * LANES:(2 * c + 2) * LANES] for c in range(n_grp)], axis=1)
        x_glu = jnp.minimum(glu, SWIGLU_LIMIT)
        x_lin = jnp.clip(lin, -SWIGLU_LIMIT, SWIGLU_LIMIT)
        act = x_glu * jax.nn.sigmoid(SWIGLU_ALPHA * x_glu) * (x_lin + 1.0)
        y_ref[...] = jnp.dot(act.astype(BF16), w2_ref[0], preferred_element_type=F32) + b2_ref[0]

    @pl.when(i >= nu_ref[0])
    def _():
        y_ref[...] = jnp.zeros(y_ref.shape, F32)


def _expert_call(block_e, n_used, xb, w1p, b1p, w2b, b2, n_blocks):
    n_exp, d, two_ff = w1p.shape
    d_ff = two_ff // 2
    half = xb.shape[1]
    once = pl.Buffered(1)

    def blk(i, be, nu):
        return jnp.minimum(i, nu[0] - 1)

    grid_spec = pltpu.PrefetchScalarGridSpec(
        num_scalar_prefetch=2,
        grid=(n_blocks,),
        in_specs=[pl.BlockSpec((ROW_BLOCK, half), lambda i, be, nu: (blk(i, be, nu), 0)),
                  pl.BlockSpec((1, d, two_ff), lambda i, be, nu: (be[blk(i, be, nu)], 0, 0), pipeline_mode=once),
                  pl.BlockSpec((1, 1, two_ff), lambda i, be, nu: (be[blk(i, be, nu)], 0, 0)),
                  pl.BlockSpec((1, d_ff, d), lambda i, be, nu: (be[blk(i, be, nu)], 0, 0), pipeline_mode=once),
                  pl.BlockSpec((1, 1, d), lambda i, be, nu: (be[blk(i, be, nu)], 0, 0))],
        out_specs=pl.BlockSpec((ROW_BLOCK, d), lambda i, be, nu: (i, 0)),
    )
    return pl.pallas_call(
        _expert_kernel,
        grid_spec=grid_spec,
        out_shape=jax.ShapeDtypeStruct((n_blocks * ROW_BLOCK, d), F32),
        compiler_params=_params(("arbitrary",)),
    )(block_e, n_used, xb, w1p, b1p.reshape(n_exp, 1, two_ff), w2b, b2.reshape(n_exp, 1, d))


def _combine_kernel(dest_ref, h1_ref, gate_ref, g_ref, yb_ref, o_ref, ybuf_ref, sem):
    tm = h1_ref.shape[0]

    def row_copy(r, k):
        d = dest_ref[r * TOP_K + k]
        return pltpu.make_async_copy(yb_ref.at[pl.ds(d, 1), :], ybuf_ref.at[k, pl.ds(r, 1), :], sem)

    def issue(r, carry):
        for k in range(TOP_K):
            row_copy(r, k).start()
        return carry

    lax.fori_loop(0, tm, issue, 0)

    def drain(r, carry):
        for k in range(TOP_K):
            row_copy(r, k).wait()
        return carry

    lax.fori_loop(0, tm, drain, 0)

    gate = gate_ref[...]
    h2 = h1_ref[...]
    for k in range(TOP_K):
        h2 = h2 + gate[:, k:k + 1] * ybuf_ref[k]
    ms = jnp.mean(h2 * h2, axis=-1, keepdims=True)
    o_ref[...] = h2 * lax.rsqrt(ms + RMS_EPS) * g_ref[...]


def _combine_call(dest_flat, h1, gate, g_final, yb, bsz, s_len, tp):
    d = h1.shape[1]
    tm = ROW_BLOCK
    ns = s_len // tm
    nb = tp // tm
    skip = (tp - s_len) // tm

    def src(i):
        return (i // ns) * nb + skip + i % ns

    return pl.pallas_call(
        _combine_kernel,
        grid=(bsz * ns,),
        in_specs=[pl.BlockSpec((tm * TOP_K,), lambda i: (src(i),), memory_space=pltpu.SMEM),
                  pl.BlockSpec((tm, d), lambda i: (src(i), 0)),
                  pl.BlockSpec((tm, LANES), lambda i: (src(i), 0)),
                  pl.BlockSpec((1, d), lambda i: (0, 0)),
                  pl.BlockSpec(memory_space=pl.ANY)],
        out_specs=pl.BlockSpec((tm, d), lambda i: (i, 0)),
        out_shape=jax.ShapeDtypeStruct((bsz * s_len, d), F32),
        scratch_shapes=[pltpu.VMEM((TOP_K, tm, d), F32), pltpu.SemaphoreType.DMA(())],
        compiler_params=_params(("arbitrary",)),
    )(dest_flat, h1, gate, g_final.reshape(1, d), yb)


def _rope_tables(pos, dim, reps):
    inv = 1.0 / (ROPE_THETA ** (jnp.arange(0, dim, 2, dtype=F32) / dim))
    ang = pos[:, None] * inv[None, :]
    cos = jnp.concatenate([jnp.cos(ang), jnp.cos(ang)], axis=-1)
    sin = jnp.concatenate([-jnp.sin(ang), jnp.sin(ang)], axis=-1)
    return jnp.tile(cos, (1, reps)), jnp.tile(sin, (1, reps))


def kernel(x, meta_tokens, norm_mix_g, w_in, lambda_q1, lambda_k1, lambda_q2, lambda_k2, subln_g, w_up_a, w_up_b, w_out, norm_ffn_g, w_router, b_router, w1, b1, w2, b2, norm_final_g):
    bsz, s_len, d = x.shape
    n_meta = meta_tokens.shape[0]
    depth = w_in.shape[0]
    n_exp = w_router.shape[-1]
    d_ff = w2.shape[-2]
    t_len = s_len + n_meta
    tp = -(-t_len // ROW_BLOCK) * ROW_BLOCK
    front = tp - t_len
    topk = min(TOPK_MAX, t_len // 4)
    rows = bsz * tp
    assert s_len % ROW_BLOCK == 0 and n_exp <= LANES and d % (2 * LANES) == 0 and d_ff % LANES == 0

    tm_proj = _pick_tile(tp, (640, 512, 256, 128))
    tt = _pick_tile(tp, (640, 512, 256, 128))
    tm_merge = _pick_tile(rows, (256, 128))

    pos = jnp.arange(tp, dtype=F32) - float(front)
    cos_h, sin_h = _rope_tables(pos, HEAD_DIM, 1)
    cos_i, sin_i = _rope_tables(pos, IDX_DIM, LANES // IDX_DIM)

    meta = jnp.broadcast_to(meta_tokens[None].astype(x.dtype), (bsz, n_meta, d))
    h = jnp.concatenate([jnp.zeros((bsz, front, d), x.dtype), meta, x], axis=1).reshape(rows, d)

    o_qa, o_ka, o_va = 0, A_WIDTH, 2 * A_WIDTH
    o_qi = 3 * A_WIDTH
    o_ki = o_qi + IDX_HEADS * IDX_DIM
    o_wi = o_ki + IDX_DIM
    o_qb = o_wi + IDX_HEADS
    o_kb = o_qb + B_QK_WIDTH
    o_vb = o_kb + B_QK_WIDTH
    o_ga = o_vb + B_V_WIDTH
    o_gb = o_ga + d

    for l in range(depth):
        lambda_init = 0.8 - 0.6 * math.exp(-0.3 * l)
        wl = w_in[l]
        col = lambda o, n: wl[:, o:o + n]
        zeros_half = jnp.zeros((d, LANES - IDX_DIM), wl.dtype)
        w_rope = jnp.concatenate([col(o_qa, A_WIDTH), col(o_ka, A_WIDTH),
                                  col(o_qb, B_QK_WIDTH), col(o_kb, B_QK_WIDTH)], axis=1).astype(BF16)
        w_val = jnp.concatenate([col(o_va, A_WIDTH), col(o_vb, B_V_WIDTH)], axis=1).astype(BF16)
        w_gate = jnp.concatenate([col(o_ga, d), col(o_gb, d)], axis=1).astype(BF16)
        w_idx = jnp.concatenate([col(o_qi, IDX_HEADS * IDX_DIM), col(o_ki, IDX_DIM), zeros_half,
                                 zeros_half, col(o_ki, IDX_DIM)], axis=1).astype(BF16)
        w_wi = jnp.concatenate([col(o_wi, IDX_HEADS), jnp.zeros((d, LANES - IDX_HEADS), wl.dtype)],
                               axis=1).astype(BF16)

        hn = _rmsnorm_call(h, norm_mix_g[l], tm_proj)
        qk = _proj_call(hn, w_rope, BF16, "rope", tm_proj, 512, cos_h, sin_h, HEAD_DIM // 2, tp=tp)
        val = _proj_call(hn, w_val, BF16, "none", tm_proj, 512)
        gates = _proj_call(hn, w_gate, BF16, "sigmoid", tm_proj, 512)
        idx_qk = _proj_call(hn, w_idx, BF16, "rope", tm_proj, 256, cos_i, sin_i, IDX_DIM // 2, tp=tp)
        wi = _proj_call(hn, w_wi, F32, "scale", tm_proj, LANES,
                        scale=IDX_HEADS ** -0.5 * IDX_DIM ** -0.5)

        n_qcols = IDX_HEADS * IDX_DIM
        nc = tp // LANES
        k_cat = jnp.stack([idx_qk[:, n_qcols:n_qcols + LANES].reshape(bsz, nc, LANES, LANES),
                           idx_qk[:, n_qcols + LANES:].reshape(bsz, nc, LANES, LANES)],
                          axis=2).reshape(bsz * 2 * tp, LANES)
        mask = _indexer_call(idx_qk, k_cat, wi, bsz, tp, front, topk)
        o_a = _attn_a_call(qk, val, mask, bsz, tp, tt)
        o_b = _attn_b_call(qk, val, lambda_q1[l], lambda_k1[l], lambda_q2[l], lambda_k2[l], subln_g[l],
                           bsz, tp, tt, front, lambda_init)

        wr = jnp.concatenate([w_router[l], jnp.zeros((d, LANES - n_exp), F32)], axis=1).astype(BF16)
        br = jnp.concatenate([b_router[l], jnp.full((LANES - n_exp,), NEG_BIG, F32)]).reshape(1, LANES)
        h1, hfp, route, gate, cnt = _merge_router_call(
            o_a, o_b, gates, h, w_up_a[l].astype(BF16), w_up_b[l].astype(BF16), w_out[l].astype(BF16),
            norm_ffn_g[l].reshape(1, d), wr, br, tp, front, tm_merge)

        n_assign = bsz * t_len * TOP_K
        n_blocks = (n_assign + n_exp * (ROW_BLOCK - 1) + ROW_BLOCK - 1) // ROW_BLOCK
        n_rows = n_blocks * ROW_BLOCK
        counts = cnt[0, :n_exp].astype(I32)
        padded = (counts + ROW_BLOCK - 1) // ROW_BLOCK * ROW_BLOCK
        pad_end = jnp.cumsum(padded)
        pad_start = pad_end - padded
        n_used = (pad_end[-1:] // ROW_BLOCK).astype(I32)
        blk_start = jnp.arange(n_blocks, dtype=I32) * ROW_BLOCK
        block_e = jnp.minimum(jnp.sum((pad_end[None, :] <= blk_start[:, None]).astype(I32), axis=1),
                              n_exp - 1).astype(I32)
        top_e = route[:, :TOP_K]
        rank = route[:, TOP_K:2 * TOP_K]
        row_id = jnp.arange(rows, dtype=I32)
        is_pad = (row_id % tp) < front
        n_trash = bsz * front * TOP_K
        trash = (n_rows + ((row_id // tp) * front + row_id % tp)[:, None] * TOP_K
                 + jnp.arange(TOP_K, dtype=I32)[None, :])
        dest = jnp.where(is_pad[:, None], trash, pad_start[top_e] + rank)
        dest_flat = dest.reshape(-1).astype(I32)
        n_rows_total = n_rows + -(-max(n_trash, 1) // 8) * 8

        xb = _dispatch_call(dest_flat, hfp, n_rows_total, tm_merge)

        w1p = _w1_prep_call(w1[l], _pick_tile(d, (256, 128)))
        b1p = b1[l].reshape(n_exp, d_ff // LANES, LANES, 2).transpose(0, 1, 3, 2).reshape(n_exp, 2 * d_ff)
        yb = _expert_call(block_e, n_used, xb, w1p, b1p, w2[l].astype(BF16), b2[l], n_blocks)

        if l + 1 < depth:
            raise NotImplementedError("only the single-layer configuration is implemented")
        out = _combine_call(dest_flat, h1, gate, norm_final_g, yb, bsz, s_len, tp)
    return out.reshape(bsz, s_len, d)
```

```python
import functools
import math

import jax
import jax.numpy as jnp
import numpy as np
from jax import lax
from jax.experimental import pallas as pl
from jax.experimental.pallas import tpu as pltpu

HEAD_DIM = 128
A_HEADS = 8
IDX_HEADS = 16
IDX_DIM = 64
TOPK_MAX = 256
B_HEADS = 4
TOP_K = 4
SWIGLU_LIMIT = 7.0
SWIGLU_ALPHA = 1.702
ROPE_THETA = 10000.0
RMS_EPS = 1e-5
ROW_BLOCK = 128
LANES = 128
NEG_BIG = -1e30
VMEM_LIMIT = 56 * 1024 * 1024

A_WIDTH = A_HEADS * HEAD_DIM
B_QK_WIDTH = B_HEADS * 2 * HEAD_DIM
B_V_WIDTH = B_HEADS * 2 * HEAD_DIM

F32 = jnp.float32
BF16 = jnp.bfloat16
I32 = jnp.int32

KEY_NEG_INF = int(np.array(0xFF800000, np.uint32).view(np.int32)) ^ 0x7FFFFFFF
INT_MIN = -(2 ** 31)


def _pick_tile(n, candidates):
    for c in candidates:
        if n % c == 0:
            return c
    raise ValueError(f"no tile for {n}")


def _params(sem, vmem=VMEM_LIMIT):
    return pltpu.CompilerParams(dimension_semantics=sem, vmem_limit_bytes=vmem)


def _rmsnorm_kernel(x_ref, g_ref, o_ref):
    x = x_ref[...]
    ms = jnp.mean(x * x, axis=-1, keepdims=True)
    o_ref[...] = (x * lax.rsqrt(ms + RMS_EPS) * g_ref[...]).astype(o_ref.dtype)


def _rmsnorm_call(x2d, g, tm):
    rows, d = x2d.shape
    return pl.pallas_call(
        _rmsnorm_kernel,
        grid=(rows // tm,),
        in_specs=[pl.BlockSpec((tm, d), lambda i: (i, 0)),
                  pl.BlockSpec((1, d), lambda i: (0, 0))],
        out_specs=pl.BlockSpec((tm, d), lambda i: (i, 0)),
        out_shape=jax.ShapeDtypeStruct((rows, d), BF16),
        compiler_params=_params(("parallel",)),
    )(x2d, g.reshape(1, d))


def _proj_kernel(*refs, mode, rot_half, scale):
    if mode == "rope":
        x_ref, w_ref, cos_ref, sin_ref, cs_ref, o_ref = refs
    else:
        x_ref, w_ref, o_ref = refs
    acc = jnp.dot(x_ref[...], w_ref[...], preferred_element_type=F32)
    if mode == "rope":
        acc = acc * cs_ref[...]
        cos = cos_ref[...]
        sin = sin_ref[...]
        tn = acc.shape[1]
        lane = lax.broadcasted_iota(I32, (acc.shape[0], LANES), 1)
        outs = []
        for c in range(tn // LANES):
            xh = acc[:, c * LANES:(c + 1) * LANES]
            if rot_half == LANES // 2:
                rot = pltpu.roll(xh, LANES // 2, 1)
            else:
                fwd = pltpu.roll(xh, LANES - rot_half, 1)
                bwd = pltpu.roll(xh, rot_half, 1)
                rot = jnp.where((lane % (2 * rot_half)) < rot_half, fwd, bwd)
            outs.append(xh * cos + rot * sin)
        acc = outs[0] if len(outs) == 1 else jnp.concatenate(outs, axis=1)
    elif mode == "sigmoid":
        acc = jax.nn.sigmoid(acc)
    elif mode == "scale":
        acc = acc * scale
    o_ref[...] = acc.astype(o_ref.dtype)


def _proj_call(xn, w, out_dtype, mode, tm, tn, cos=None, sin=None, rot_half=0, scale=1.0, tp=None,
               col_scale=None):
    rows, d = xn.shape
    n = w.shape[1]
    in_specs = [pl.BlockSpec((tm, d), lambda i, j: (i, 0)),
                pl.BlockSpec((d, tn), lambda i, j: (0, j))]
    args = [xn, w]
    if mode == "rope":
        nt = tp // tm
        if col_scale is None:
            col_scale = jnp.ones((1, n), F32)
        in_specs += [pl.BlockSpec((tm, LANES), lambda i, j: (i % nt, 0)),
                     pl.BlockSpec((tm, LANES), lambda i, j: (i % nt, 0)),
                     pl.BlockSpec((1, tn), lambda i, j: (0, j))]
        args += [cos, sin, col_scale]
    return pl.pallas_call(
        functools.partial(_proj_kernel, mode=mode, rot_half=rot_half, scale=scale),
        grid=(rows // tm, n // tn),
        in_specs=in_specs,
        out_specs=pl.BlockSpec((tm, tn), lambda i, j: (i, j)),
        out_shape=jax.ShapeDtypeStruct((rows, n), out_dtype),
        compiler_params=_params(("parallel", "arbitrary")),
    )(*args)


def _indexer_kernel(qi_ref, kc_ref, wi_ref, mask_ref, keys_ref, qs_ref, wb_ref, p_ref,
                    *, front, topk, n_chunks_total):
    i = pl.program_id(1)
    n_chunk = i + 1
    n_pair = (n_chunk + 1) // 2
    tq = qi_ref.shape[0]
    n_hp = IDX_HEADS // 2
    q_idx = i * tq + lax.broadcasted_iota(I32, (tq, LANES), 0)
    lane = lax.broadcasted_iota(I32, (tq, LANES), 1)

    w = wi_ref[...]
    for h in range(IDX_HEADS):
        wb_ref[h] = jnp.broadcast_to(w[:, h:h + 1], (tq, LANES))
    for p in range(n_hp):
        qs_ref[p * tq:(p + 1) * tq, :] = qi_ref[:, p * LANES:(p + 1) * LANES]

    nt_dims = (((1,), (1,)), ((), ()))

    def score_body(c2, carry):
        for u in range(2):
            c = 2 * c2 + u
            off = pl.multiple_of(c * LANES, LANES)
            kc = kc_ref[pl.ds(pl.multiple_of(c * 2 * LANES, 2 * LANES), 2 * LANES), :]
            s = lax.dot_general(qs_ref[...], kc, nt_dims, preferred_element_type=F32)
            acc = jnp.zeros((tq, LANES), F32)
            for p in range(n_hp):
                sp = s[p * tq:(p + 1) * tq]
                acc = acc + wb_ref[2 * p] * jnp.maximum(sp[:, :LANES], 0.0)
                acc = acc + wb_ref[2 * p + 1] * jnp.maximum(sp[:, LANES:], 0.0)
            k_idx = off + lane
            valid = (k_idx <= q_idx) & (k_idx >= front)
            acc = jnp.where(acc == 0.0, 0.0, acc)
            score = jnp.where(valid, acc, -jnp.inf)
            bits = pltpu.bitcast(score, I32)
            keys_ref[:, pl.ds(off, LANES)] = jnp.where(bits < 0, bits ^ 0x7FFFFFFF, bits)
        return carry

    lax.fori_loop(0, n_pair, score_body, 0)

    def count(pred_fn):
        def body(c2, cnt):
            off = pl.multiple_of(c2 * 2 * LANES, 2 * LANES)
            k0 = keys_ref[:, pl.ds(off, LANES)]
            k1 = keys_ref[:, pl.ds(off + LANES, LANES)]
            cnt = cnt + jnp.where(pred_fn(k0, off + lane), 1, 0)
            return cnt + jnp.where(pred_fn(k1, off + LANES + lane), 1, 0)
        cnt = lax.fori_loop(0, n_pair, body, jnp.zeros((tq, LANES), I32))
        return jnp.sum(cnt, axis=1, keepdims=True)

    def bit_cond(state):
        it, _, cnt_t = state
        return (it < 32) & (jnp.max(jnp.where(cnt_t != topk, 1, 0)) > 0)

    def bit_body(state):
        it, t, cnt_t = state
        trial = t + jnp.left_shift(jnp.int32(1), 31 - it)
        cnt = count(lambda k, _: k >= trial)
        take = cnt >= topk
        return it + 1, jnp.where(take, trial, t), jnp.where(take, cnt, cnt_t)

    _, thr, cnt_ge = lax.while_loop(
        bit_cond, bit_body,
        (jnp.int32(0), jnp.full((tq, 1), INT_MIN, I32), jnp.full((tq, 1), -1, I32)))

    finite = thr > KEY_NEG_INF
    tie_rows = finite & (cnt_ge > topk)
    p_ref[...] = jnp.full((tq, LANES), (n_chunks_total + 1) * LANES, I32)

    @pl.when(jnp.max(jnp.where(tie_rows, 1, 0)) > 0)
    def _():
        cnt_gt = count(lambda k, _: k > thr)
        need = topk - cnt_gt

        def idx_body(it, p):
            trial = p + jnp.left_shift(jnp.int32(1), 14 - it)
            cnt = count(lambda k, idx: (k == thr) & (idx < trial))
            return jnp.where(cnt < need, trial, p)

        p = lax.fori_loop(0, 15, idx_body, jnp.zeros((tq, 1), I32))
        p = jnp.where(tie_rows, p, (n_chunks_total + 1) * LANES)
        p_ref[...] = jnp.broadcast_to(p, (tq, LANES))

    p_lim = p_ref[...]

    def write_body(c, carry):
        off = pl.multiple_of(c * LANES, LANES)
        k = keys_ref[:, pl.ds(off, LANES)]
        k_idx = off + lane
        sel = (k > thr) | ((k == thr) & (k_idx <= p_lim))
        valid = (k_idx <= q_idx) & (k_idx >= front)
        mask_ref[:, pl.ds(off, LANES)] = jnp.where(sel & valid, 1, 0).astype(jnp.int8)
        return carry

    lax.fori_loop(0, n_chunk, write_body, 0)

    def zero_body(c, carry):
        off = pl.multiple_of(c * LANES, LANES)
        mask_ref[:, pl.ds(off, LANES)] = jnp.zeros((tq, LANES), jnp.int8)
        return carry

    lax.fori_loop(n_chunk, n_chunks_total, zero_body, 0)


def _indexer_call(idx_qk, k_cat, wi, bsz, tp, front, topk):
    tq = ROW_BLOCK
    nq = tp // tq
    n_qcols = IDX_HEADS * IDX_DIM
    return pl.pallas_call(
        functools.partial(_indexer_kernel, front=front, topk=topk, n_chunks_total=nq),
        grid=(bsz, nq),
        in_specs=[pl.BlockSpec((tq, n_qcols), lambda b, i: (b * nq + i, 0)),
                  pl.BlockSpec((2 * (tp + LANES), LANES), lambda b, i: (b, 0)),
                  pl.BlockSpec((tq, LANES), lambda b, i: (b * nq + i, 0))],
        out_specs=pl.BlockSpec((tq, tp), lambda b, i: (b * nq + i, 0)),
        out_shape=jax.ShapeDtypeStruct((bsz * tp, tp), jnp.int8),
        scratch_shapes=[pltpu.VMEM((tq, tp + LANES), I32),
                        pltpu.VMEM((n_qcols // LANES * tq, LANES), BF16),
                        pltpu.VMEM((IDX_HEADS, tq, LANES), F32),
                        pltpu.VMEM((tq, LANES), I32)],
        compiler_params=_params(("parallel", "arbitrary")),
    )(idx_qk, k_cat, wi)


def _attn_a_kernel(q_ref, k_ref, v_ref, mask_ref, o_ref, m_ref, l_ref, s_ref, p_ref, *acc_refs):
    i = pl.program_id(1)
    j = pl.program_id(2)
    nt_dims = (((1,), (1,)), ((), ()))

    @pl.when(j == 0)
    def _():
        m_ref[...] = jnp.full(m_ref.shape, NEG_BIG, F32)
        l_ref[...] = jnp.zeros(l_ref.shape, F32)
        for acc_ref in acc_refs:
            acc_ref[...] = jnp.zeros(acc_ref.shape, F32)

    @pl.when(j <= i)
    def _():
        def group(r, carry):
            rows = pl.ds(pl.multiple_of(r * ROW_BLOCK, ROW_BLOCK), ROW_BLOCK)
            lane = lax.broadcasted_iota(I32, (ROW_BLOCK, LANES), 1)
            bias = jnp.where(mask_ref[rows, :].astype(I32) != 0, 0.0, NEG_BIG)
            m_all = m_ref[rows, :]
            m_loc = m_all
            for h in range(A_HEADS):
                sl = slice(h * HEAD_DIM, (h + 1) * HEAD_DIM)
                s = lax.dot_general(q_ref[rows, sl], k_ref[:, sl], nt_dims, preferred_element_type=F32) + bias
                s_ref[h] = s
                m_loc = jnp.where(lane == h, jnp.max(s, axis=1, keepdims=True), m_loc)
            m_new = jnp.maximum(m_all, m_loc)
            alpha = jnp.exp2(m_all - m_new)
            l_out = alpha * l_ref[rows, :]
            for h in range(A_HEADS):
                p = jnp.exp2(s_ref[h] - m_new[:, h:h + 1])
                p_ref[h] = p.astype(BF16)
                l_out = l_out + jnp.where(lane == h, jnp.sum(p, axis=1, keepdims=True), 0.0)
            for h in range(A_HEADS):
                sl = slice(h * HEAD_DIM, (h + 1) * HEAD_DIM)
                acc_refs[h][rows, :] = alpha[:, h:h + 1] * acc_refs[h][rows, :] + jnp.dot(
                    p_ref[h], v_ref[:, sl], preferred_element_type=F32)
            m_ref[rows, :] = m_new
            l_ref[rows, :] = l_out
            return carry

        lax.fori_loop(0, q_ref.shape[0] // ROW_BLOCK, group, 0)

    @pl.when(j == i)
    def _():
        l_all = l_ref[...]
        for h in range(A_HEADS):
            sl = slice(h * HEAD_DIM, (h + 1) * HEAD_DIM)
            o_ref[:, sl] = (acc_refs[h][...] / l_all[:, h:h + 1]).astype(o_ref.dtype)


def _attn_a_call(qk, v, mask, bsz, tp, tt):
    nb = tp // tt
    w = A_WIDTH
    return pl.pallas_call(
        _attn_a_kernel,
        grid=(bsz, nb, nb),
        in_specs=[pl.BlockSpec((tt, w), lambda b, i, j: (b * nb + i, 0)),
                  pl.BlockSpec((tt, w), lambda b, i, j: (b * nb + jnp.minimum(i, j), 1)),
                  pl.BlockSpec((tt, w), lambda b, i, j: (b * nb + jnp.minimum(i, j), 0)),
                  pl.BlockSpec((tt, tt), lambda b, i, j: (b * nb + i, jnp.minimum(i, j)))],
        out_specs=pl.BlockSpec((tt, w), lambda b, i, j: (b * nb + i, 0)),
        out_shape=jax.ShapeDtypeStruct((bsz * tp, w), BF16),
        scratch_shapes=[pltpu.VMEM((tt, LANES), F32), pltpu.VMEM((tt, LANES), F32),
                        pltpu.VMEM((A_HEADS, ROW_BLOCK, tt), F32),
                        pltpu.VMEM((A_HEADS, ROW_BLOCK, tt), BF16)]
        + [pltpu.VMEM((tt, HEAD_DIM), F32) for _ in range(A_HEADS)],
        compiler_params=_params(("parallel", "parallel", "arbitrary")),
    )(qk, qk, v, mask)


def _attn_b_kernel(q_ref, k_ref, v_ref, lq1_ref, lk1_ref, lq2_ref, lk2_ref, g_ref, o_ref,
                   m_ref, l_ref, acc_ref, *, front, lambda_init):
    i = pl.program_id(1)
    j = pl.program_id(2)
    tt = q_ref.shape[0]
    nt_dims = (((1,), (1,)), ((), ()))
    dv = 2 * HEAD_DIM

    @pl.when(j == 0)
    def _():
        m_ref[...] = jnp.full(m_ref.shape, NEG_BIG, F32)
        l_ref[...] = jnp.zeros(l_ref.shape, F32)
        acc_ref[...] = jnp.zeros(acc_ref.shape, F32)

    def tile(masked):
        def group(r, carry):
            r0 = pl.multiple_of(r * ROW_BLOCK, ROW_BLOCK)
            rows = pl.ds(r0, ROW_BLOCK)
            if masked:
                q_idx = i * tt + r0 + lax.broadcasted_iota(I32, (ROW_BLOCK, tt), 0)
                k_idx = j * tt + lax.broadcasted_iota(I32, (ROW_BLOCK, tt), 1)
                bias = jnp.where((k_idx <= q_idx) & (k_idx >= front), 0.0, NEG_BIG)
            for h in range(B_HEADS):
                vh = v_ref[:, h * dv:(h + 1) * dv]
                for part in range(2):
                    c = 2 * h + part
                    sl = slice(c * HEAD_DIM, (c + 1) * HEAD_DIM)
                    s = lax.dot_general(q_ref[rows, sl], k_ref[:, sl], nt_dims, preferred_element_type=F32)
                    if masked:
                        s = s + bias
                    m_prev = m_ref[c, rows, :]
                    m_new = jnp.maximum(m_prev, jnp.max(s, axis=1, keepdims=True))
                    alpha = jnp.exp2(m_prev - m_new)
                    p = jnp.exp2(s - m_new)
                    l_ref[c, rows, :] = alpha * l_ref[c, rows, :] + jnp.sum(p, axis=1, keepdims=True)
                    acc_ref[c, rows, :] = alpha * acc_ref[c, rows, :] + jnp.dot(
                        p.astype(BF16), vh, preferred_element_type=F32)
                    m_ref[c, rows, :] = m_new
            return carry

        lax.fori_loop(0, tt // ROW_BLOCK, group, 0)

    pl.when((j == 0) | (j == i))(functools.partial(tile, True))
    pl.when((j > 0) & (j < i))(functools.partial(tile, False))

    @pl.when(j == i)
    def _():
        lam = (jnp.exp(jnp.sum(lq1_ref[...] * lk1_ref[...], axis=1, keepdims=True))
               - jnp.exp(jnp.sum(lq2_ref[...] * lk2_ref[...], axis=1, keepdims=True))
               + lambda_init)
        g = g_ref[...]
        for h in range(B_HEADS):
            o = acc_ref[2 * h] / l_ref[2 * h] - lam * (acc_ref[2 * h + 1] / l_ref[2 * h + 1])
            ms = jnp.mean(o * o, axis=-1, keepdims=True)
            y = o * lax.rsqrt(ms + RMS_EPS) * g
            o_ref[:, h * dv:(h + 1) * dv] = (y * (1.0 - lambda_init)).astype(o_ref.dtype)


def _attn_b_call(qk, v, lq1, lk1, lq2, lk2, subln_g, bsz, tp, tt, front, lambda_init):
    nb = tp // tt
    w = B_QK_WIDTH
    dv = 2 * HEAD_DIM
    vec = lambda a: a.reshape(1, -1).astype(F32)
    small = lambda n: pl.BlockSpec((1, n), lambda b, i, j: (0, 0))
    return pl.pallas_call(
        functools.partial(_attn_b_kernel, front=front, lambda_init=lambda_init),
        grid=(bsz, nb, nb),
        in_specs=[pl.BlockSpec((tt, w), lambda b, i, j: (b * nb + i, 2)),
                  pl.BlockSpec((tt, w), lambda b, i, j: (b * nb + jnp.minimum(i, j), 3)),
                  pl.BlockSpec((tt, B_V_WIDTH), lambda b, i, j: (b * nb + jnp.minimum(i, j), 1)),
                  small(HEAD_DIM), small(HEAD_DIM), small(HEAD_DIM), small(HEAD_DIM), small(dv)],
        out_specs=pl.BlockSpec((tt, B_V_WIDTH), lambda b, i, j: (b * nb + i, 0)),
        out_shape=jax.ShapeDtypeStruct((bsz * tp, B_V_WIDTH), BF16),
        scratch_shapes=[pltpu.VMEM((2 * B_HEADS, tt, 1), F32),
                        pltpu.VMEM((2 * B_HEADS, tt, 1), F32),
                        pltpu.VMEM((2 * B_HEADS, tt, dv), F32)],
        compiler_params=_params(("parallel", "parallel", "arbitrary")),
    )(qk, qk, v, vec(lq1), vec(lk1), vec(lq2), vec(lk2), vec(subln_g))


def _merge_router_kernel(oa_ref, ob_ref, ga_ref, gb_ref, h_ref, wua_ref, wub_ref, wo_ref, g_ref,
                         wr_ref, br_ref, h1_ref, hfp_ref, route_ref, gate_ref, cnt_ref, carry_ref,
                         *, tp, front):
    i = pl.program_id(0)
    tm = oa_ref.shape[0]
    d = h_ref.shape[1]

    @pl.when(i == 0)
    def _():
        carry_ref[...] = jnp.zeros(carry_ref.shape, F32)

    ua = jnp.dot(oa_ref[...], wua_ref[...], preferred_element_type=F32)
    ub = jnp.dot(ob_ref[...], wub_ref[...], preferred_element_type=F32)
    merged = ga_ref[...].astype(F32) * ua + gb_ref[...].astype(F32) * ub
    mix = jnp.dot(merged.astype(BF16), wo_ref[...], preferred_element_type=F32)
    h1 = h_ref[...] + mix
    h1_ref[...] = h1

    ms = jnp.mean(h1 * h1, axis=-1, keepdims=True)
    hf = (h1 * lax.rsqrt(ms + RMS_EPS) * g_ref[...]).astype(BF16)

    hf_bits = pltpu.bitcast(hf.astype(F32), jnp.uint32)
    hfp_ref[...] = (hf_bits[:, :d // 2] & jnp.uint32(0xFFFF0000)) | (hf_bits[:, d // 2:] >> 16)

    logits = jnp.dot(hf, wr_ref[...], preferred_element_type=F32) + br_ref[...]
    lane = lax.broadcasted_iota(I32, (tm, LANES), 1)
    row = (i * tm + lax.broadcasted_iota(I32, (tm, 1), 0)) % tp
    routed = row >= front

    work = logits
    vals, idxs, onehots = [], [], []
    for _ in range(TOP_K):
        v = jnp.max(work, axis=1, keepdims=True)
        e = jnp.min(jnp.where(work == v, lane, LANES), axis=1, keepdims=True)
        hit = lane == e
        vals.append(v)
        idxs.append(e)
        onehots.append(hit)
        work = jnp.where(hit, -jnp.inf, work)

    exps = [jnp.exp(v - vals[0]) for v in vals]
    denom = exps[0] + exps[1] + exps[2] + exps[3]

    member = jnp.zeros((tm, LANES), F32)
    for hit in onehots:
        member = member + jnp.where(hit & routed, 1.0, 0.0)
    r_i = lax.broadcasted_iota(I32, (tm, tm), 0)
    c_i = lax.broadcasted_iota(I32, (tm, tm), 1)
    lower = jnp.where(c_i < r_i, 1.0, 0.0).astype(BF16)
    before = jnp.dot(lower, member.astype(BF16), preferred_element_type=F32) + carry_ref[...]
    carry_ref[...] = carry_ref[...] + jnp.sum(member, axis=0, keepdims=True)

    route = jnp.zeros((tm, LANES), I32)
    gates = jnp.zeros((tm, LANES), F32)
    for k in range(TOP_K):
        rank = jnp.sum(jnp.where(onehots[k], before, 0.0), axis=1, keepdims=True)
        route = jnp.where(lane == k, idxs[k], route)
        route = jnp.where(lane == TOP_K + k, rank.astype(I32), route)
        gates = jnp.where(lane == k, exps[k] / denom, gates)
    route_ref[...] = route
    gate_ref[...] = gates
    cnt_ref[...] = jnp.broadcast_to(carry_ref[...], cnt_ref.shape)


def _merge_router_call(o_a, o_b, gates, h0, wua, wub, wo, g_ffn, wr, br, tp, front, tm):
    rows, d = h0.shape
    nt = rows // tm
    once = pl.Buffered(1)
    const = lambda shape: pl.BlockSpec(shape, lambda i: (0, 0), pipeline_mode=once)
    return pl.pallas_call(
        functools.partial(_merge_router_kernel, tp=tp, front=front),
        grid=(nt,),
        in_specs=[pl.BlockSpec((tm, A_WIDTH), lambda i: (i, 0)),
                  pl.BlockSpec((tm, B_V_WIDTH), lambda i: (i, 0)),
                  pl.BlockSpec((tm, d), lambda i: (i, 0)),
                  pl.BlockSpec((tm, d), lambda i: (i, 1)),
                  pl.BlockSpec((tm, d), lambda i: (i, 0)),
                  const((A_WIDTH, d)), const((B_V_WIDTH, d)), const((d, d)),
                  const((1, d)), const((d, LANES)), const((1, LANES))],
        out_specs=[pl.BlockSpec((tm, d), lambda i: (i, 0)),
                   pl.BlockSpec((tm, d // 2), lambda i: (i, 0)),
                   pl.BlockSpec((tm, LANES), lambda i: (i, 0)),
                   pl.BlockSpec((tm, LANES), lambda i: (i, 0)),
                   pl.BlockSpec((8, LANES), lambda i: (0, 0))],
        out_shape=[jax.ShapeDtypeStruct((rows, d), F32),
                   jax.ShapeDtypeStruct((rows, d // 2), jnp.uint32),
                   jax.ShapeDtypeStruct((rows, LANES), I32),
                   jax.ShapeDtypeStruct((rows, LANES), F32),
                   jax.ShapeDtypeStruct((8, LANES), F32)],
        scratch_shapes=[pltpu.VMEM((1, LANES), F32)],
        compiler_params=_params(("arbitrary",)),
    )(o_a, o_b, gates, gates, h0, wua, wub, wo, g_ffn, wr, br)


def _dispatch_kernel(dest_ref, hfp_ref, xb_in_ref, xb_ref, sem):
    del xb_in_ref
    tm = hfp_ref.shape[0]

    def row_copy(r, k):
        d = dest_ref[r * TOP_K + k]
        return pltpu.make_async_copy(hfp_ref.at[pl.ds(r, 1), :], xb_ref.at[pl.ds(d, 1), :], sem)

    def issue(r, carry):
        for k in range(TOP_K):
            row_copy(r, k).start()
        return carry

    lax.fori_loop(0, tm, issue, 0)

    def drain(r, carry):
        for k in range(TOP_K):
            row_copy(r, k).wait()
        return carry

    lax.fori_loop(0, tm, drain, 0)


def _dispatch_call(dest_flat, hfp, n_rows_total, tm):
    rows, half = hfp.shape
    xb0 = jnp.zeros((n_rows_total, half), jnp.uint32)
    return pl.pallas_call(
        _dispatch_kernel,
        grid=(rows // tm,),
        in_specs=[pl.BlockSpec((tm * TOP_K,), lambda i: (i,), memory_space=pltpu.SMEM),
                  pl.BlockSpec((tm, half), lambda i: (i, 0)),
                  pl.BlockSpec(memory_space=pl.ANY)],
        out_specs=pl.BlockSpec(memory_space=pl.ANY),
        out_shape=jax.ShapeDtypeStruct((n_rows_total, half), jnp.uint32),
        scratch_shapes=[pltpu.SemaphoreType.DMA(())],
        input_output_aliases={2: 0},
        compiler_params=_params(("arbitrary",)),
    )(dest_flat, hfp, xb0)


def _w1_prep_kernel(w_ref, o_ref):
    n = 2 * LANES
    s_i = lax.broadcasted_iota(I32, (n, n), 0)
    j_i = lax.broadcasted_iota(I32, (n, n), 1)
    src = jnp.where(j_i < LANES, 2 * j_i, 2 * (j_i - LANES) + 1)
    perm = jnp.where(s_i == src, 1.0, 0.0).astype(BF16)
    for c in range(w_ref.shape[2] // n):
        blk = w_ref[0, :, c * n:(c + 1) * n].astype(BF16)
        o_ref[0, :, c * n:(c + 1) * n] = jnp.dot(blk, perm, preferred_element_type=F32).astype(BF16)


def _w1_prep_call(w1l, tk):
    n_exp, d, two_ff = w1l.shape
    return pl.pallas_call(
        _w1_prep_kernel,
        grid=(n_exp, d // tk),
        in_specs=[pl.BlockSpec((1, tk, two_ff), lambda e, k: (e, k, 0))],
        out_specs=pl.BlockSpec((1, tk, two_ff), lambda e, k: (e, k, 0)),
        out_shape=jax.ShapeDtypeStruct((n_exp, d, two_ff), BF16),
        compiler_params=_params(("parallel", "parallel")),
    )(w1l)


def _expert_kernel(be_ref, nu_ref, xb_ref, w1_ref, b1_ref, w2_ref, b2_ref, y_ref):
    i = pl.program_id(0)
    d_ff = w2_ref.shape[1]

    @pl.when(i < nu_ref[0])
    def _():
        packed = xb_ref[...]
        hi = pltpu.bitcast(packed & jnp.uint32(0xFFFF0000), F32)
        lo = pltpu.bitcast(packed << 16, F32)
        x = jnp.concatenate([hi, lo], axis=1).astype(BF16)
        u = jnp.dot(x, w1_ref[0], preferred_element_type=F32) + b1_ref[0]
        n_grp = d_ff // LANES
        glu = jnp.concatenate([u[:, 2 * c * LANES:(2 * c + 1) * LANES] for c in range(n_grp)], axis=1)
        lin = jnp.concatenate([u[:, (2 * c + 1) * LANES:(2 * c + 2) * LANES] for c in range(n_grp)], axis=1)
        x_glu = jnp.minimum(glu, SWIGLU_LIMIT)
        x_lin = jnp.clip(lin, -SWIGLU_LIMIT, SWIGLU_LIMIT)
        act = x_glu * jax.nn.sigmoid(SWIGLU_ALPHA * x_glu) * (x_lin + 1.0)
        y_ref[...] = jnp.dot(act.astype(BF16), w2_ref[0], preferred_element_type=F32) + b2_ref[0]

    @pl.when(i >= nu_ref[0])
    def _():
        y_ref[...] = jnp.zeros(y_ref.shape, F32)


def _expert_call(block_e, n_used, xb, w1p, b1p, w2b, b2, n_blocks):
    n_exp, d, two_ff = w1p.shape
    d_ff = two_ff // 2
    half = xb.shape[1]
    once = pl.Buffered(1)

    def blk(i, be, nu):
        return jnp.minimum(i, nu[0] - 1)

    grid_spec = pltpu.PrefetchScalarGridSpec(
        num_scalar_prefetch=2,
        grid=(n_blocks,),
        in_specs=[pl.BlockSpec((ROW_BLOCK, half), lambda i, be, nu: (blk(i, be, nu), 0)),
                  pl.BlockSpec((1, d, two_ff), lambda i, be, nu: (be[blk(i, be, nu)], 0, 0), pipeline_mode=once),
                  pl.BlockSpec((1, 1, two_ff), lambda i, be, nu: (be[blk(i, be, nu)], 0, 0)),
                  pl.BlockSpec((1, d_ff, d), lambda i, be, nu: (be[blk(i, be, nu)], 0, 0)),
                  pl.BlockSpec((1, 1, d), lambda i, be, nu: (be[blk(i, be, nu)], 0, 0))],
        out_specs=pl.BlockSpec((ROW_BLOCK, d), lambda i, be, nu: (i, 0)),
    )
    return pl.pallas_call(
        _expert_kernel,
        grid_spec=grid_spec,
        out_shape=jax.ShapeDtypeStruct((n_blocks * ROW_BLOCK, d), F32),
        compiler_params=_params(("arbitrary",)),
    )(block_e, n_used, xb, w1p, b1p.reshape(n_exp, 1, two_ff), w2b, b2.reshape(n_exp, 1, d))


def _combine_kernel(dest_ref, h1_ref, gate_ref, g_ref, yb_ref, o_ref, ybuf_ref, sem):
    tm = h1_ref.shape[0]

    def row_copy(r, k):
        d = dest_ref[r * TOP_K + k]
        return pltpu.make_async_copy(yb_ref.at[pl.ds(d, 1), :], ybuf_ref.at[k, pl.ds(r, 1), :], sem)

    def issue(r, carry):
        for k in range(TOP_K):
            row_copy(r, k).start()
        return carry

    lax.fori_loop(0, tm, issue, 0)

    def drain(r, carry):
        for k in range(TOP_K):
            row_copy(r, k).wait()
        return carry

    lax.fori_loop(0, tm, drain, 0)

    gate = gate_ref[...]
    h2 = h1_ref[...]
    for k in range(TOP_K):
        h2 = h2 + gate[:, k:k + 1] * ybuf_ref[k]
    ms = jnp.mean(h2 * h2, axis=-1, keepdims=True)
    o_ref[...] = h2 * lax.rsqrt(ms + RMS_EPS) * g_ref[...]


def _combine_call(dest_flat, h1, gate, g_final, yb, bsz, s_len, tp):
    d = h1.shape[1]
    tm = ROW_BLOCK
    ns = s_len // tm
    nb = tp // tm
    skip = (tp - s_len) // tm

    def src(i):
        return (i // ns) * nb + skip + i % ns

    return pl.pallas_call(
        _combine_kernel,
        grid=(bsz * ns,),
        in_specs=[pl.BlockSpec((tm * TOP_K,), lambda i: (src(i),), memory_space=pltpu.SMEM),
                  pl.BlockSpec((tm, d), lambda i: (src(i), 0)),
                  pl.BlockSpec((tm, LANES), lambda i: (src(i), 0)),
                  pl.BlockSpec((1, d), lambda i: (0, 0)),
                  pl.BlockSpec(memory_space=pl.ANY)],
        out_specs=pl.BlockSpec((tm, d), lambda i: (i, 0)),
        out_shape=jax.ShapeDtypeStruct((bsz * s_len, d), F32),
        scratch_shapes=[pltpu.VMEM((TOP_K, tm, d), F32), pltpu.SemaphoreType.DMA(())],
        compiler_params=_params(("arbitrary",)),
    )(dest_flat, h1, gate, g_final.reshape(1, d), yb)


def _rope_tables(pos, dim, reps):
    inv = 1.0 / (ROPE_THETA ** (jnp.arange(0, dim, 2, dtype=F32) / dim))
    ang = pos[:, None] * inv[None, :]
    cos = jnp.concatenate([jnp.cos(ang), jnp.cos(ang)], axis=-1)
    sin = jnp.concatenate([-jnp.sin(ang), jnp.sin(ang)], axis=-1)
    return jnp.tile(cos, (1, reps)), jnp.tile(sin, (1, reps))


def kernel(x, meta_tokens, norm_mix_g, w_in, lambda_q1, lambda_k1, lambda_q2, lambda_k2, subln_g, w_up_a, w_up_b, w_out, norm_ffn_g, w_router, b_router, w1, b1, w2, b2, norm_final_g):
    bsz, s_len, d = x.shape
    n_meta = meta_tokens.shape[0]
    depth = w_in.shape[0]
    n_exp = w_router.shape[-1]
    d_ff = w2.shape[-2]
    t_len = s_len + n_meta
    tp = -(-t_len // ROW_BLOCK) * ROW_BLOCK
    front = tp - t_len
    topk = min(TOPK_MAX, t_len // 4)
    rows = bsz * tp
    assert s_len % ROW_BLOCK == 0 and n_exp <= LANES and d % (2 * LANES) == 0 and d_ff % LANES == 0

    tm_proj = _pick_tile(tp, (640, 512, 256, 128))
    tt = _pick_tile(tp, (640, 512, 256, 128))
    tm_merge = _pick_tile(rows, (256, 128))

    pos = jnp.arange(tp, dtype=F32) - float(front)
    cos_h, sin_h = _rope_tables(pos, HEAD_DIM, 1)
    cos_i, sin_i = _rope_tables(pos, IDX_DIM, LANES // IDX_DIM)

    meta = jnp.broadcast_to(meta_tokens[None].astype(x.dtype), (bsz, n_meta, d))
    h = jnp.concatenate([jnp.zeros((bsz, front, d), x.dtype), meta, x], axis=1).reshape(rows, d)

    o_qa, o_ka, o_va = 0, A_WIDTH, 2 * A_WIDTH
    o_qi = 3 * A_WIDTH
    o_ki = o_qi + IDX_HEADS * IDX_DIM
    o_wi = o_ki + IDX_DIM
    o_qb = o_wi + IDX_HEADS
    o_kb = o_qb + B_QK_WIDTH
    o_vb = o_kb + B_QK_WIDTH
    o_ga = o_vb + B_V_WIDTH
    o_gb = o_ga + d

    for l in range(depth):
        lambda_init = 0.8 - 0.6 * math.exp(-0.3 * l)
        wl = w_in[l]
        col = lambda o, n: wl[:, o:o + n]
        zeros_half = jnp.zeros((d, LANES - IDX_DIM), wl.dtype)
        w_rope = jnp.concatenate([col(o_qa, A_WIDTH), col(o_ka, A_WIDTH),
                                  col(o_qb, B_QK_WIDTH), col(o_kb, B_QK_WIDTH)], axis=1).astype(BF16)
        w_val = jnp.concatenate([col(o_va, A_WIDTH), col(o_vb, B_V_WIDTH)], axis=1).astype(BF16)
        w_gate = jnp.concatenate([col(o_ga, d), col(o_gb, d)], axis=1).astype(BF16)
        w_idx = jnp.concatenate([col(o_qi, IDX_HEADS * IDX_DIM), col(o_ki, IDX_DIM), zeros_half,
                                 zeros_half, col(o_ki, IDX_DIM)], axis=1).astype(BF16)
        w_wi = jnp.concatenate([col(o_wi, IDX_HEADS), jnp.zeros((d, LANES - IDX_HEADS), wl.dtype)],
                               axis=1).astype(BF16)

        hn = _rmsnorm_call(h, norm_mix_g[l], tm_proj)
        q_scale = HEAD_DIM ** -0.5 * math.log2(math.e)
        col_scale = jnp.concatenate([jnp.full((A_WIDTH,), q_scale, F32), jnp.ones((A_WIDTH,), F32),
                                     jnp.full((B_QK_WIDTH,), q_scale, F32), jnp.ones((B_QK_WIDTH,), F32)])
        qk = _proj_call(hn, w_rope, BF16, "rope", tm_proj, 1024, cos_h, sin_h, HEAD_DIM // 2, tp=tp,
                        col_scale=col_scale.reshape(1, -1))
        val = _proj_call(hn, w_val, BF16, "none", tm_proj, 1024)
        gates = _proj_call(hn, w_gate, BF16, "sigmoid", tm_proj, _pick_tile(2 * d, (1024, 512, 256)))
        idx_qk = _proj_call(hn, w_idx, BF16, "rope", tm_proj, 256, cos_i, sin_i, IDX_DIM // 2, tp=tp)
        wi = _proj_call(hn, w_wi, F32, "scale", tm_proj, LANES,
                        scale=IDX_HEADS ** -0.5 * IDX_DIM ** -0.5)

        n_qcols = IDX_HEADS * IDX_DIM
        nc = tp // LANES
        k_cat = jnp.stack([idx_qk[:, n_qcols:n_qcols + LANES].reshape(bsz, nc, LANES, LANES),
                           idx_qk[:, n_qcols + LANES:].reshape(bsz, nc, LANES, LANES)], axis=2)
        k_cat = jnp.pad(k_cat, ((0, 0), (0, 1), (0, 0), (0, 0), (0, 0))).reshape(
            bsz * 2 * (tp + LANES), LANES)
        mask = _indexer_call(idx_qk, k_cat, wi, bsz, tp, front, topk)
        o_a = _attn_a_call(qk, val, mask, bsz, tp, tt)
        o_b = _attn_b_call(qk, val, lambda_q1[l], lambda_k1[l], lambda_q2[l], lambda_k2[l], subln_g[l],
                           bsz, tp, tt, front, lambda_init)

        wr = jnp.concatenate([w_router[l], jnp.zeros((d, LANES - n_exp), F32)], axis=1).astype(BF16)
        br = jnp.concatenate([b_router[l], jnp.full((LANES - n_exp,), NEG_BIG, F32)]).reshape(1, LANES)
        h1, hfp, route, gate, cnt = _merge_router_call(
            o_a, o_b, gates, h, w_up_a[l].astype(BF16), w_up_b[l].astype(BF16), w_out[l].astype(BF16),
            norm_ffn_g[l].reshape(1, d), wr, br, tp, front, tm_merge)

        n_assign = bsz * t_len * TOP_K
        n_blocks = (n_assign + n_exp * (ROW_BLOCK - 1) + ROW_BLOCK - 1) // ROW_BLOCK
        n_rows = n_blocks * ROW_BLOCK
        counts = cnt[0, :n_exp].astype(I32)
        padded = (counts + ROW_BLOCK - 1) // ROW_BLOCK * ROW_BLOCK
        pad_end = jnp.cumsum(padded)
        pad_start = pad_end - padded
        n_used = (pad_end[-1:] // ROW_BLOCK).astype(I32)
        blk_start = jnp.arange(n_blocks, dtype=I32) * ROW_BLOCK
        block_e = jnp.minimum(jnp.sum((pad_end[None, :] <= blk_start[:, None]).astype(I32), axis=1),
                              n_exp - 1).astype(I32)
        top_e = route[:, :TOP_K]
        rank = route[:, TOP_K:2 * TOP_K]
        row_id = jnp.arange(rows, dtype=I32)
        is_pad = (row_id % tp) < front
        n_trash = bsz * front * TOP_K
        trash = (n_rows + ((row_id // tp) * front + row_id % tp)[:, None] * TOP_K
                 + jnp.arange(TOP_K, dtype=I32)[None, :])
        dest = jnp.where(is_pad[:, None], trash, pad_start[top_e] + rank)
        dest_flat = dest.reshape(-1).astype(I32)
        n_rows_total = n_rows + -(-max(n_trash, 1) // 8) * 8

        xb = _dispatch_call(dest_flat, hfp, n_rows_total, tm_merge)

        w1p = _w1_prep_call(w1[l], _pick_tile(d, (256, 128)))
        b1p = b1[l].reshape(n_exp, d_ff // LANES, LANES, 2).transpose(0, 1, 3, 2).reshape(n_exp, 2 * d_ff)
        yb = _expert_call(block_e, n_used, xb, w1p, b1p, w2[l].astype(BF16), b2[l], n_blocks)

        if l + 1 < depth:
            raise NotImplementedError("only the single-layer configuration is implemented")
        out = _combine_call(dest_flat, h1, gate, norm_final_g, yb, bsz, s_len, tp)
    return out.reshape(bsz, s_len, d)
```

```python
import functools
import math

import jax
import jax.numpy as jnp
import numpy as np
from jax import lax
from jax.experimental import pallas as pl
from jax.experimental.pallas import tpu as pltpu

HEAD_DIM = 128
A_HEADS = 8
IDX_HEADS = 16
IDX_DIM = 64
TOPK_MAX = 256
B_HEADS = 4
TOP_K = 4
SWIGLU_LIMIT = 7.0
SWIGLU_ALPHA = 1.702
ROPE_THETA = 10000.0
RMS_EPS = 1e-5
ROW_BLOCK = 128
LANES = 128
NEG_BIG = -1e30
VMEM_LIMIT = 56 * 1024 * 1024

A_WIDTH = A_HEADS * HEAD_DIM
B_QK_WIDTH = B_HEADS * 2 * HEAD_DIM
B_V_WIDTH = B_HEADS * 2 * HEAD_DIM

F32 = jnp.float32
BF16 = jnp.bfloat16
I32 = jnp.int32

KEY_NEG_INF = int(np.array(0xFF800000, np.uint32).view(np.int32)) ^ 0x7FFFFFFF
INT_MIN = -(2 ** 31)


def _pick_tile(n, candidates):
    for c in candidates:
        if n % c == 0:
            return c
    raise ValueError(f"no tile for {n}")


def _params(sem, vmem=VMEM_LIMIT):
    return pltpu.CompilerParams(dimension_semantics=sem, vmem_limit_bytes=vmem)


def _rmsnorm_kernel(x_ref, g_ref, o_ref):
    x = x_ref[...]
    ms = jnp.mean(x * x, axis=-1, keepdims=True)
    o_ref[...] = (x * lax.rsqrt(ms + RMS_EPS) * g_ref[...]).astype(o_ref.dtype)


def _rmsnorm_call(x2d, g, tm):
    rows, d = x2d.shape
    return pl.pallas_call(
        _rmsnorm_kernel,
        grid=(rows // tm,),
        in_specs=[pl.BlockSpec((tm, d), lambda i: (i, 0)),
                  pl.BlockSpec((1, d), lambda i: (0, 0))],
        out_specs=pl.BlockSpec((tm, d), lambda i: (i, 0)),
        out_shape=jax.ShapeDtypeStruct((rows, d), BF16),
        compiler_params=_params(("parallel",)),
    )(x2d, g.reshape(1, d))


def _proj_kernel(*refs, mode, rot_half, scale):
    if mode == "rope":
        x_ref, w_ref, cos_ref, sin_ref, cs_ref, o_ref = refs
    else:
        x_ref, w_ref, o_ref = refs
    acc = jnp.dot(x_ref[...], w_ref[...], preferred_element_type=F32)
    if mode == "rope":
        acc = acc * cs_ref[...]
        cos = cos_ref[...]
        sin = sin_ref[...]
        tn = acc.shape[1]
        lane = lax.broadcasted_iota(I32, (acc.shape[0], LANES), 1)
        outs = []
        for c in range(tn // LANES):
            xh = acc[:, c * LANES:(c + 1) * LANES]
            if rot_half == LANES // 2:
                rot = pltpu.roll(xh, LANES // 2, 1)
            else:
                fwd = pltpu.roll(xh, LANES - rot_half, 1)
                bwd = pltpu.roll(xh, rot_half, 1)
                rot = jnp.where((lane % (2 * rot_half)) < rot_half, fwd, bwd)
            outs.append(xh * cos + rot * sin)
        acc = outs[0] if len(outs) == 1 else jnp.concatenate(outs, axis=1)
    elif mode == "sigmoid":
        acc = jax.nn.sigmoid(acc)
    elif mode == "scale":
        acc = acc * scale
    o_ref[...] = acc.astype(o_ref.dtype)


def _proj_call(xn, w, out_dtype, mode, tm, tn, cos=None, sin=None, rot_half=0, scale=1.0, tp=None,
               col_scale=None):
    rows, d = xn.shape
    n = w.shape[1]
    in_specs = [pl.BlockSpec((tm, d), lambda i, j: (i, 0)),
                pl.BlockSpec((d, tn), lambda i, j: (0, j))]
    args = [xn, w]
    if mode == "rope":
        nt = tp // tm
        if col_scale is None:
            col_scale = jnp.ones((1, n), F32)
        in_specs += [pl.BlockSpec((tm, LANES), lambda i, j: (i % nt, 0)),
                     pl.BlockSpec((tm, LANES), lambda i, j: (i % nt, 0)),
                     pl.BlockSpec((1, tn), lambda i, j: (0, j))]
        args += [cos, sin, col_scale]
    return pl.pallas_call(
        functools.partial(_proj_kernel, mode=mode, rot_half=rot_half, scale=scale),
        grid=(rows // tm, n // tn),
        in_specs=in_specs,
        out_specs=pl.BlockSpec((tm, tn), lambda i, j: (i, j)),
        out_shape=jax.ShapeDtypeStruct((rows, n), out_dtype),
        compiler_params=_params(("parallel", "arbitrary")),
    )(*args)


def _indexer_kernel(qi_ref, kc_ref, wi_ref, mask_ref, keys_ref, qs_ref, p_ref,
                    *, front, topk, n_chunks_total):
    i = pl.program_id(1)
    n_chunk = i + 1
    n_pair = (n_chunk + 1) // 2
    tq = qi_ref.shape[0]
    n_hp = IDX_HEADS // 2
    q_idx = i * tq + lax.broadcasted_iota(I32, (LANES, tq), 1)
    k_off = lax.broadcasted_iota(I32, (LANES, tq), 0)

    w_t = wi_ref[...].T
    for p in range(n_hp):
        qs_ref[p * tq:(p + 1) * tq, :] = qi_ref[:, p * LANES:(p + 1) * LANES]

    nt_dims = (((1,), (1,)), ((), ()))

    def score_body(c2, carry):
        for u in range(2):
            c = 2 * c2 + u
            off = pl.multiple_of(c * LANES, LANES)
            kc = kc_ref[pl.ds(pl.multiple_of(c * 2 * LANES, 2 * LANES), 2 * LANES), :]
            acc = jnp.zeros((LANES, tq), F32)
            for g in range(n_hp // 2):
                qg = qs_ref[2 * g * tq:(2 * g + 2) * tq, :]
                st = lax.dot_general(kc, qg, nt_dims, preferred_element_type=F32)
                for v in range(2):
                    h0 = 2 * (2 * g + v)
                    sp = st[:, v * tq:(v + 1) * tq]
                    acc = acc + w_t[h0:h0 + 1, :] * jnp.maximum(sp[:LANES], 0.0)
                    acc = acc + w_t[h0 + 1:h0 + 2, :] * jnp.maximum(sp[LANES:], 0.0)
            k_idx = off + k_off
            valid = (k_idx <= q_idx) & (k_idx >= front)
            acc = jnp.where(acc == 0.0, 0.0, acc)
            score = jnp.where(valid, acc, -jnp.inf)
            bits = pltpu.bitcast(score, I32)
            keys_ref[pl.ds(off, LANES), :] = jnp.where(bits < 0, bits ^ 0x7FFFFFFF, bits)
        return carry

    lax.fori_loop(0, n_pair, score_body, 0)

    def count(pred_fn):
        def body(c2, cnt):
            off = pl.multiple_of(c2 * 2 * LANES, 2 * LANES)
            k0 = keys_ref[pl.ds(off, LANES), :]
            k1 = keys_ref[pl.ds(off + LANES, LANES), :]
            cnt = cnt + jnp.where(pred_fn(k0, off + k_off), 1, 0)
            return cnt + jnp.where(pred_fn(k1, off + LANES + k_off), 1, 0)
        cnt = lax.fori_loop(0, n_pair, body, jnp.zeros((LANES, tq), I32))
        return jnp.sum(cnt, axis=0, keepdims=True)

    bits_per_check = 4

    def bit_cond(state):
        it, _, cnt_t = state
        return (it < 32) & (jnp.max(jnp.where(cnt_t != topk, 1, 0)) > 0)

    def bit_body(state):
        it, t, cnt_t = state
        for u in range(bits_per_check):
            trial = t + jnp.left_shift(jnp.int32(1), 31 - u - it)
            cnt = count(lambda k, _: k >= trial)
            take = cnt >= topk
            t = jnp.where(take, trial, t)
            cnt_t = jnp.where(take, cnt, cnt_t)
        return it + bits_per_check, t, cnt_t

    _, thr, cnt_ge = lax.while_loop(
        bit_cond, bit_body,
        (jnp.int32(0), jnp.full((1, tq), INT_MIN, I32), jnp.full((1, tq), -1, I32)))

    finite = thr > KEY_NEG_INF
    tie_rows = finite & (cnt_ge > topk)
    p_ref[...] = jnp.full(p_ref.shape, (n_chunks_total + 1) * LANES, I32)

    @pl.when(jnp.max(jnp.where(tie_rows, 1, 0)) > 0)
    def _():
        cnt_gt = count(lambda k, _: k > thr)
        need = topk - cnt_gt

        def idx_body(it, p):
            trial = p + jnp.left_shift(jnp.int32(1), 14 - it)
            cnt = count(lambda k, idx: (k == thr) & (idx < trial))
            return jnp.where(cnt < need, trial, p)

        p = lax.fori_loop(0, 15, idx_body, jnp.zeros((1, tq), I32))
        p = jnp.where(tie_rows, p, (n_chunks_total + 1) * LANES)
        p_ref[...] = jnp.broadcast_to(p, p_ref.shape)

    p_lim = p_ref[0:1, :]
    eye = jnp.where(lax.broadcasted_iota(I32, (tq, tq), 0) == lax.broadcasted_iota(I32, (tq, tq), 1),
                    1.0, 0.0).astype(BF16)

    per_trip = 4

    def write_body(c4, carry):
        for u in range(per_trip):
            c = jnp.minimum(per_trip * c4 + u, n_chunk - 1)
            off = pl.multiple_of(c * LANES, LANES)
            k = keys_ref[pl.ds(off, LANES), :]
            k_idx = off + k_off
            sel = (k > thr) | ((k == thr) & (k_idx <= p_lim))
            valid = (k_idx <= q_idx) & (k_idx >= front)
            sel_t = jnp.where(sel & valid, 1.0, 0.0).astype(BF16)
            sel_qk = lax.dot_general(eye, sel_t, nt_dims, preferred_element_type=F32)
            mask_ref[:, pl.ds(off, LANES)] = sel_qk.astype(jnp.int8)
        return carry

    lax.fori_loop(0, (n_chunk + per_trip - 1) // per_trip, write_body, 0)

    def zero_body(c, carry):
        off = pl.multiple_of(c * LANES, LANES)
        mask_ref[:, pl.ds(off, LANES)] = jnp.zeros((tq, LANES), jnp.int8)
        return carry

    lax.fori_loop(n_chunk, n_chunks_total, zero_body, 0)


def _indexer_call(idx_qk, k_cat, wi, bsz, tp, front, topk):
    tq = ROW_BLOCK
    nq = tp // tq
    n_qcols = IDX_HEADS * IDX_DIM
    return pl.pallas_call(
        functools.partial(_indexer_kernel, front=front, topk=topk, n_chunks_total=nq),
        grid=(bsz, nq),
        in_specs=[pl.BlockSpec((tq, n_qcols), lambda b, i: (b * nq + i, 0)),
                  pl.BlockSpec((2 * (tp + LANES), LANES), lambda b, i: (b, 0)),
                  pl.BlockSpec((tq, LANES), lambda b, i: (b * nq + i, 0))],
        out_specs=pl.BlockSpec((tq, tp), lambda b, i: (b * nq + i, 0)),
        out_shape=jax.ShapeDtypeStruct((bsz * tp, tp), jnp.int8),
        scratch_shapes=[pltpu.VMEM((tp + LANES, tq), I32),
                        pltpu.VMEM((n_qcols // LANES * tq, LANES), BF16),
                        pltpu.VMEM((8, tq), I32)],
        compiler_params=_params(("parallel", "arbitrary")),
    )(idx_qk, k_cat, wi)


def _attn_a_kernel(q_ref, k_ref, v_ref, mask_ref, o_ref, m_ref, l_ref, s_ref, p_ref, *acc_refs):
    i = pl.program_id(1)
    j = pl.program_id(2)
    nt_dims = (((1,), (1,)), ((), ()))

    @pl.when(j == 0)
    def _():
        m_ref[...] = jnp.full(m_ref.shape, NEG_BIG, F32)
        l_ref[...] = jnp.zeros(l_ref.shape, F32)
        for acc_ref in acc_refs:
            acc_ref[...] = jnp.zeros(acc_ref.shape, F32)

    @pl.when(j <= i)
    def _():
        def group(r, carry):
            rows = pl.ds(pl.multiple_of(r * ROW_BLOCK, ROW_BLOCK), ROW_BLOCK)
            lane = lax.broadcasted_iota(I32, (ROW_BLOCK, LANES), 1)
            bias = jnp.where(mask_ref[rows, :].astype(I32) != 0, 0.0, NEG_BIG)
            m_all = m_ref[rows, :]
            m_loc = m_all
            for h in range(A_HEADS):
                sl = slice(h * HEAD_DIM, (h + 1) * HEAD_DIM)
                s = lax.dot_general(q_ref[rows, sl], k_ref[:, sl], nt_dims, preferred_element_type=F32) + bias
                s_ref[h] = s
                m_loc = jnp.where(lane == h, jnp.max(s, axis=1, keepdims=True), m_loc)
            m_new = jnp.maximum(m_all, m_loc)
            alpha = jnp.exp2(m_all - m_new)
            l_out = alpha * l_ref[rows, :]
            for h in range(A_HEADS):
                p = jnp.exp2(s_ref[h] - m_new[:, h:h + 1])
                p_ref[h] = p.astype(BF16)
                l_out = l_out + jnp.where(lane == h, jnp.sum(p, axis=1, keepdims=True), 0.0)
            for h in range(A_HEADS):
                sl = slice(h * HEAD_DIM, (h + 1) * HEAD_DIM)
                acc_refs[h][rows, :] = alpha[:, h:h + 1] * acc_refs[h][rows, :] + jnp.dot(
                    p_ref[h], v_ref[:, sl], preferred_element_type=F32)
            m_ref[rows, :] = m_new
            l_ref[rows, :] = l_out
            return carry

        lax.fori_loop(0, q_ref.shape[0] // ROW_BLOCK, group, 0)

    @pl.when(j == i)
    def _():
        l_all = l_ref[...]
        for h in range(A_HEADS):
            sl = slice(h * HEAD_DIM, (h + 1) * HEAD_DIM)
            o_ref[:, sl] = (acc_refs[h][...] / l_all[:, h:h + 1]).astype(o_ref.dtype)


def _attn_a_call(qk, v, mask, bsz, tp, tt):
    nb = tp // tt
    w = A_WIDTH
    return pl.pallas_call(
        _attn_a_kernel,
        grid=(bsz, nb, nb),
        in_specs=[pl.BlockSpec((tt, w), lambda b, i, j: (b * nb + i, 0)),
                  pl.BlockSpec((tt, w), lambda b, i, j: (b * nb + jnp.minimum(i, j), 1)),
                  pl.BlockSpec((tt, w), lambda b, i, j: (b * nb + jnp.minimum(i, j), 0)),
                  pl.BlockSpec((tt, tt), lambda b, i, j: (b * nb + i, jnp.minimum(i, j)))],
        out_specs=pl.BlockSpec((tt, w), lambda b, i, j: (b * nb + i, 0)),
        out_shape=jax.ShapeDtypeStruct((bsz * tp, w), BF16),
        scratch_shapes=[pltpu.VMEM((tt, LANES), F32), pltpu.VMEM((tt, LANES), F32),
                        pltpu.VMEM((A_HEADS, ROW_BLOCK, tt), F32),
                        pltpu.VMEM((A_HEADS, ROW_BLOCK, tt), BF16)]
        + [pltpu.VMEM((tt, HEAD_DIM), F32) for _ in range(A_HEADS)],
        compiler_params=_params(("parallel", "parallel", "arbitrary")),
    )(qk, qk, v, mask)


def _attn_b_kernel(q_ref, k_ref, v_ref, lq1_ref, lk1_ref, lq2_ref, lk2_ref, g_ref, o_ref,
                   m_ref, l_ref, s_ref, p_ref, acc_ref, *, front, lambda_init):
    i = pl.program_id(1)
    j = pl.program_id(2)
    tt = q_ref.shape[0]
    nt_dims = (((1,), (1,)), ((), ()))
    dv = 2 * HEAD_DIM

    @pl.when(j == 0)
    def _():
        m_ref[...] = jnp.full(m_ref.shape, NEG_BIG, F32)
        l_ref[...] = jnp.zeros(l_ref.shape, F32)
        acc_ref[...] = jnp.zeros(acc_ref.shape, F32)

    n_set = 2 * B_HEADS

    def tile(masked):
        def group(r, carry):
            r0 = pl.multiple_of(r * ROW_BLOCK, ROW_BLOCK)
            rows = pl.ds(r0, ROW_BLOCK)
            lane = lax.broadcasted_iota(I32, (ROW_BLOCK, LANES), 1)
            if masked:
                q_idx = i * tt + r0 + lax.broadcasted_iota(I32, (ROW_BLOCK, tt), 0)
                k_idx = j * tt + lax.broadcasted_iota(I32, (ROW_BLOCK, tt), 1)
                bias = jnp.where((k_idx <= q_idx) & (k_idx >= front), 0.0, NEG_BIG)
            m_all = m_ref[rows, :]
            m_loc = m_all
            for c in range(n_set):
                sl = slice(c * HEAD_DIM, (c + 1) * HEAD_DIM)
                s = lax.dot_general(q_ref[rows, sl], k_ref[:, sl], nt_dims, preferred_element_type=F32)
                if masked:
                    s = s + bias
                s_ref[c] = s
                m_loc = jnp.where(lane == c, jnp.max(s, axis=1, keepdims=True), m_loc)
            m_new = jnp.maximum(m_all, m_loc)
            alpha = jnp.exp2(m_all - m_new)
            l_out = alpha * l_ref[rows, :]
            for c in range(n_set):
                p = jnp.exp2(s_ref[c] - m_new[:, c:c + 1])
                p_ref[c] = p.astype(BF16)
                l_out = l_out + jnp.where(lane == c, jnp.sum(p, axis=1, keepdims=True), 0.0)
            for c in range(n_set):
                vh = v_ref[:, (c // 2) * dv:(c // 2 + 1) * dv]
                acc_ref[c, rows, :] = alpha[:, c:c + 1] * acc_ref[c, rows, :] + jnp.dot(
                    p_ref[c], vh, preferred_element_type=F32)
            m_ref[rows, :] = m_new
            l_ref[rows, :] = l_out
            return carry

        lax.fori_loop(0, tt // ROW_BLOCK, group, 0)

    pl.when((j == 0) | (j == i))(functools.partial(tile, True))
    pl.when((j > 0) & (j < i))(functools.partial(tile, False))

    @pl.when(j == i)
    def _():
        lam = (jnp.exp(jnp.sum(lq1_ref[...] * lk1_ref[...], axis=1, keepdims=True))
               - jnp.exp(jnp.sum(lq2_ref[...] * lk2_ref[...], axis=1, keepdims=True))
               + lambda_init)
        g = g_ref[...]
        l_all = l_ref[...]
        for h in range(B_HEADS):
            o = (acc_ref[2 * h] / l_all[:, 2 * h:2 * h + 1]
                 - lam * (acc_ref[2 * h + 1] / l_all[:, 2 * h + 1:2 * h + 2]))
            ms = jnp.mean(o * o, axis=-1, keepdims=True)
            y = o * lax.rsqrt(ms + RMS_EPS) * g
            o_ref[:, h * dv:(h + 1) * dv] = (y * (1.0 - lambda_init)).astype(o_ref.dtype)


def _attn_b_call(qk, v, lq1, lk1, lq2, lk2, subln_g, bsz, tp, tt, front, lambda_init):
    nb = tp // tt
    w = B_QK_WIDTH
    dv = 2 * HEAD_DIM
    vec = lambda a: a.reshape(1, -1).astype(F32)
    small = lambda n: pl.BlockSpec((1, n), lambda b, i, j: (0, 0))
    return pl.pallas_call(
        functools.partial(_attn_b_kernel, front=front, lambda_init=lambda_init),
        grid=(bsz, nb, nb),
        in_specs=[pl.BlockSpec((tt, w), lambda b, i, j: (b * nb + i, 2)),
                  pl.BlockSpec((tt, w), lambda b, i, j: (b * nb + jnp.minimum(i, j), 3)),
                  pl.BlockSpec((tt, B_V_WIDTH), lambda b, i, j: (b * nb + jnp.minimum(i, j), 1)),
                  small(HEAD_DIM), small(HEAD_DIM), small(HEAD_DIM), small(HEAD_DIM), small(dv)],
        out_specs=pl.BlockSpec((tt, B_V_WIDTH), lambda b, i, j: (b * nb + i, 0)),
        out_shape=jax.ShapeDtypeStruct((bsz * tp, B_V_WIDTH), BF16),
        scratch_shapes=[pltpu.VMEM((tt, LANES), F32),
                        pltpu.VMEM((tt, LANES), F32),
                        pltpu.VMEM((2 * B_HEADS, ROW_BLOCK, tt), F32),
                        pltpu.VMEM((2 * B_HEADS, ROW_BLOCK, tt), BF16),
                        pltpu.VMEM((2 * B_HEADS, tt, dv), F32)],
        compiler_params=_params(("parallel", "parallel", "arbitrary")),
    )(qk, qk, v, vec(lq1), vec(lk1), vec(lq2), vec(lk2), vec(subln_g))


def _merge_router_kernel(oa_ref, ob_ref, ga_ref, gb_ref, h_ref, wua_ref, wub_ref, wo_ref, g_ref,
                         wr_ref, br_ref, h1_ref, hfp_ref, route_ref, gate_ref, cnt_ref, carry_ref,
                         *, tp, front):
    i = pl.program_id(0)
    tm = oa_ref.shape[0]
    d = h_ref.shape[1]

    @pl.when(i == 0)
    def _():
        carry_ref[...] = jnp.zeros(carry_ref.shape, F32)

    ua = jnp.dot(oa_ref[...], wua_ref[...], preferred_element_type=F32)
    ub = jnp.dot(ob_ref[...], wub_ref[...], preferred_element_type=F32)
    merged = ga_ref[...].astype(F32) * ua + gb_ref[...].astype(F32) * ub
    mix = jnp.dot(merged.astype(BF16), wo_ref[...], preferred_element_type=F32)
    h1 = h_ref[...] + mix
    h1_ref[...] = h1

    ms = jnp.mean(h1 * h1, axis=-1, keepdims=True)
    hf = (h1 * lax.rsqrt(ms + RMS_EPS) * g_ref[...]).astype(BF16)

    hf_bits = pltpu.bitcast(hf.astype(F32), jnp.uint32)
    hfp_ref[...] = (hf_bits[:, :d // 2] & jnp.uint32(0xFFFF0000)) | (hf_bits[:, d // 2:] >> 16)

    logits = jnp.dot(hf, wr_ref[...], preferred_element_type=F32) + br_ref[...]
    lane = lax.broadcasted_iota(I32, (tm, LANES), 1)
    row = (i * tm + lax.broadcasted_iota(I32, (tm, 1), 0)) % tp
    routed = row >= front

    work = logits
    vals, idxs, onehots = [], [], []
    for _ in range(TOP_K):
        v = jnp.max(work, axis=1, keepdims=True)
        e = jnp.min(jnp.where(work == v, lane, LANES), axis=1, keepdims=True)
        hit = lane == e
        vals.append(v)
        idxs.append(e)
        onehots.append(hit)
        work = jnp.where(hit, -jnp.inf, work)

    exps = [jnp.exp(v - vals[0]) for v in vals]
    denom = exps[0] + exps[1] + exps[2] + exps[3]

    member = jnp.zeros((tm, LANES), F32)
    for hit in onehots:
        member = member + jnp.where(hit & routed, 1.0, 0.0)
    r_i = lax.broadcasted_iota(I32, (tm, tm), 0)
    c_i = lax.broadcasted_iota(I32, (tm, tm), 1)
    lower = jnp.where(c_i < r_i, 1.0, 0.0).astype(BF16)
    before = jnp.dot(lower, member.astype(BF16), preferred_element_type=F32) + carry_ref[...]
    carry_ref[...] = carry_ref[...] + jnp.sum(member, axis=0, keepdims=True)

    route = jnp.zeros((tm, LANES), I32)
    gates = jnp.zeros((tm, LANES), F32)
    for k in range(TOP_K):
        rank = jnp.sum(jnp.where(onehots[k], before, 0.0), axis=1, keepdims=True)
        route = jnp.where(lane == k, idxs[k], route)
        route = jnp.where(lane == TOP_K + k, rank.astype(I32), route)
        gates = jnp.where(lane == k, exps[k] / denom, gates)
    route_ref[...] = route
    gate_ref[...] = gates
    cnt_ref[...] = jnp.broadcast_to(carry_ref[...], cnt_ref.shape)


def _merge_router_call(o_a, o_b, gates, h0, wua, wub, wo, g_ffn, wr, br, tp, front, tm):
    rows, d = h0.shape
    nt = rows // tm
    once = pl.Buffered(1)
    const = lambda shape: pl.BlockSpec(shape, lambda i: (0, 0), pipeline_mode=once)
    return pl.pallas_call(
        functools.partial(_merge_router_kernel, tp=tp, front=front),
        grid=(nt,),
        in_specs=[pl.BlockSpec((tm, A_WIDTH), lambda i: (i, 0)),
                  pl.BlockSpec((tm, B_V_WIDTH), lambda i: (i, 0)),
                  pl.BlockSpec((tm, d), lambda i: (i, 0)),
                  pl.BlockSpec((tm, d), lambda i: (i, 1)),
                  pl.BlockSpec((tm, d), lambda i: (i, 0)),
                  const((A_WIDTH, d)), const((B_V_WIDTH, d)), const((d, d)),
                  const((1, d)), const((d, LANES)), const((1, LANES))],
        out_specs=[pl.BlockSpec((tm, d), lambda i: (i, 0)),
                   pl.BlockSpec((tm, d // 2), lambda i: (i, 0)),
                   pl.BlockSpec((tm, LANES), lambda i: (i, 0)),
                   pl.BlockSpec((tm, LANES), lambda i: (i, 0)),
                   pl.BlockSpec((8, LANES), lambda i: (0, 0))],
        out_shape=[jax.ShapeDtypeStruct((rows, d), F32),
                   jax.ShapeDtypeStruct((rows, d // 2), jnp.uint32),
                   jax.ShapeDtypeStruct((rows, LANES), I32),
                   jax.ShapeDtypeStruct((rows, LANES), F32),
                   jax.ShapeDtypeStruct((8, LANES), F32)],
        scratch_shapes=[pltpu.VMEM((1, LANES), F32)],
        compiler_params=_params(("arbitrary",)),
    )(o_a, o_b, gates, gates, h0, wua, wub, wo, g_ffn, wr, br)


def _dispatch_kernel(dest_ref, hfp_ref, xb_in_ref, xb_ref, sem):
    del xb_in_ref
    tm = hfp_ref.shape[0]

    def row_copy(r, k):
        d = dest_ref[r * TOP_K + k]
        return pltpu.make_async_copy(hfp_ref.at[pl.ds(r, 1), :], xb_ref.at[pl.ds(d, 1), :], sem)

    def issue(r, carry):
        for k in range(TOP_K):
            row_copy(r, k).start()
        return carry

    lax.fori_loop(0, tm, issue, 0)

    def drain(r, carry):
        for k in range(TOP_K):
            row_copy(r, k).wait()
        return carry

    lax.fori_loop(0, tm, drain, 0)


def _dispatch_call(dest_flat, hfp, n_rows_total, tm):
    rows, half = hfp.shape
    xb0 = jnp.zeros((n_rows_total, half), jnp.uint32)
    return pl.pallas_call(
        _dispatch_kernel,
        grid=(rows // tm,),
        in_specs=[pl.BlockSpec((tm * TOP_K,), lambda i: (i,), memory_space=pltpu.SMEM),
                  pl.BlockSpec((tm, half), lambda i: (i, 0)),
                  pl.BlockSpec(memory_space=pl.ANY)],
        out_specs=pl.BlockSpec(memory_space=pl.ANY),
        out_shape=jax.ShapeDtypeStruct((n_rows_total, half), jnp.uint32),
        scratch_shapes=[pltpu.SemaphoreType.DMA(())],
        input_output_aliases={2: 0},
        compiler_params=_params(("arbitrary",)),
    )(dest_flat, hfp, xb0)


def _w1_prep_kernel(w_ref, o_ref):
    n = 2 * LANES
    s_i = lax.broadcasted_iota(I32, (n, n), 0)
    j_i = lax.broadcasted_iota(I32, (n, n), 1)
    src = jnp.where(j_i < LANES, 2 * j_i, 2 * (j_i - LANES) + 1)
    perm = jnp.where(s_i == src, 1.0, 0.0).astype(BF16)
    for c in range(w_ref.shape[2] // n):
        blk = w_ref[0, :, c * n:(c + 1) * n].astype(BF16)
        o_ref[0, :, c * n:(c + 1) * n] = jnp.dot(blk, perm, preferred_element_type=F32).astype(BF16)


def _w1_prep_call(w1l, tk):
    n_exp, d, two_ff = w1l.shape
    return pl.pallas_call(
        _w1_prep_kernel,
        grid=(n_exp, d // tk),
        in_specs=[pl.BlockSpec((1, tk, two_ff), lambda e, k: (e, k, 0))],
        out_specs=pl.BlockSpec((1, tk, two_ff), lambda e, k: (e, k, 0)),
        out_shape=jax.ShapeDtypeStruct((n_exp, d, two_ff), BF16),
        compiler_params=_params(("parallel", "parallel")),
    )(w1l)


def _expert_kernel(be_ref, nu_ref, xb_ref, w1_ref, b1_ref, w2_ref, b2_ref, y_ref):
    i = pl.program_id(0)
    d_ff = w2_ref.shape[1]

    @pl.when(i < nu_ref[0])
    def _():
        packed = xb_ref[...]
        hi = pltpu.bitcast(packed & jnp.uint32(0xFFFF0000), F32)
        lo = pltpu.bitcast(packed << 16, F32)
        x = jnp.concatenate([hi, lo], axis=1).astype(BF16)
        u = jnp.dot(x, w1_ref[0], preferred_element_type=F32) + b1_ref[0]
        n_grp = d_ff // LANES
        glu = jnp.concatenate([u[:, 2 * c * LANES:(2 * c + 1) * LANES] for c in range(n_grp)], axis=1)
        lin = jnp.concatenate([u[:, (2 * c + 1) * LANES:(2 * c + 2) * LANES] for c in range(n_grp)], axis=1)
        x_glu = jnp.minimum(glu, SWIGLU_LIMIT)
        x_lin = jnp.clip(lin, -SWIGLU_LIMIT, SWIGLU_LIMIT)
        act = x_glu * jax.nn.sigmoid(SWIGLU_ALPHA * x_glu) * (x_lin + 1.0)
        y_ref[...] = jnp.dot(act.astype(BF16), w2_ref[0], preferred_element_type=F32) + b2_ref[0]

    @pl.when(i >= nu_ref[0])
    def _():
        y_ref[...] = jnp.zeros(y_ref.shape, F32)


def _expert_call(block_e, n_used, xb, w1p, b1p, w2b, b2, n_blocks):
    n_exp, d, two_ff = w1p.shape
    d_ff = two_ff // 2
    half = xb.shape[1]

    def blk(i, be, nu):
        return jnp.minimum(i, nu[0] - 1)

    grid_spec = pltpu.PrefetchScalarGridSpec(
        num_scalar_prefetch=2,
        grid=(n_blocks,),
        in_specs=[pl.BlockSpec((ROW_BLOCK, half), lambda i, be, nu: (blk(i, be, nu), 0)),
                  pl.BlockSpec((1, d, two_ff), lambda i, be, nu: (be[blk(i, be, nu)], 0, 0)),
                  pl.BlockSpec((1, 1, two_ff), lambda i, be, nu: (be[blk(i, be, nu)], 0, 0)),
                  pl.BlockSpec((1, d_ff, d), lambda i, be, nu: (be[blk(i, be, nu)], 0, 0)),
                  pl.BlockSpec((1, 1, d), lambda i, be, nu: (be[blk(i, be, nu)], 0, 0))],
        out_specs=pl.BlockSpec((ROW_BLOCK, d), lambda i, be, nu: (i, 0)),
    )
    return pl.pallas_call(
        _expert_kernel,
        grid_spec=grid_spec,
        out_shape=jax.ShapeDtypeStruct((n_blocks * ROW_BLOCK, d), F32),
        compiler_params=_params(("arbitrary",), vmem=60 * 1024 * 1024),
    )(block_e, n_used, xb, w1p, b1p.reshape(n_exp, 1, two_ff), w2b, b2.reshape(n_exp, 1, d))


def _combine_kernel(dest_ref, h1_ref, gate_ref, g_ref, yb_ref, o_ref, ybuf_ref, sem):
    tm = h1_ref.shape[0]

    def row_copy(r, k):
        d = dest_ref[r * TOP_K + k]
        return pltpu.make_async_copy(yb_ref.at[pl.ds(d, 1), :], ybuf_ref.at[k, pl.ds(r, 1), :], sem)

    def issue(r, carry):
        for k in range(TOP_K):
            row_copy(r, k).start()
        return carry

    lax.fori_loop(0, tm, issue, 0)

    def drain(r, carry):
        for k in range(TOP_K):
            row_copy(r, k).wait()
        return carry

    lax.fori_loop(0, tm, drain, 0)

    gate = gate_ref[...]
    h2 = h1_ref[...]
    for k in range(TOP_K):
        h2 = h2 + gate[:, k:k + 1] * ybuf_ref[k]
    ms = jnp.mean(h2 * h2, axis=-1, keepdims=True)
    o_ref[...] = h2 * lax.rsqrt(ms + RMS_EPS) * g_ref[...]


def _combine_call(dest_flat, h1, gate, g_final, yb, bsz, s_len, tp):
    d = h1.shape[1]
    tm = ROW_BLOCK
    ns = s_len // tm
    nb = tp // tm
    skip = (tp - s_len) // tm

    def src(i):
        return (i // ns) * nb + skip + i % ns

    return pl.pallas_call(
        _combine_kernel,
        grid=(bsz * ns,),
        in_specs=[pl.BlockSpec((tm * TOP_K,), lambda i: (src(i),), memory_space=pltpu.SMEM),
                  pl.BlockSpec((tm, d), lambda i: (src(i), 0)),
                  pl.BlockSpec((tm, LANES), lambda i: (src(i), 0)),
                  pl.BlockSpec((1, d), lambda i: (0, 0)),
                  pl.BlockSpec(memory_space=pl.ANY)],
        out_specs=pl.BlockSpec((tm, d), lambda i: (i, 0)),
        out_shape=jax.ShapeDtypeStruct((bsz * s_len, d), F32),
        scratch_shapes=[pltpu.VMEM((TOP_K, tm, d), F32), pltpu.SemaphoreType.DMA(())],
        compiler_params=_params(("arbitrary",)),
    )(dest_flat, h1, gate, g_final.reshape(1, d), yb)


def _rope_tables(pos, dim, reps):
    inv = 1.0 / (ROPE_THETA ** (jnp.arange(0, dim, 2, dtype=F32) / dim))
    ang = pos[:, None] * inv[None, :]
    cos = jnp.concatenate([jnp.cos(ang), jnp.cos(ang)], axis=-1)
    sin = jnp.concatenate([-jnp.sin(ang), jnp.sin(ang)], axis=-1)
    return jnp.tile(cos, (1, reps)), jnp.tile(sin, (1, reps))


def kernel(x, meta_tokens, norm_mix_g, w_in, lambda_q1, lambda_k1, lambda_q2, lambda_k2, subln_g, w_up_a, w_up_b, w_out, norm_ffn_g, w_router, b_router, w1, b1, w2, b2, norm_final_g):
    bsz, s_len, d = x.shape
    n_meta = meta_tokens.shape[0]
    depth = w_in.shape[0]
    n_exp = w_router.shape[-1]
    d_ff = w2.shape[-2]
    t_len = s_len + n_meta
    tp = -(-t_len // ROW_BLOCK) * ROW_BLOCK
    front = tp - t_len
    topk = min(TOPK_MAX, t_len // 4)
    rows = bsz * tp
    assert s_len % ROW_BLOCK == 0 and n_exp <= LANES and d % (2 * LANES) == 0 and d_ff % LANES == 0

    tm_proj = _pick_tile(tp, (640, 512, 256, 128))
    tt = _pick_tile(tp, (640, 512, 256, 128))
    tm_merge = _pick_tile(rows, (256, 128))

    pos = jnp.arange(tp, dtype=F32) - float(front)
    cos_h, sin_h = _rope_tables(pos, HEAD_DIM, 1)
    cos_i, sin_i = _rope_tables(pos, IDX_DIM, LANES // IDX_DIM)

    meta = jnp.broadcast_to(meta_tokens[None].astype(x.dtype), (bsz, n_meta, d))
    h = jnp.concatenate([jnp.zeros((bsz, front, d), x.dtype), meta, x], axis=1).reshape(rows, d)

    o_qa, o_ka, o_va = 0, A_WIDTH, 2 * A_WIDTH
    o_qi = 3 * A_WIDTH
    o_ki = o_qi + IDX_HEADS * IDX_DIM
    o_wi = o_ki + IDX_DIM
    o_qb = o_wi + IDX_HEADS
    o_kb = o_qb + B_QK_WIDTH
    o_vb = o_kb + B_QK_WIDTH
    o_ga = o_vb + B_V_WIDTH
    o_gb = o_ga + d

    for l in range(depth):
        lambda_init = 0.8 - 0.6 * math.exp(-0.3 * l)
        wl = w_in[l]
        col = lambda o, n: wl[:, o:o + n]
        zeros_half = jnp.zeros((d, LANES - IDX_DIM), wl.dtype)
        w_rope = jnp.concatenate([col(o_qa, A_WIDTH), col(o_ka, A_WIDTH),
                                  col(o_qb, B_QK_WIDTH), col(o_kb, B_QK_WIDTH)], axis=1).astype(BF16)
        w_val = jnp.concatenate([col(o_va, A_WIDTH), col(o_vb, B_V_WIDTH)], axis=1).astype(BF16)
        w_gate = jnp.concatenate([col(o_ga, d), col(o_gb, d)], axis=1).astype(BF16)
        w_idx = jnp.concatenate([col(o_qi, IDX_HEADS * IDX_DIM), col(o_ki, IDX_DIM), zeros_half,
                                 zeros_half, col(o_ki, IDX_DIM)], axis=1).astype(BF16)
        w_wi = jnp.concatenate([col(o_wi, IDX_HEADS), jnp.zeros((d, LANES - IDX_HEADS), wl.dtype)],
                               axis=1).astype(BF16)

        hn = _rmsnorm_call(h, norm_mix_g[l], tm_proj)
        q_scale = HEAD_DIM ** -0.5 * math.log2(math.e)
        col_scale = jnp.concatenate([jnp.full((A_WIDTH,), q_scale, F32), jnp.ones((A_WIDTH,), F32),
                                     jnp.full((B_QK_WIDTH,), q_scale, F32), jnp.ones((B_QK_WIDTH,), F32)])
        qk = _proj_call(hn, w_rope, BF16, "rope", tm_proj, 1024, cos_h, sin_h, HEAD_DIM // 2, tp=tp,
                        col_scale=col_scale.reshape(1, -1))
        val = _proj_call(hn, w_val, BF16, "none", tm_proj, 1024)
        gates = _proj_call(hn, w_gate, BF16, "sigmoid", tm_proj, _pick_tile(2 * d, (1024, 512, 256)))
        idx_qk = _proj_call(hn, w_idx, BF16, "rope", tm_proj, 256, cos_i, sin_i, IDX_DIM // 2, tp=tp)
        wi = _proj_call(hn, w_wi, F32, "scale", tm_proj, LANES,
                        scale=IDX_HEADS ** -0.5 * IDX_DIM ** -0.5)

        n_qcols = IDX_HEADS * IDX_DIM
        nc = tp // LANES
        k_cat = jnp.stack([idx_qk[:, n_qcols:n_qcols + LANES].reshape(bsz, nc, LANES, LANES),
                           idx_qk[:, n_qcols + LANES:].reshape(bsz, nc, LANES, LANES)], axis=2)
        k_cat = jnp.pad(k_cat, ((0, 0), (0, 1), (0, 0), (0, 0), (0, 0))).reshape(
            bsz * 2 * (tp + LANES), LANES)
        mask = _indexer_call(idx_qk, k_cat, wi, bsz, tp, front, topk)
        o_a = _attn_a_call(qk, val, mask, bsz, tp, tt)
        o_b = _attn_b_call(qk, val, lambda_q1[l], lambda_k1[l], lambda_q2[l], lambda_k2[l], subln_g[l],
                           bsz, tp, tt, front, lambda_init)

        wr = jnp.concatenate([w_router[l], jnp.zeros((d, LANES - n_exp), F32)], axis=1).astype(BF16)
        br = jnp.concatenate([b_router[l], jnp.full((LANES - n_exp,), NEG_BIG, F32)]).reshape(1, LANES)
        h1, hfp, route, gate, cnt = _merge_router_call(
            o_a, o_b, gates, h, w_up_a[l].astype(BF16), w_up_b[l].astype(BF16), w_out[l].astype(BF16),
            norm_ffn_g[l].reshape(1, d), wr, br, tp, front, tm_merge)

        n_assign = bsz * t_len * TOP_K
        n_blocks = (n_assign + n_exp * (ROW_BLOCK - 1) + ROW_BLOCK - 1) // ROW_BLOCK
        n_rows = n_blocks * ROW_BLOCK
        counts = cnt[0, :n_exp].astype(I32)
        padded = (counts + ROW_BLOCK - 1) // ROW_BLOCK * ROW_BLOCK
        pad_end = jnp.cumsum(padded)
        pad_start = pad_end - padded
        n_used = (pad_end[-1:] // ROW_BLOCK).astype(I32)
        blk_start = jnp.arange(n_blocks, dtype=I32) * ROW_BLOCK
        block_e = jnp.minimum(jnp.sum((pad_end[None, :] <= blk_start[:, None]).astype(I32), axis=1),
                              n_exp - 1).astype(I32)
        top_e = route[:, :TOP_K]
        rank = route[:, TOP_K:2 * TOP_K]
        row_id = jnp.arange(rows, dtype=I32)
        is_pad = (row_id % tp) < front
        n_trash = bsz * front * TOP_K
        trash = (n_rows + ((row_id // tp) * front + row_id % tp)[:, None] * TOP_K
                 + jnp.arange(TOP_K, dtype=I32)[None, :])
        dest = jnp.where(is_pad[:, None], trash, pad_start[top_e] + rank)
        dest_flat = dest.reshape(-1).astype(I32)
        n_rows_total = n_rows + -(-max(n_trash, 1) // 8) * 8

        xb = _dispatch_call(dest_flat, hfp, n_rows_total, tm_merge)

        w1p = _w1_prep_call(w1[l], _pick_tile(d, (256, 128)))
        b1p = b1[l].reshape(n_exp, d_ff // LANES, LANES, 2).transpose(0, 1, 3, 2).reshape(n_exp, 2 * d_ff)
        yb = _expert_call(block_e, n_used, xb, w1p, b1p, w2[l].astype(BF16), b2[l], n_blocks)

        if l + 1 < depth:
            raise NotImplementedError("only the single-layer configuration is implemented")
        out = _combine_call(dest_flat, h1, gate, norm_final_g, yb, bsz, s_len, tp)
    return out.reshape(bsz, s_len, d)
```

```python
import functools
import math

import jax
import jax.numpy as jnp
import numpy as np
from jax import lax
from jax.experimental import pallas as pl
from jax.experimental.pallas import tpu as pltpu

HEAD_DIM = 128
A_HEADS = 8
IDX_HEADS = 16
IDX_DIM = 64
TOPK_MAX = 256
B_HEADS = 4
TOP_K = 4
SWIGLU_LIMIT = 7.0
SWIGLU_ALPHA = 1.702
ROPE_THETA = 10000.0
RMS_EPS = 1e-5
ROW_BLOCK = 128
LANES = 128
NEG_BIG = -1e30
VMEM_LIMIT = 56 * 1024 * 1024

A_WIDTH = A_HEADS * HEAD_DIM
B_QK_WIDTH = B_HEADS * 2 * HEAD_DIM
B_V_WIDTH = B_HEADS * 2 * HEAD_DIM

F32 = jnp.float32
BF16 = jnp.bfloat16
I32 = jnp.int32

KEY_NEG_INF = int(np.array(0xFF800000, np.uint32).view(np.int32)) ^ 0x7FFFFFFF
INT_MIN = -(2 ** 31)


def _pick_tile(n, candidates):
    for c in candidates:
        if n % c == 0:
            return c
    raise ValueError(f"no tile for {n}")


def _params(sem, vmem=VMEM_LIMIT):
    return pltpu.CompilerParams(dimension_semantics=sem, vmem_limit_bytes=vmem)


def _rmsnorm_kernel(x_ref, g_ref, o_ref):
    x = x_ref[...]
    ms = jnp.mean(x * x, axis=-1, keepdims=True)
    o_ref[...] = (x * lax.rsqrt(ms + RMS_EPS) * g_ref[...]).astype(o_ref.dtype)


def _rmsnorm_call(x2d, g, tm):
    rows, d = x2d.shape
    return pl.pallas_call(
        _rmsnorm_kernel,
        grid=(rows // tm,),
        in_specs=[pl.BlockSpec((tm, d), lambda i: (i, 0)),
                  pl.BlockSpec((1, d), lambda i: (0, 0))],
        out_specs=pl.BlockSpec((tm, d), lambda i: (i, 0)),
        out_shape=jax.ShapeDtypeStruct((rows, d), BF16),
        compiler_params=_params(("parallel",)),
    )(x2d, g.reshape(1, d))


def _proj_kernel(*refs, mode, rot_half, scale):
    if mode == "rope":
        x_ref, w_ref, cos_ref, sin_ref, cs_ref, o_ref = refs
    else:
        x_ref, w_ref, o_ref = refs
    acc = jnp.dot(x_ref[...], w_ref[...], preferred_element_type=F32)
    if mode == "rope":
        acc = acc * cs_ref[...]
        cos = cos_ref[...]
        sin = sin_ref[...]
        tn = acc.shape[1]
        lane = lax.broadcasted_iota(I32, (acc.shape[0], LANES), 1)
        outs = []
        for c in range(tn // LANES):
            xh = acc[:, c * LANES:(c + 1) * LANES]
            if rot_half == LANES // 2:
                rot = pltpu.roll(xh, LANES // 2, 1)
            else:
                fwd = pltpu.roll(xh, LANES - rot_half, 1)
                bwd = pltpu.roll(xh, rot_half, 1)
                rot = jnp.where((lane % (2 * rot_half)) < rot_half, fwd, bwd)
            outs.append(xh * cos + rot * sin)
        acc = outs[0] if len(outs) == 1 else jnp.concatenate(outs, axis=1)
    elif mode == "sigmoid":
        acc = jax.nn.sigmoid(acc)
    elif mode == "scale":
        acc = acc * scale
    o_ref[...] = acc.astype(o_ref.dtype)


def _proj_call(xn, w, out_dtype, mode, tm, tn, cos=None, sin=None, rot_half=0, scale=1.0, tp=None,
               col_scale=None):
    rows, d = xn.shape
    n = w.shape[1]
    in_specs = [pl.BlockSpec((tm, d), lambda i, j: (i, 0)),
                pl.BlockSpec((d, tn), lambda i, j: (0, j))]
    args = [xn, w]
    if mode == "rope":
        nt = tp // tm
        if col_scale is None:
            col_scale = jnp.ones((1, n), F32)
        in_specs += [pl.BlockSpec((tm, LANES), lambda i, j: (i % nt, 0)),
                     pl.BlockSpec((tm, LANES), lambda i, j: (i % nt, 0)),
                     pl.BlockSpec((1, tn), lambda i, j: (0, j))]
        args += [cos, sin, col_scale]
    return pl.pallas_call(
        functools.partial(_proj_kernel, mode=mode, rot_half=rot_half, scale=scale),
        grid=(rows // tm, n // tn),
        in_specs=in_specs,
        out_specs=pl.BlockSpec((tm, tn), lambda i, j: (i, j)),
        out_shape=jax.ShapeDtypeStruct((rows, n), out_dtype),
        compiler_params=_params(("parallel", "arbitrary")),
    )(*args)


def _indexer_kernel(qi_ref, kc_ref, wi_ref, mask_ref, keys_ref, qs_ref, p_ref,
                    *, front, topk, n_chunks_total):
    i = pl.program_id(1)
    n_chunk = i + 1
    n_pair = (n_chunk + 1) // 2
    tq = qi_ref.shape[0]
    n_hp = IDX_HEADS // 2
    q_idx = i * tq + lax.broadcasted_iota(I32, (LANES, tq), 1)
    k_off = lax.broadcasted_iota(I32, (LANES, tq), 0)

    w_t = wi_ref[...].T
    for p in range(n_hp):
        qs_ref[p * tq:(p + 1) * tq, :] = qi_ref[:, p * LANES:(p + 1) * LANES]

    nt_dims = (((1,), (1,)), ((), ()))

    def score_body(c2, carry):
        for u in range(2):
            c = 2 * c2 + u
            off = pl.multiple_of(c * LANES, LANES)
            kc = kc_ref[pl.ds(pl.multiple_of(c * 2 * LANES, 2 * LANES), 2 * LANES), :]
            acc = jnp.zeros((LANES, tq), F32)
            for g in range(n_hp // 2):
                qg = qs_ref[2 * g * tq:(2 * g + 2) * tq, :]
                st = lax.dot_general(kc, qg, nt_dims, preferred_element_type=F32)
                for v in range(2):
                    h0 = 2 * (2 * g + v)
                    sp = st[:, v * tq:(v + 1) * tq]
                    acc = acc + w_t[h0:h0 + 1, :] * jnp.maximum(sp[:LANES], 0.0)
                    acc = acc + w_t[h0 + 1:h0 + 2, :] * jnp.maximum(sp[LANES:], 0.0)
            k_idx = off + k_off
            valid = (k_idx <= q_idx) & (k_idx >= front)
            acc = jnp.where(acc == 0.0, 0.0, acc)
            score = jnp.where(valid, acc, -jnp.inf)
            bits = pltpu.bitcast(score, I32)
            keys_ref[pl.ds(off, LANES), :] = jnp.where(bits < 0, bits ^ 0x7FFFFFFF, bits)
        return carry

    lax.fori_loop(0, n_pair, score_body, 0)

    def count(pred_fn):
        def body(c2, cnt):
            off = pl.multiple_of(c2 * 2 * LANES, 2 * LANES)
            k0 = keys_ref[pl.ds(off, LANES), :]
            k1 = keys_ref[pl.ds(off + LANES, LANES), :]
            cnt = cnt + jnp.where(pred_fn(k0, off + k_off), 1, 0)
            return cnt + jnp.where(pred_fn(k1, off + LANES + k_off), 1, 0)
        cnt = lax.fori_loop(0, n_pair, body, jnp.zeros((LANES, tq), I32))
        return jnp.sum(cnt, axis=0, keepdims=True)

    bits_per_check = 4

    def bit_cond(state):
        it, _, cnt_t = state
        return (it < 32) & (jnp.max(jnp.where(cnt_t != topk, 1, 0)) > 0)

    def bit_body(state):
        it, t, cnt_t = state
        for u in range(bits_per_check):
            trial = t + jnp.left_shift(jnp.int32(1), 31 - u - it)
            cnt = count(lambda k, _: k >= trial)
            take = cnt >= topk
            t = jnp.where(take, trial, t)
            cnt_t = jnp.where(take, cnt, cnt_t)
        return it + bits_per_check, t, cnt_t

    _, thr, cnt_ge = lax.while_loop(
        bit_cond, bit_body,
        (jnp.int32(0), jnp.full((1, tq), INT_MIN, I32), jnp.full((1, tq), -1, I32)))

    finite = thr > KEY_NEG_INF
    tie_rows = finite & (cnt_ge > topk)
    p_ref[...] = jnp.full(p_ref.shape, (n_chunks_total + 1) * LANES, I32)

    @pl.when(jnp.max(jnp.where(tie_rows, 1, 0)) > 0)
    def _():
        cnt_gt = count(lambda k, _: k > thr)
        need = topk - cnt_gt

        def idx_body(it, p):
            trial = p + jnp.left_shift(jnp.int32(1), 14 - it)
            cnt = count(lambda k, idx: (k == thr) & (idx < trial))
            return jnp.where(cnt < need, trial, p)

        p = lax.fori_loop(0, 15, idx_body, jnp.zeros((1, tq), I32))
        p = jnp.where(tie_rows, p, (n_chunks_total + 1) * LANES)
        p_ref[...] = jnp.broadcast_to(p, p_ref.shape)

    p_lim = p_ref[0:1, :]
    eye = jnp.where(lax.broadcasted_iota(I32, (tq, tq), 0) == lax.broadcasted_iota(I32, (tq, tq), 1),
                    1.0, 0.0).astype(BF16)

    per_trip = 4

    def write_body(c4, carry):
        for u in range(per_trip):
            c = jnp.minimum(per_trip * c4 + u, n_chunk - 1)
            off = pl.multiple_of(c * LANES, LANES)
            k = keys_ref[pl.ds(off, LANES), :]
            k_idx = off + k_off
            sel = (k > thr) | ((k == thr) & (k_idx <= p_lim))
            valid = (k_idx <= q_idx) & (k_idx >= front)
            sel_t = jnp.where(sel & valid, 1.0, 0.0).astype(BF16)
            sel_qk = lax.dot_general(eye, sel_t, nt_dims, preferred_element_type=F32)
            mask_ref[:, pl.ds(off, LANES)] = sel_qk.astype(jnp.int8)
        return carry

    lax.fori_loop(0, (n_chunk + per_trip - 1) // per_trip, write_body, 0)

    def zero_body(c, carry):
        off = pl.multiple_of(c * LANES, LANES)
        mask_ref[:, pl.ds(off, LANES)] = jnp.zeros((tq, LANES), jnp.int8)
        return carry

    lax.fori_loop(n_chunk, n_chunks_total, zero_body, 0)


def _indexer_call(idx_qk, k_cat, wi, bsz, tp, front, topk):
    tq = ROW_BLOCK
    nq = tp // tq
    n_qcols = IDX_HEADS * IDX_DIM
    return pl.pallas_call(
        functools.partial(_indexer_kernel, front=front, topk=topk, n_chunks_total=nq),
        grid=(bsz, nq),
        in_specs=[pl.BlockSpec((tq, n_qcols), lambda b, i: (b * nq + i, 0)),
                  pl.BlockSpec((2 * (tp + LANES), LANES), lambda b, i: (b, 0)),
                  pl.BlockSpec((tq, LANES), lambda b, i: (b * nq + i, 0))],
        out_specs=pl.BlockSpec((tq, tp), lambda b, i: (b * nq + i, 0)),
        out_shape=jax.ShapeDtypeStruct((bsz * tp, tp), jnp.int8),
        scratch_shapes=[pltpu.VMEM((tp + LANES, tq), I32),
                        pltpu.VMEM((n_qcols // LANES * tq, LANES), BF16),
                        pltpu.VMEM((8, tq), I32)],
        compiler_params=_params(("parallel", "arbitrary")),
    )(idx_qk, k_cat, wi)


def _causal_tile_pairs(nb):
    pairs = [(i, j) for i in range(nb) for j in range(i + 1)]
    return (jnp.array([p[0] for p in pairs], I32), jnp.array([p[1] for p in pairs], I32))


def _attn_a_kernel(qt_ref, kt_ref, q_ref, k_ref, v_ref, mask_ref, o_ref, m_ref, l_ref, s_ref, p_ref,
                   *acc_refs):
    t = pl.program_id(1)
    i = qt_ref[t]
    j = kt_ref[t]
    tt = q_ref.shape[0]
    nt_dims = (((1,), (1,)), ((), ()))

    @pl.when(j == 0)
    def _():
        m_ref[...] = jnp.full(m_ref.shape, NEG_BIG, F32)
        l_ref[...] = jnp.zeros(l_ref.shape, F32)
        for acc_ref in acc_refs:
            acc_ref[...] = jnp.zeros(acc_ref.shape, F32)

    def group(rows, kw):
        lane = lax.broadcasted_iota(I32, (ROW_BLOCK, LANES), 1)
        bias = jnp.where(mask_ref[rows, :kw].astype(I32) != 0, 0.0, NEG_BIG)
        m_all = m_ref[rows, :]
        m_loc = m_all
        for h in range(A_HEADS):
            sl = slice(h * HEAD_DIM, (h + 1) * HEAD_DIM)
            s = lax.dot_general(q_ref[rows, sl], k_ref[:kw, sl], nt_dims, preferred_element_type=F32) + bias
            s_ref[h, :, :kw] = s
            m_loc = jnp.where(lane == h, jnp.max(s, axis=1, keepdims=True), m_loc)
        m_new = jnp.maximum(m_all, m_loc)
        alpha = jnp.exp2(m_all - m_new)
        l_out = alpha * l_ref[rows, :]
        for h in range(A_HEADS):
            p = jnp.exp2(s_ref[h, :, :kw] - m_new[:, h:h + 1])
            p_ref[h, :, :kw] = p.astype(BF16)
            l_out = l_out + jnp.where(lane == h, jnp.sum(p, axis=1, keepdims=True), 0.0)
        for h in range(A_HEADS):
            sl = slice(h * HEAD_DIM, (h + 1) * HEAD_DIM)
            acc_refs[h][rows, :] = alpha[:, h:h + 1] * acc_refs[h][rows, :] + jnp.dot(
                p_ref[h, :, :kw], v_ref[:kw, sl], preferred_element_type=F32)
        m_ref[rows, :] = m_new
        l_ref[rows, :] = l_out

    @pl.when(j < i)
    def _():
        def body(r, carry):
            group(pl.ds(pl.multiple_of(r * ROW_BLOCK, ROW_BLOCK), ROW_BLOCK), tt)
            return carry

        lax.fori_loop(0, tt // ROW_BLOCK, body, 0)

    @pl.when(j == i)
    def _():
        for r in range(tt // ROW_BLOCK):
            group(slice(r * ROW_BLOCK, (r + 1) * ROW_BLOCK), (r + 1) * ROW_BLOCK)
        l_all = l_ref[...]
        for h in range(A_HEADS):
            sl = slice(h * HEAD_DIM, (h + 1) * HEAD_DIM)
            o_ref[:, sl] = (acc_refs[h][...] / l_all[:, h:h + 1]).astype(o_ref.dtype)


def _attn_a_call(qk, v, mask, bsz, tp, tt):
    nb = tp // tt
    w = A_WIDTH
    q_tile, k_tile = _causal_tile_pairs(nb)
    grid_spec = pltpu.PrefetchScalarGridSpec(
        num_scalar_prefetch=2,
        grid=(bsz, q_tile.shape[0]),
        in_specs=[pl.BlockSpec((tt, w), lambda b, t, qt, kt: (b * nb + qt[t], 0)),
                  pl.BlockSpec((tt, w), lambda b, t, qt, kt: (b * nb + kt[t], 1)),
                  pl.BlockSpec((tt, w), lambda b, t, qt, kt: (b * nb + kt[t], 0)),
                  pl.BlockSpec((tt, tt), lambda b, t, qt, kt: (b * nb + qt[t], kt[t]))],
        out_specs=pl.BlockSpec((tt, w), lambda b, t, qt, kt: (b * nb + qt[t], 0)),
        scratch_shapes=[pltpu.VMEM((tt, LANES), F32), pltpu.VMEM((tt, LANES), F32),
                        pltpu.VMEM((A_HEADS, ROW_BLOCK, tt), F32),
                        pltpu.VMEM((A_HEADS, ROW_BLOCK, tt), BF16)]
        + [pltpu.VMEM((tt, HEAD_DIM), F32) for _ in range(A_HEADS)],
    )
    return pl.pallas_call(
        _attn_a_kernel,
        grid_spec=grid_spec,
        out_shape=jax.ShapeDtypeStruct((bsz * tp, w), BF16),
        compiler_params=_params(("parallel", "arbitrary")),
    )(q_tile, k_tile, qk, qk, v, mask)


def _attn_b_kernel(qt_ref, kt_ref, q_ref, k_ref, v_ref, lq1_ref, lk1_ref, lq2_ref, lk2_ref, g_ref, o_ref,
                   m_ref, l_ref, s_ref, p_ref, acc_ref, *, front, lambda_init):
    t = pl.program_id(1)
    i = qt_ref[t]
    j = kt_ref[t]
    tt = q_ref.shape[0]
    nt_dims = (((1,), (1,)), ((), ()))
    dv = 2 * HEAD_DIM
    n_set = 2 * B_HEADS

    @pl.when(j == 0)
    def _():
        m_ref[...] = jnp.full(m_ref.shape, NEG_BIG, F32)
        l_ref[...] = jnp.zeros(l_ref.shape, F32)
        acc_ref[...] = jnp.zeros(acc_ref.shape, F32)

    def group(rows, r0, kw, masked):
        lane = lax.broadcasted_iota(I32, (ROW_BLOCK, LANES), 1)
        if masked:
            q_idx = i * tt + r0 + lax.broadcasted_iota(I32, (ROW_BLOCK, kw), 0)
            k_idx = j * tt + lax.broadcasted_iota(I32, (ROW_BLOCK, kw), 1)
            bias = jnp.where((k_idx <= q_idx) & (k_idx >= front), 0.0, NEG_BIG)
        m_all = m_ref[rows, :]
        m_loc = m_all
        for c in range(n_set):
            sl = slice(c * HEAD_DIM, (c + 1) * HEAD_DIM)
            s = lax.dot_general(q_ref[rows, sl], k_ref[:kw, sl], nt_dims, preferred_element_type=F32)
            if masked:
                s = s + bias
            s_ref[c, :, :kw] = s
            m_loc = jnp.where(lane == c, jnp.max(s, axis=1, keepdims=True), m_loc)
        m_new = jnp.maximum(m_all, m_loc)
        alpha = jnp.exp2(m_all - m_new)
        l_out = alpha * l_ref[rows, :]
        for c in range(n_set):
            p = jnp.exp2(s_ref[c, :, :kw] - m_new[:, c:c + 1])
            p_ref[c, :, :kw] = p.astype(BF16)
            l_out = l_out + jnp.where(lane == c, jnp.sum(p, axis=1, keepdims=True), 0.0)
        for c in range(n_set):
            vh = v_ref[:kw, (c // 2) * dv:(c // 2 + 1) * dv]
            acc_ref[c, rows, :] = alpha[:, c:c + 1] * acc_ref[c, rows, :] + jnp.dot(
                p_ref[c, :, :kw], vh, preferred_element_type=F32)
        m_ref[rows, :] = m_new
        l_ref[rows, :] = l_out

    def full_tile(masked):
        def body(r, carry):
            r0 = pl.multiple_of(r * ROW_BLOCK, ROW_BLOCK)
            group(pl.ds(r0, ROW_BLOCK), r0, tt, masked)
            return carry

        lax.fori_loop(0, tt // ROW_BLOCK, body, 0)

    pl.when((j == 0) & (j < i))(functools.partial(full_tile, True))
    pl.when((j > 0) & (j < i))(functools.partial(full_tile, False))

    @pl.when(j == i)
    def _():
        for r in range(tt // ROW_BLOCK):
            group(slice(r * ROW_BLOCK, (r + 1) * ROW_BLOCK), r * ROW_BLOCK, (r + 1) * ROW_BLOCK, True)
        lam = (jnp.exp(jnp.sum(lq1_ref[...] * lk1_ref[...], axis=1, keepdims=True))
               - jnp.exp(jnp.sum(lq2_ref[...] * lk2_ref[...], axis=1, keepdims=True))
               + lambda_init)
        g = g_ref[...]
        l_all = l_ref[...]
        for h in range(B_HEADS):
            o = (acc_ref[2 * h] / l_all[:, 2 * h:2 * h + 1]
                 - lam * (acc_ref[2 * h + 1] / l_all[:, 2 * h + 1:2 * h + 2]))
            ms = jnp.mean(o * o, axis=-1, keepdims=True)
            y = o * lax.rsqrt(ms + RMS_EPS) * g
            o_ref[:, h * dv:(h + 1) * dv] = (y * (1.0 - lambda_init)).astype(o_ref.dtype)


def _attn_b_call(qk, v, lq1, lk1, lq2, lk2, subln_g, bsz, tp, tt, front, lambda_init):
    nb = tp // tt
    w = B_QK_WIDTH
    dv = 2 * HEAD_DIM
    vec = lambda a: a.reshape(1, -1).astype(F32)
    small = lambda n: pl.BlockSpec((1, n), lambda b, t, qt, kt: (0, 0))
    q_tile, k_tile = _causal_tile_pairs(nb)
    grid_spec = pltpu.PrefetchScalarGridSpec(
        num_scalar_prefetch=2,
        grid=(bsz, q_tile.shape[0]),
        in_specs=[pl.BlockSpec((tt, w), lambda b, t, qt, kt: (b * nb + qt[t], 2)),
                  pl.BlockSpec((tt, w), lambda b, t, qt, kt: (b * nb + kt[t], 3)),
                  pl.BlockSpec((tt, B_V_WIDTH), lambda b, t, qt, kt: (b * nb + kt[t], 1)),
                  small(HEAD_DIM), small(HEAD_DIM), small(HEAD_DIM), small(HEAD_DIM), small(dv)],
        out_specs=pl.BlockSpec((tt, B_V_WIDTH), lambda b, t, qt, kt: (b * nb + qt[t], 0)),
        scratch_shapes=[pltpu.VMEM((tt, LANES), F32),
                        pltpu.VMEM((tt, LANES), F32),
                        pltpu.VMEM((2 * B_HEADS, ROW_BLOCK, tt), F32),
                        pltpu.VMEM((2 * B_HEADS, ROW_BLOCK, tt), BF16),
                        pltpu.VMEM((2 * B_HEADS, tt, dv), F32)],
    )
    return pl.pallas_call(
        functools.partial(_attn_b_kernel, front=front, lambda_init=lambda_init),
        grid_spec=grid_spec,
        out_shape=jax.ShapeDtypeStruct((bsz * tp, B_V_WIDTH), BF16),
        compiler_params=_params(("parallel", "arbitrary")),
    )(q_tile, k_tile, qk, qk, v, vec(lq1), vec(lk1), vec(lq2), vec(lk2), vec(subln_g))


def _merge_router_kernel(oa_ref, ob_ref, ga_ref, gb_ref, h_ref, wua_ref, wub_ref, wo_ref, g_ref,
                         wr_ref, br_ref, h1_ref, hfp_ref, route_ref, gate_ref, cnt_ref, carry_ref,
                         *, tp, front):
    i = pl.program_id(0)
    tm = oa_ref.shape[0]
    d = h_ref.shape[1]

    @pl.when(i == 0)
    def _():
        carry_ref[...] = jnp.zeros(carry_ref.shape, F32)

    ua = jnp.dot(oa_ref[...], wua_ref[...], preferred_element_type=F32)
    ub = jnp.dot(ob_ref[...], wub_ref[...], preferred_element_type=F32)
    merged = ga_ref[...].astype(F32) * ua + gb_ref[...].astype(F32) * ub
    mix = jnp.dot(merged.astype(BF16), wo_ref[...], preferred_element_type=F32)
    h1 = h_ref[...] + mix
    h1_ref[...] = h1

    ms = jnp.mean(h1 * h1, axis=-1, keepdims=True)
    hf = (h1 * lax.rsqrt(ms + RMS_EPS) * g_ref[...]).astype(BF16)

    hf_bits = pltpu.bitcast(hf.astype(F32), jnp.uint32)
    hfp_ref[...] = (hf_bits[:, :d // 2] & jnp.uint32(0xFFFF0000)) | (hf_bits[:, d // 2:] >> 16)

    logits = jnp.dot(hf, wr_ref[...], preferred_element_type=F32) + br_ref[...]
    lane = lax.broadcasted_iota(I32, (tm, LANES), 1)
    row = (i * tm + lax.broadcasted_iota(I32, (tm, 1), 0)) % tp
    routed = row >= front

    work = logits
    vals, idxs, onehots = [], [], []
    for _ in range(TOP_K):
        v = jnp.max(work, axis=1, keepdims=True)
        e = jnp.min(jnp.where(work == v, lane, LANES), axis=1, keepdims=True)
        hit = lane == e
        vals.append(v)
        idxs.append(e)
        onehots.append(hit)
        work = jnp.where(hit, -jnp.inf, work)

    exps = [jnp.exp(v - vals[0]) for v in vals]
    denom = exps[0] + exps[1] + exps[2] + exps[3]

    member = jnp.zeros((tm, LANES), F32)
    for hit in onehots:
        member = member + jnp.where(hit & routed, 1.0, 0.0)
    r_i = lax.broadcasted_iota(I32, (tm, tm), 0)
    c_i = lax.broadcasted_iota(I32, (tm, tm), 1)
    lower = jnp.where(c_i < r_i, 1.0, 0.0).astype(BF16)
    before = jnp.dot(lower, member.astype(BF16), preferred_element_type=F32) + carry_ref[...]
    carry_ref[...] = carry_ref[...] + jnp.sum(member, axis=0, keepdims=True)

    route = jnp.zeros((tm, LANES), I32)
    gates = jnp.zeros((tm, LANES), F32)
    for k in range(TOP_K):
        rank = jnp.sum(jnp.where(onehots[k], before, 0.0), axis=1, keepdims=True)
        route = jnp.where(lane == k, idxs[k], route)
        route = jnp.where(lane == TOP_K + k, rank.astype(I32), route)
        gates = jnp.where(lane == k, exps[k] / denom, gates)
    route_ref[...] = route
    gate_ref[...] = gates
    cnt_ref[...] = jnp.broadcast_to(carry_ref[...], cnt_ref.shape)


def _merge_router_call(o_a, o_b, gates, h0, wua, wub, wo, g_ffn, wr, br, tp, front, tm):
    rows, d = h0.shape
    nt = rows // tm
    once = pl.Buffered(1)
    const = lambda shape: pl.BlockSpec(shape, lambda i: (0, 0), pipeline_mode=once)
    return pl.pallas_call(
        functools.partial(_merge_router_kernel, tp=tp, front=front),
        grid=(nt,),
        in_specs=[pl.BlockSpec((tm, A_WIDTH), lambda i: (i, 0)),
                  pl.BlockSpec((tm, B_V_WIDTH), lambda i: (i, 0)),
                  pl.BlockSpec((tm, d), lambda i: (i, 0)),
                  pl.BlockSpec((tm, d), lambda i: (i, 1)),
                  pl.BlockSpec((tm, d), lambda i: (i, 0)),
                  const((A_WIDTH, d)), const((B_V_WIDTH, d)), const((d, d)),
                  const((1, d)), const((d, LANES)), const((1, LANES))],
        out_specs=[pl.BlockSpec((tm, d), lambda i: (i, 0)),
                   pl.BlockSpec((tm, d // 2), lambda i: (i, 0)),
                   pl.BlockSpec((tm, LANES), lambda i: (i, 0)),
                   pl.BlockSpec((tm, LANES), lambda i: (i, 0)),
                   pl.BlockSpec((8, LANES), lambda i: (0, 0))],
        out_shape=[jax.ShapeDtypeStruct((rows, d), F32),
                   jax.ShapeDtypeStruct((rows, d // 2), jnp.uint32),
                   jax.ShapeDtypeStruct((rows, LANES), I32),
                   jax.ShapeDtypeStruct((rows, LANES), F32),
                   jax.ShapeDtypeStruct((8, LANES), F32)],
        scratch_shapes=[pltpu.VMEM((1, LANES), F32)],
        compiler_params=_params(("arbitrary",)),
    )(o_a, o_b, gates, gates, h0, wua, wub, wo, g_ffn, wr, br)


def _dispatch_kernel(dest_ref, hfp_ref, xb_in_ref, xb_ref, sem):
    del xb_in_ref
    tm = hfp_ref.shape[0]

    def row_copy(r, k):
        d = dest_ref[r * TOP_K + k]
        return pltpu.make_async_copy(hfp_ref.at[pl.ds(r, 1), :], xb_ref.at[pl.ds(d, 1), :], sem)

    def issue(r, carry):
        for k in range(TOP_K):
            row_copy(r, k).start()
        return carry

    lax.fori_loop(0, tm, issue, 0, unroll=8)

    def drain(r, carry):
        for k in range(TOP_K):
            row_copy(r, k).wait()
        return carry

    lax.fori_loop(0, tm, drain, 0, unroll=8)


def _dispatch_call(dest_flat, hfp, n_rows_total, tm):
    rows, half = hfp.shape
    xb0 = jnp.zeros((n_rows_total, half), jnp.uint32)
    return pl.pallas_call(
        _dispatch_kernel,
        grid=(rows // tm,),
        in_specs=[pl.BlockSpec((tm * TOP_K,), lambda i: (i,), memory_space=pltpu.SMEM),
                  pl.BlockSpec((tm, half), lambda i: (i, 0)),
                  pl.BlockSpec(memory_space=pl.ANY)],
        out_specs=pl.BlockSpec(memory_space=pl.ANY),
        out_shape=jax.ShapeDtypeStruct((n_rows_total, half), jnp.uint32),
        scratch_shapes=[pltpu.SemaphoreType.DMA(())],
        input_output_aliases={2: 0},
        compiler_params=_params(("arbitrary",)),
    )(dest_flat, hfp, xb0)


def _w1_prep_kernel(w_ref, o_ref):
    n = 2 * LANES
    s_i = lax.broadcasted_iota(I32, (n, n), 0)
    j_i = lax.broadcasted_iota(I32, (n, n), 1)
    src = jnp.where(j_i < LANES, 2 * j_i, 2 * (j_i - LANES) + 1)
    perm = jnp.where(s_i == src, 1.0, 0.0).astype(BF16)
    for c in range(w_ref.shape[2] // n):
        blk = w_ref[0, :, c * n:(c + 1) * n].astype(BF16)
        o_ref[0, :, c * n:(c + 1) * n] = jnp.dot(blk, perm, preferred_element_type=F32).astype(BF16)


def _w1_prep_call(w1l, tk):
    n_exp, d, two_ff = w1l.shape
    return pl.pallas_call(
        _w1_prep_kernel,
        grid=(n_exp, d // tk),
        in_specs=[pl.BlockSpec((1, tk, two_ff), lambda e, k: (e, k, 0))],
        out_specs=pl.BlockSpec((1, tk, two_ff), lambda e, k: (e, k, 0)),
        out_shape=jax.ShapeDtypeStruct((n_exp, d, two_ff), BF16),
        compiler_params=_params(("parallel", "parallel")),
    )(w1l)


def _expert_kernel(be_ref, nu_ref, xb_ref, w1_ref, b1_ref, w2_ref, b2_ref, y_ref):
    i = pl.program_id(0)
    d_ff = w2_ref.shape[1]

    @pl.when(i < nu_ref[0])
    def _():
        packed = xb_ref[...]
        hi = pltpu.bitcast(packed & jnp.uint32(0xFFFF0000), F32)
        lo = pltpu.bitcast(packed << 16, F32)
        x = jnp.concatenate([hi, lo], axis=1).astype(BF16)
        u = jnp.dot(x, w1_ref[0], preferred_element_type=F32) + b1_ref[0]
        n_grp = d_ff // LANES
        glu = jnp.concatenate([u[:, 2 * c * LANES:(2 * c + 1) * LANES] for c in range(n_grp)], axis=1)
        lin = jnp.concatenate([u[:, (2 * c + 1) * LANES:(2 * c + 2) * LANES] for c in range(n_grp)], axis=1)
        x_glu = jnp.minimum(glu, SWIGLU_LIMIT)
        x_lin = jnp.clip(lin, -SWIGLU_LIMIT, SWIGLU_LIMIT)
        act = x_glu * jax.nn.sigmoid(SWIGLU_ALPHA * x_glu) * (x_lin + 1.0)
        y_ref[...] = jnp.dot(act.astype(BF16), w2_ref[0], preferred_element_type=F32) + b2_ref[0]

    @pl.when(i >= nu_ref[0])
    def _():
        y_ref[...] = jnp.zeros(y_ref.shape, F32)


def _expert_call(block_e, n_used, xb, w1p, b1p, w2b, b2, n_blocks):
    n_exp, d, two_ff = w1p.shape
    d_ff = two_ff // 2
    half = xb.shape[1]

    def blk(i, be, nu):
        return jnp.minimum(i, nu[0] - 1)

    grid_spec = pltpu.PrefetchScalarGridSpec(
        num_scalar_prefetch=2,
        grid=(n_blocks,),
        in_specs=[pl.BlockSpec((ROW_BLOCK, half), lambda i, be, nu: (blk(i, be, nu), 0)),
                  pl.BlockSpec((1, d, two_ff), lambda i, be, nu: (be[blk(i, be, nu)], 0, 0)),
                  pl.BlockSpec((1, 1, two_ff), lambda i, be, nu: (be[blk(i, be, nu)], 0, 0)),
                  pl.BlockSpec((1, d_ff, d), lambda i, be, nu: (be[blk(i, be, nu)], 0, 0)),
                  pl.BlockSpec((1, 1, d), lambda i, be, nu: (be[blk(i, be, nu)], 0, 0))],
        out_specs=pl.BlockSpec((ROW_BLOCK, d), lambda i, be, nu: (i, 0)),
    )
    return pl.pallas_call(
        _expert_kernel,
        grid_spec=grid_spec,
        out_shape=jax.ShapeDtypeStruct((n_blocks * ROW_BLOCK, d), F32),
        compiler_params=_params(("arbitrary",), vmem=60 * 1024 * 1024),
    )(block_e, n_used, xb, w1p, b1p.reshape(n_exp, 1, two_ff), w2b, b2.reshape(n_exp, 1, d))


def _combine_kernel(dest_ref, h1_ref, gate_ref, g_ref, yb_ref, o_ref, ybuf_ref, sem):
    tm = h1_ref.shape[0]

    def row_copy(r, k):
        d = dest_ref[r * TOP_K + k]
        return pltpu.make_async_copy(yb_ref.at[pl.ds(d, 1), :], ybuf_ref.at[k, pl.ds(r, 1), :], sem)

    def issue(r, carry):
        for k in range(TOP_K):
            row_copy(r, k).start()
        return carry

    lax.fori_loop(0, tm, issue, 0, unroll=8)

    def drain(r, carry):
        for k in range(TOP_K):
            row_copy(r, k).wait()
        return carry

    lax.fori_loop(0, tm, drain, 0, unroll=8)

    gate = gate_ref[...]
    h2 = h1_ref[...]
    for k in range(TOP_K):
        h2 = h2 + gate[:, k:k + 1] * ybuf_ref[k]
    ms = jnp.mean(h2 * h2, axis=-1, keepdims=True)
    o_ref[...] = h2 * lax.rsqrt(ms + RMS_EPS) * g_ref[...]


def _combine_call(dest_flat, h1, gate, g_final, yb, bsz, s_len, tp):
    d = h1.shape[1]
    tm = ROW_BLOCK
    ns = s_len // tm
    nb = tp // tm
    skip = (tp - s_len) // tm

    def src(i):
        return (i // ns) * nb + skip + i % ns

    return pl.pallas_call(
        _combine_kernel,
        grid=(bsz * ns,),
        in_specs=[pl.BlockSpec((tm * TOP_K,), lambda i: (src(i),), memory_space=pltpu.SMEM),
                  pl.BlockSpec((tm, d), lambda i: (src(i), 0)),
                  pl.BlockSpec((tm, LANES), lambda i: (src(i), 0)),
                  pl.BlockSpec((1, d), lambda i: (0, 0)),
                  pl.BlockSpec(memory_space=pl.ANY)],
        out_specs=pl.BlockSpec((tm, d), lambda i: (i, 0)),
        out_shape=jax.ShapeDtypeStruct((bsz * s_len, d), F32),
        scratch_shapes=[pltpu.VMEM((TOP_K, tm, d), F32), pltpu.SemaphoreType.DMA(())],
        compiler_params=_params(("arbitrary",)),
    )(dest_flat, h1, gate, g_final.reshape(1, d), yb)


def _rope_tables(pos, dim, reps):
    inv = 1.0 / (ROPE_THETA ** (jnp.arange(0, dim, 2, dtype=F32) / dim))
    ang = pos[:, None] * inv[None, :]
    cos = jnp.concatenate([jnp.cos(ang), jnp.cos(ang)], axis=-1)
    sin = jnp.concatenate([-jnp.sin(ang), jnp.sin(ang)], axis=-1)
    return jnp.tile(cos, (1, reps)), jnp.tile(sin, (1, reps))


def kernel(x, meta_tokens, norm_mix_g, w_in, lambda_q1, lambda_k1, lambda_q2, lambda_k2, subln_g, w_up_a, w_up_b, w_out, norm_ffn_g, w_router, b_router, w1, b1, w2, b2, norm_final_g):
    bsz, s_len, d = x.shape
    n_meta = meta_tokens.shape[0]
    depth = w_in.shape[0]
    n_exp = w_router.shape[-1]
    d_ff = w2.shape[-2]
    t_len = s_len + n_meta
    tp = -(-t_len // ROW_BLOCK) * ROW_BLOCK
    front = tp - t_len
    topk = min(TOPK_MAX, t_len // 4)
    rows = bsz * tp
    assert s_len % ROW_BLOCK == 0 and n_exp <= LANES and d % (2 * LANES) == 0 and d_ff % LANES == 0

    tm_proj = _pick_tile(tp, (640, 512, 256, 128))
    tt = _pick_tile(tp, (640, 512, 256, 128))
    tm_merge = _pick_tile(rows, (256, 128))

    pos = jnp.arange(tp, dtype=F32) - float(front)
    cos_h, sin_h = _rope_tables(pos, HEAD_DIM, 1)
    cos_i, sin_i = _rope_tables(pos, IDX_DIM, LANES // IDX_DIM)

    meta = jnp.broadcast_to(meta_tokens[None].astype(x.dtype), (bsz, n_meta, d))
    h = jnp.concatenate([jnp.zeros((bsz, front, d), x.dtype), meta, x], axis=1).reshape(rows, d)

    o_qa, o_ka, o_va = 0, A_WIDTH, 2 * A_WIDTH
    o_qi = 3 * A_WIDTH
    o_ki = o_qi + IDX_HEADS * IDX_DIM
    o_wi = o_ki + IDX_DIM
    o_qb = o_wi + IDX_HEADS
    o_kb = o_qb + B_QK_WIDTH
    o_vb = o_kb + B_QK_WIDTH
    o_ga = o_vb + B_V_WIDTH
    o_gb = o_ga + d

    for l in range(depth):
        lambda_init = 0.8 - 0.6 * math.exp(-0.3 * l)
        wl = w_in[l]
        col = lambda o, n: wl[:, o:o + n]
        zeros_half = jnp.zeros((d, LANES - IDX_DIM), wl.dtype)
        w_rope = jnp.concatenate([col(o_qa, A_WIDTH), col(o_ka, A_WIDTH),
                                  col(o_qb, B_QK_WIDTH), col(o_kb, B_QK_WIDTH)], axis=1).astype(BF16)
        w_val = jnp.concatenate([col(o_va, A_WIDTH), col(o_vb, B_V_WIDTH)], axis=1).astype(BF16)
        w_gate = jnp.concatenate([col(o_ga, d), col(o_gb, d)], axis=1).astype(BF16)
        w_idx = jnp.concatenate([col(o_qi, IDX_HEADS * IDX_DIM), col(o_ki, IDX_DIM), zeros_half,
                                 zeros_half, col(o_ki, IDX_DIM)], axis=1).astype(BF16)
        w_wi = jnp.concatenate([col(o_wi, IDX_HEADS), jnp.zeros((d, LANES - IDX_HEADS), wl.dtype)],
                               axis=1).astype(BF16)

        hn = _rmsnorm_call(h, norm_mix_g[l], tm_proj)
        q_scale = HEAD_DIM ** -0.5 * math.log2(math.e)
        col_scale = jnp.concatenate([jnp.full((A_WIDTH,), q_scale, F32), jnp.ones((A_WIDTH,), F32),
                                     jnp.full((B_QK_WIDTH,), q_scale, F32), jnp.ones((B_QK_WIDTH,), F32)])
        qk = _proj_call(hn, w_rope, BF16, "rope", tm_proj, 1024, cos_h, sin_h, HEAD_DIM // 2, tp=tp,
                        col_scale=col_scale.reshape(1, -1))
        val = _proj_call(hn, w_val, BF16, "none", tm_proj, 1024)
        gates = _proj_call(hn, w_gate, BF16, "sigmoid", tm_proj, _pick_tile(2 * d, (1024, 512, 256)))
        idx_qk = _proj_call(hn, w_idx, BF16, "rope", tm_proj, (IDX_HEADS * IDX_DIM + 2 * LANES) // 2,
                            cos_i, sin_i, IDX_DIM // 2, tp=tp)
        wi = _proj_call(hn, w_wi, F32, "scale", tm_proj, LANES,
                        scale=IDX_HEADS ** -0.5 * IDX_DIM ** -0.5)

        n_qcols = IDX_HEADS * IDX_DIM
        nc = tp // LANES
        k_cat = jnp.stack([idx_qk[:, n_qcols:n_qcols + LANES].reshape(bsz, nc, LANES, LANES),
                           idx_qk[:, n_qcols + LANES:].reshape(bsz, nc, LANES, LANES)], axis=2)
        k_cat = jnp.pad(k_cat, ((0, 0), (0, 1), (0, 0), (0, 0), (0, 0))).reshape(
            bsz * 2 * (tp + LANES), LANES)
        mask = _indexer_call(idx_qk, k_cat, wi, bsz, tp, front, topk)
        o_a = _attn_a_call(qk, val, mask, bsz, tp, tt)
        o_b = _attn_b_call(qk, val, lambda_q1[l], lambda_k1[l], lambda_q2[l], lambda_k2[l], subln_g[l],
                           bsz, tp, tt, front, lambda_init)

        wr = jnp.concatenate([w_router[l], jnp.zeros((d, LANES - n_exp), F32)], axis=1).astype(BF16)
        br = jnp.concatenate([b_router[l], jnp.full((LANES - n_exp,), NEG_BIG, F32)]).reshape(1, LANES)
        h1, hfp, route, gate, cnt = _merge_router_call(
            o_a, o_b, gates, h, w_up_a[l].astype(BF16), w_up_b[l].astype(BF16), w_out[l].astype(BF16),
            norm_ffn_g[l].reshape(1, d), wr, br, tp, front, tm_merge)

        n_assign = bsz * t_len * TOP_K
        n_blocks = (n_assign + n_exp * (ROW_BLOCK - 1) + ROW_BLOCK - 1) // ROW_BLOCK
        n_rows = n_blocks * ROW_BLOCK
        counts = cnt[0, :n_exp].astype(I32)
        padded = (counts + ROW_BLOCK - 1) // ROW_BLOCK * ROW_BLOCK
        pad_end = jnp.cumsum(padded)
        pad_start = pad_end - padded
        n_used = (pad_end[-1:] // ROW_BLOCK).astype(I32)
        blk_start = jnp.arange(n_blocks, dtype=I32) * ROW_BLOCK
        block_e = jnp.minimum(jnp.sum((pad_end[None, :] <= blk_start[:, None]).astype(I32), axis=1),
                              n_exp - 1).astype(I32)
        top_e = route[:, :TOP_K]
        rank = route[:, TOP_K:2 * TOP_K]
        row_id = jnp.arange(rows, dtype=I32)
        is_pad = (row_id % tp) < front
        n_trash = bsz * front * TOP_K
        trash = (n_rows + ((row_id // tp) * front + row_id % tp)[:, None] * TOP_K
                 + jnp.arange(TOP_K, dtype=I32)[None, :])
        dest = jnp.where(is_pad[:, None], trash, pad_start[top_e] + rank)
        dest_flat = dest.reshape(-1).astype(I32)
        n_rows_total = n_rows + -(-max(n_trash, 1) // 8) * 8

        xb = _dispatch_call(dest_flat, hfp, n_rows_total, tm_merge)

        w1p = _w1_prep_call(w1[l], _pick_tile(d, (256, 128)))
        b1p = b1[l].reshape(n_exp, d_ff // LANES, LANES, 2).transpose(0, 1, 3, 2).reshape(n_exp, 2 * d_ff)
        yb = _expert_call(block_e, n_used, xb, w1p, b1p, w2[l].astype(BF16), b2[l], n_blocks)

        if l + 1 < depth:
            raise NotImplementedError("only the single-layer configuration is implemented")
        out = _combine_call(dest_flat, h1, gate, norm_final_g, yb, bsz, s_len, tp)
    return out.reshape(bsz, s_len, d)
```

```python
import functools
import math

import jax
import jax.numpy as jnp
import numpy as np
from jax import lax
from jax.experimental import pallas as pl
from jax.experimental.pallas import tpu as pltpu

HEAD_DIM = 128
A_HEADS = 8
IDX_HEADS = 16
IDX_DIM = 64
TOPK_MAX = 256
B_HEADS = 4
TOP_K = 4
SWIGLU_LIMIT = 7.0
SWIGLU_ALPHA = 1.702
ROPE_THETA = 10000.0
RMS_EPS = 1e-5
ROW_BLOCK = 128
LANES = 128
NEG_BIG = -1e30
VMEM_LIMIT = 56 * 1024 * 1024
SCORE_CHUNKS = 4
EXP_UNIT = 1 << 23
SPEC_BUCKETS = 4
SPEC_MIN_CHUNKS = 6

A_WIDTH = A_HEADS * HEAD_DIM
B_QK_WIDTH = B_HEADS * 2 * HEAD_DIM
B_V_WIDTH = B_HEADS * 2 * HEAD_DIM

F32 = jnp.float32
BF16 = jnp.bfloat16
I32 = jnp.int32

KEY_NEG_INF = int(np.array(0xFF800000, np.uint32).view(np.int32)) ^ 0x7FFFFFFF
INT_MIN = -(2 ** 31)


def _pick_tile(n, candidates):
    for c in candidates:
        if n % c == 0:
            return c
    raise ValueError(f"no tile for {n}")


def _params(sem, vmem=VMEM_LIMIT):
    return pltpu.CompilerParams(dimension_semantics=sem, vmem_limit_bytes=vmem)


def _rmsnorm_kernel(x_ref, g_ref, o_ref):
    x = x_ref[...]
    ms = jnp.mean(x * x, axis=-1, keepdims=True)
    o_ref[...] = (x * lax.rsqrt(ms + RMS_EPS) * g_ref[...]).astype(o_ref.dtype)


def _rmsnorm_call(x2d, g, tm):
    rows, d = x2d.shape
    return pl.pallas_call(
        _rmsnorm_kernel,
        grid=(rows // tm,),
        in_specs=[pl.BlockSpec((tm, d), lambda i: (i, 0)),
                  pl.BlockSpec((1, d), lambda i: (0, 0))],
        out_specs=pl.BlockSpec((tm, d), lambda i: (i, 0)),
        out_shape=jax.ShapeDtypeStruct((rows, d), BF16),
        compiler_params=_params(("parallel",)),
    )(x2d, g.reshape(1, d))


def _proj_kernel(*refs, mode, rot_half, scale):
    if mode == "rope":
        x_ref, w_ref, cos_ref, sin_ref, cs_ref, o_ref = refs
    else:
        x_ref, w_ref, o_ref = refs
    acc = jnp.dot(x_ref[...], w_ref[...], preferred_element_type=F32)
    if mode == "rope":
        acc = acc * cs_ref[...]
        cos = cos_ref[...]
        sin = sin_ref[...]
        tn = acc.shape[1]
        lane = lax.broadcasted_iota(I32, (acc.shape[0], LANES), 1)
        outs = []
        for c in range(tn // LANES):
            xh = acc[:, c * LANES:(c + 1) * LANES]
            if rot_half == LANES // 2:
                rot = pltpu.roll(xh, LANES // 2, 1)
            else:
                fwd = pltpu.roll(xh, LANES - rot_half, 1)
                bwd = pltpu.roll(xh, rot_half, 1)
                rot = jnp.where((lane % (2 * rot_half)) < rot_half, fwd, bwd)
            outs.append(xh * cos + rot * sin)
        acc = outs[0] if len(outs) == 1 else jnp.concatenate(outs, axis=1)
    elif mode == "sigmoid":
        acc = jax.nn.sigmoid(acc)
    elif mode == "scale":
        acc = acc * scale
    o_ref[...] = acc.astype(o_ref.dtype)


def _proj_call(xn, w, out_dtype, mode, tm, tn, cos=None, sin=None, rot_half=0, scale=1.0, tp=None,
               col_scale=None):
    rows, d = xn.shape
    n = w.shape[1]
    in_specs = [pl.BlockSpec((tm, d), lambda i, j: (i, 0)),
                pl.BlockSpec((d, tn), lambda i, j: (0, j))]
    args = [xn, w]
    if mode == "rope":
        nt = tp // tm
        if col_scale is None:
            col_scale = jnp.ones((1, n), F32)
        in_specs += [pl.BlockSpec((tm, LANES), lambda i, j: (i % nt, 0)),
                     pl.BlockSpec((tm, LANES), lambda i, j: (i % nt, 0)),
                     pl.BlockSpec((1, tn), lambda i, j: (0, j))]
        args += [cos, sin, col_scale]
    return pl.pallas_call(
        functools.partial(_proj_kernel, mode=mode, rot_half=rot_half, scale=scale),
        grid=(rows // tm, n // tn),
        in_specs=in_specs,
        out_specs=pl.BlockSpec((tm, tn), lambda i, j: (i, j)),
        out_shape=jax.ShapeDtypeStruct((rows, n), out_dtype),
        compiler_params=_params(("parallel", "arbitrary")),
    )(*args)


def _indexer_kernel(qi_ref, kc_ref, wi_ref, mask_ref, keys_ref, qs_ref, p_ref,
                    *, front, topk, n_chunks_total):
    i = pl.program_id(1)
    n_chunk = i + 1
    n_pair = (n_chunk + 1) // 2
    tq = qi_ref.shape[0]
    n_hp = IDX_HEADS // 2
    q_idx = i * tq + lax.broadcasted_iota(I32, (LANES, tq), 1)
    k_off = lax.broadcasted_iota(I32, (LANES, tq), 0)

    w_t = wi_ref[...].T
    for p in range(n_hp):
        qs_ref[p * tq:(p + 1) * tq, :] = qi_ref[:, p * LANES:(p + 1) * LANES]

    nt_dims = (((1,), (1,)), ((), ()))

    def score_body(c4, carry):
        for u in range(SCORE_CHUNKS):
            c = SCORE_CHUNKS * c4 + u
            off = pl.multiple_of(c * LANES, LANES)
            kc = kc_ref[pl.ds(pl.multiple_of(c * 2 * LANES, 2 * LANES), 2 * LANES), :]
            acc = jnp.zeros((LANES, tq), F32)
            for g in range(n_hp // 2):
                qg = qs_ref[2 * g * tq:(2 * g + 2) * tq, :]
                st = lax.dot_general(kc, qg, nt_dims, preferred_element_type=F32)
                for v in range(2):
                    h0 = 2 * (2 * g + v)
                    sp = st[:, v * tq:(v + 1) * tq]
                    acc = acc + w_t[h0:h0 + 1, :] * jnp.maximum(sp[:LANES], 0.0)
                    acc = acc + w_t[h0 + 1:h0 + 2, :] * jnp.maximum(sp[LANES:], 0.0)
            k_idx = off + k_off
            valid = (k_idx <= q_idx) & (k_idx >= front)
            acc = jnp.where(acc == 0.0, 0.0, acc)
            score = jnp.where(valid, acc, -jnp.inf)
            bits = pltpu.bitcast(score, I32)
            key = jnp.where(bits < 0, bits ^ 0x7FFFFFFF, bits)
            keys_ref[pl.ds(off, LANES), :] = key
            carry = jnp.maximum(carry, key)
        return carry

    key_max = lax.fori_loop(0, (n_chunk + SCORE_CHUNKS - 1) // SCORE_CHUNKS, score_body,
                            jnp.full((LANES, tq), INT_MIN, I32))

    def count(pred_fn):
        def body(c2, cnt):
            off = pl.multiple_of(c2 * 2 * LANES, 2 * LANES)
            k0 = keys_ref[pl.ds(off, LANES), :]
            k1 = keys_ref[pl.ds(off + LANES, LANES), :]
            cnt = cnt + jnp.where(pred_fn(k0, off + k_off), 1, 0)
            return cnt + jnp.where(pred_fn(k1, off + LANES + k_off), 1, 0)
        cnt = lax.fori_loop(0, n_pair, body, jnp.zeros((LANES, tq), I32))
        return jnp.sum(cnt, axis=0, keepdims=True)

    bits_per_check = 4

    def bit_cond(state):
        it, _, cnt_t = state
        return (it < 32) & (jnp.max(jnp.where(cnt_t != topk, 1, 0)) > 0)

    def bit_body(state):
        it, t, cnt_t = state
        for u in range(bits_per_check):
            trial = t + jnp.left_shift(jnp.int32(1), 31 - u - it)
            cnt = count(lambda k, _: k >= trial)
            take = cnt >= topk
            t = jnp.where(take, trial, t)
            cnt_t = jnp.where(take, cnt, cnt_t)
        return it + bits_per_check, t, cnt_t

    full_search = (jnp.int32(0), jnp.full((1, tq), INT_MIN, I32), jnp.full((1, tq), -1, I32))

    def bucket_start():
        base = jnp.max(key_max, axis=0, keepdims=True) & jnp.int32(-EXP_UNIT)
        found = jnp.zeros((1, tq), I32)
        t_b = jnp.full((1, tq), INT_MIN, I32)
        c_b = jnp.full((1, tq), -1, I32)
        for d in range(SPEC_BUCKETS):
            start = base - d * EXP_UNIT
            cnt = count(lambda k, _: k >= start)
            first = (cnt >= topk) & (found == 0)
            t_b = jnp.where(first, start, t_b)
            c_b = jnp.where(first, cnt, c_b)
            found = jnp.where(cnt >= topk, 1, found)
        usable = (found > 0) & (base < jnp.int32(0x7F800000))
        every = jnp.min(jnp.where(usable, 1, 0)) > 0
        return (jnp.where(every, jnp.int32(8), jnp.int32(0)), jnp.where(every, t_b, INT_MIN),
                jnp.where(every, c_b, -1))

    state0 = lax.cond(n_chunk >= SPEC_MIN_CHUNKS, bucket_start, lambda: full_search)
    _, thr, cnt_ge = lax.while_loop(bit_cond, bit_body, state0)

    finite = thr > KEY_NEG_INF
    tie_rows = finite & (cnt_ge > topk)
    p_ref[...] = jnp.full(p_ref.shape, (n_chunks_total + 1) * LANES, I32)

    @pl.when(jnp.max(jnp.where(tie_rows, 1, 0)) > 0)
    def _():
        cnt_gt = count(lambda k, _: k > thr)
        need = topk - cnt_gt

        def idx_body(it, p):
            trial = p + jnp.left_shift(jnp.int32(1), 14 - it)
            cnt = count(lambda k, idx: (k == thr) & (idx < trial))
            return jnp.where(cnt < need, trial, p)

        p = lax.fori_loop(0, 15, idx_body, jnp.zeros((1, tq), I32))
        p = jnp.where(tie_rows, p, (n_chunks_total + 1) * LANES)
        p_ref[...] = jnp.broadcast_to(p, p_ref.shape)

    p_lim = p_ref[0:1, :]
    eye = jnp.where(lax.broadcasted_iota(I32, (tq, tq), 0) == lax.broadcasted_iota(I32, (tq, tq), 1),
                    1.0, 0.0).astype(BF16)

    per_trip = 4

    def write_body(c4, carry):
        for u in range(per_trip):
            c = jnp.minimum(per_trip * c4 + u, n_chunk - 1)
            off = pl.multiple_of(c * LANES, LANES)
            k = keys_ref[pl.ds(off, LANES), :]
            k_idx = off + k_off
            sel = (k > thr) | ((k == thr) & (k_idx <= p_lim))
            valid = (k_idx <= q_idx) & (k_idx >= front)
            sel_t = jnp.where(sel & valid, 1.0, 0.0).astype(BF16)
            sel_qk = lax.dot_general(eye, sel_t, nt_dims, preferred_element_type=F32)
            mask_ref[:, pl.ds(off, LANES)] = sel_qk.astype(jnp.int8)
        return carry

    lax.fori_loop(0, (n_chunk + per_trip - 1) // per_trip, write_body, 0)

    def zero_body(c, carry):
        off = pl.multiple_of(c * LANES, LANES)
        mask_ref[:, pl.ds(off, LANES)] = jnp.zeros((tq, LANES), jnp.int8)
        return carry

    lax.fori_loop(n_chunk, n_chunks_total, zero_body, 0)


def _indexer_call(idx_qk, k_cat, wi, bsz, tp, front, topk):
    tq = ROW_BLOCK
    nq = tp // tq
    n_qcols = IDX_HEADS * IDX_DIM
    return pl.pallas_call(
        functools.partial(_indexer_kernel, front=front, topk=topk, n_chunks_total=nq),
        grid=(bsz, nq),
        in_specs=[pl.BlockSpec((tq, n_qcols), lambda b, i: (b * nq + i, 0)),
                  pl.BlockSpec((2 * (tp + (SCORE_CHUNKS - 1) * LANES), LANES), lambda b, i: (b, 0)),
                  pl.BlockSpec((tq, LANES), lambda b, i: (b * nq + i, 0))],
        out_specs=pl.BlockSpec((tq, tp), lambda b, i: (b * nq + i, 0)),
        out_shape=jax.ShapeDtypeStruct((bsz * tp, tp), jnp.int8),
        scratch_shapes=[pltpu.VMEM((tp + (SCORE_CHUNKS - 1) * LANES, tq), I32),
                        pltpu.VMEM((n_qcols // LANES * tq, LANES), BF16),
                        pltpu.VMEM((8, tq), I32)],
        compiler_params=_params(("parallel", "arbitrary")),
    )(idx_qk, k_cat, wi)


def _causal_tile_pairs(nb):
    pairs = [(i, j) for i in range(nb) for j in range(i + 1)]
    return (jnp.array([p[0] for p in pairs], I32), jnp.array([p[1] for p in pairs], I32))


def _attn_a_kernel(qt_ref, kt_ref, q_ref, k_ref, v_ref, mask_ref, o_ref, m_ref, l_ref, s_ref, p_ref,
                   *acc_refs):
    t = pl.program_id(1)
    i = qt_ref[t]
    j = kt_ref[t]
    tt = q_ref.shape[0]
    nt_dims = (((1,), (1,)), ((), ()))

    @pl.when(j == 0)
    def _():
        m_ref[...] = jnp.full(m_ref.shape, NEG_BIG, F32)
        l_ref[...] = jnp.zeros(l_ref.shape, F32)
        for acc_ref in acc_refs:
            acc_ref[...] = jnp.zeros(acc_ref.shape, F32)

    def group(rows, kw):
        lane = lax.broadcasted_iota(I32, (ROW_BLOCK, LANES), 1)
        bias = jnp.where(mask_ref[rows, :kw].astype(I32) != 0, 0.0, NEG_BIG)
        m_all = m_ref[rows, :]
        m_loc = m_all
        for h in range(A_HEADS):
            sl = slice(h * HEAD_DIM, (h + 1) * HEAD_DIM)
            s = lax.dot_general(q_ref[rows, sl], k_ref[:kw, sl], nt_dims, preferred_element_type=F32) + bias
            s_ref[h, :, :kw] = s
            m_loc = jnp.where(lane == h, jnp.max(s, axis=1, keepdims=True), m_loc)
        m_new = jnp.maximum(m_all, m_loc)
        alpha = jnp.exp2(m_all - m_new)
        l_out = alpha * l_ref[rows, :]
        for h in range(A_HEADS):
            p = jnp.exp2(s_ref[h, :, :kw] - m_new[:, h:h + 1])
            p_ref[h, :, :kw] = p.astype(BF16)
            l_out = l_out + jnp.where(lane == h, jnp.sum(p, axis=1, keepdims=True), 0.0)
        for h in range(A_HEADS):
            sl = slice(h * HEAD_DIM, (h + 1) * HEAD_DIM)
            acc_refs[h][rows, :] = alpha[:, h:h + 1] * acc_refs[h][rows, :] + jnp.dot(
                p_ref[h, :, :kw], v_ref[:kw, sl], preferred_element_type=F32)
        m_ref[rows, :] = m_new
        l_ref[rows, :] = l_out

    @pl.when(j < i)
    def _():
        def body(r, carry):
            group(pl.ds(pl.multiple_of(r * ROW_BLOCK, ROW_BLOCK), ROW_BLOCK), tt)
            return carry

        lax.fori_loop(0, tt // ROW_BLOCK, body, 0)

    @pl.when(j == i)
    def _():
        for r in range(tt // ROW_BLOCK):
            group(slice(r * ROW_BLOCK, (r + 1) * ROW_BLOCK), (r + 1) * ROW_BLOCK)
        l_all = l_ref[...]
        for h in range(A_HEADS):
            sl = slice(h * HEAD_DIM, (h + 1) * HEAD_DIM)
            o_ref[:, sl] = (acc_refs[h][...] / l_all[:, h:h + 1]).astype(o_ref.dtype)


def _attn_a_call(qk, v, mask, bsz, tp, tt):
    nb = tp // tt
    w = A_WIDTH
    q_tile, k_tile = _causal_tile_pairs(nb)
    grid_spec = pltpu.PrefetchScalarGridSpec(
        num_scalar_prefetch=2,
        grid=(bsz, q_tile.shape[0]),
        in_specs=[pl.BlockSpec((tt, w), lambda b, t, qt, kt: (b * nb + qt[t], 0)),
                  pl.BlockSpec((tt, w), lambda b, t, qt, kt: (b * nb + kt[t], 1)),
                  pl.BlockSpec((tt, w), lambda b, t, qt, kt: (b * nb + kt[t], 0)),
                  pl.BlockSpec((tt, tt), lambda b, t, qt, kt: (b * nb + qt[t], kt[t]))],
        out_specs=pl.BlockSpec((tt, w), lambda b, t, qt, kt: (b * nb + qt[t], 0)),
        scratch_shapes=[pltpu.VMEM((tt, LANES), F32), pltpu.VMEM((tt, LANES), F32),
                        pltpu.VMEM((A_HEADS, ROW_BLOCK, tt), F32),
                        pltpu.VMEM((A_HEADS, ROW_BLOCK, tt), BF16)]
        + [pltpu.VMEM((tt, HEAD_DIM), F32) for _ in range(A_HEADS)],
    )
    return pl.pallas_call(
        _attn_a_kernel,
        grid_spec=grid_spec,
        out_shape=jax.ShapeDtypeStruct((bsz * tp, w), BF16),
        compiler_params=_params(("parallel", "arbitrary")),
    )(q_tile, k_tile, qk, qk, v, mask)


def _attn_b_kernel(qt_ref, kt_ref, q_ref, k_ref, v_ref, lq1_ref, lk1_ref, lq2_ref, lk2_ref, g_ref, o_ref,
                   m_ref, l_ref, s_ref, p_ref, acc_ref, *, front, lambda_init):
    t = pl.program_id(1)
    i = qt_ref[t]
    j = kt_ref[t]
    tt = q_ref.shape[0]
    nt_dims = (((1,), (1,)), ((), ()))
    dv = 2 * HEAD_DIM
    n_set = 2 * B_HEADS

    @pl.when(j == 0)
    def _():
        m_ref[...] = jnp.full(m_ref.shape, NEG_BIG, F32)
        l_ref[...] = jnp.zeros(l_ref.shape, F32)
        acc_ref[...] = jnp.zeros(acc_ref.shape, F32)

    def group(rows, r0, kw, masked):
        lane = lax.broadcasted_iota(I32, (ROW_BLOCK, LANES), 1)
        if masked:
            q_idx = i * tt + r0 + lax.broadcasted_iota(I32, (ROW_BLOCK, kw), 0)
            k_idx = j * tt + lax.broadcasted_iota(I32, (ROW_BLOCK, kw), 1)
            bias = jnp.where((k_idx <= q_idx) & (k_idx >= front), 0.0, NEG_BIG)
        m_all = m_ref[rows, :]
        m_loc = m_all
        for c in range(n_set):
            sl = slice(c * HEAD_DIM, (c + 1) * HEAD_DIM)
            s = lax.dot_general(q_ref[rows, sl], k_ref[:kw, sl], nt_dims, preferred_element_type=F32)
            if masked:
                s = s + bias
            s_ref[c, :, :kw] = s
            m_loc = jnp.where(lane == c, jnp.max(s, axis=1, keepdims=True), m_loc)
        m_new = jnp.maximum(m_all, m_loc)
        alpha = jnp.exp2(m_all - m_new)
        l_out = alpha * l_ref[rows, :]
        for c in range(n_set):
            p = jnp.exp2(s_ref[c, :, :kw] - m_new[:, c:c + 1])
            p_ref[c, :, :kw] = p.astype(BF16)
            l_out = l_out + jnp.where(lane == c, jnp.sum(p, axis=1, keepdims=True), 0.0)
        for c in range(n_set):
            vh = v_ref[:kw, (c // 2) * dv:(c // 2 + 1) * dv]
            acc_ref[c, rows, :] = alpha[:, c:c + 1] * acc_ref[c, rows, :] + jnp.dot(
                p_ref[c, :, :kw], vh, preferred_element_type=F32)
        m_ref[rows, :] = m_new
        l_ref[rows, :] = l_out

    def full_tile(masked):
        def body(r, carry):
            r0 = pl.multiple_of(r * ROW_BLOCK, ROW_BLOCK)
            group(pl.ds(r0, ROW_BLOCK), r0, tt, masked)
            return carry

        lax.fori_loop(0, tt // ROW_BLOCK, body, 0)

    pl.when((j == 0) & (j < i))(functools.partial(full_tile, True))
    pl.when((j > 0) & (j < i))(functools.partial(full_tile, False))

    @pl.when(j == i)
    def _():
        for r in range(tt // ROW_BLOCK):
            group(slice(r * ROW_BLOCK, (r + 1) * ROW_BLOCK), r * ROW_BLOCK, (r + 1) * ROW_BLOCK, True)
        lam = (jnp.exp(jnp.sum(lq1_ref[...] * lk1_ref[...], axis=1, keepdims=True))
               - jnp.exp(jnp.sum(lq2_ref[...] * lk2_ref[...], axis=1, keepdims=True))
               + lambda_init)
        g = g_ref[...]
        l_all = l_ref[...]
        for h in range(B_HEADS):
            o = (acc_ref[2 * h] / l_all[:, 2 * h:2 * h + 1]
                 - lam * (acc_ref[2 * h + 1] / l_all[:, 2 * h + 1:2 * h + 2]))
            ms = jnp.mean(o * o, axis=-1, keepdims=True)
            y = o * lax.rsqrt(ms + RMS_EPS) * g
            o_ref[:, h * dv:(h + 1) * dv] = (y * (1.0 - lambda_init)).astype(o_ref.dtype)


def _attn_b_call(qk, v, lq1, lk1, lq2, lk2, subln_g, bsz, tp, tt, front, lambda_init):
    nb = tp // tt
    w = B_QK_WIDTH
    dv = 2 * HEAD_DIM
    vec = lambda a: a.reshape(1, -1).astype(F32)
    small = lambda n: pl.BlockSpec((1, n), lambda b, t, qt, kt: (0, 0))
    q_tile, k_tile = _causal_tile_pairs(nb)
    grid_spec = pltpu.PrefetchScalarGridSpec(
        num_scalar_prefetch=2,
        grid=(bsz, q_tile.shape[0]),
        in_specs=[pl.BlockSpec((tt, w), lambda b, t, qt, kt: (b * nb + qt[t], 2)),
                  pl.BlockSpec((tt, w), lambda b, t, qt, kt: (b * nb + kt[t], 3)),
                  pl.BlockSpec((tt, B_V_WIDTH), lambda b, t, qt, kt: (b * nb + kt[t], 1)),
                  small(HEAD_DIM), small(HEAD_DIM), small(HEAD_DIM), small(HEAD_DIM), small(dv)],
        out_specs=pl.BlockSpec((tt, B_V_WIDTH), lambda b, t, qt, kt: (b * nb + qt[t], 0)),
        scratch_shapes=[pltpu.VMEM((tt, LANES), F32),
                        pltpu.VMEM((tt, LANES), F32),
                        pltpu.VMEM((2 * B_HEADS, ROW_BLOCK, tt), F32),
                        pltpu.VMEM((2 * B_HEADS, ROW_BLOCK, tt), BF16),
                        pltpu.VMEM((2 * B_HEADS, tt, dv), F32)],
    )
    return pl.pallas_call(
        functools.partial(_attn_b_kernel, front=front, lambda_init=lambda_init),
        grid_spec=grid_spec,
        out_shape=jax.ShapeDtypeStruct((bsz * tp, B_V_WIDTH), BF16),
        compiler_params=_params(("parallel", "arbitrary")),
    )(q_tile, k_tile, qk, qk, v, vec(lq1), vec(lk1), vec(lq2), vec(lk2), vec(subln_g))


def _merge_router_kernel(oa_ref, ob_ref, ga_ref, gb_ref, h_ref, wua_ref, wub_ref, wo_ref, g_ref,
                         wr_ref, br_ref, h1_ref, hfp_ref, route_ref, gate_ref, cnt_ref, carry_ref,
                         *, tp, front):
    i = pl.program_id(0)
    tm = oa_ref.shape[0]
    d = h_ref.shape[1]

    @pl.when(i == 0)
    def _():
        carry_ref[...] = jnp.zeros(carry_ref.shape, F32)

    ua = jnp.dot(oa_ref[...], wua_ref[...], preferred_element_type=F32)
    ub = jnp.dot(ob_ref[...], wub_ref[...], preferred_element_type=F32)
    merged = ga_ref[...].astype(F32) * ua + gb_ref[...].astype(F32) * ub
    mix = jnp.dot(merged.astype(BF16), wo_ref[...], preferred_element_type=F32)
    h1 = h_ref[...] + mix
    h1_ref[...] = h1

    ms = jnp.mean(h1 * h1, axis=-1, keepdims=True)
    hf = (h1 * lax.rsqrt(ms + RMS_EPS) * g_ref[...]).astype(BF16)

    hf_bits = pltpu.bitcast(hf.astype(F32), jnp.uint32)
    hfp_ref[...] = (hf_bits[:, :d // 2] & jnp.uint32(0xFFFF0000)) | (hf_bits[:, d // 2:] >> 16)

    logits = jnp.dot(hf, wr_ref[...], preferred_element_type=F32) + br_ref[...]
    lane = lax.broadcasted_iota(I32, (tm, LANES), 1)
    row = (i * tm + lax.broadcasted_iota(I32, (tm, 1), 0)) % tp
    routed = row >= front

    work = logits
    vals, idxs, onehots = [], [], []
    for _ in range(TOP_K):
        v = jnp.max(work, axis=1, keepdims=True)
        e = jnp.min(jnp.where(work == v, lane, LANES), axis=1, keepdims=True)
        hit = lane == e
        vals.append(v)
        idxs.append(e)
        onehots.append(hit)
        work = jnp.where(hit, -jnp.inf, work)

    exps = [jnp.exp(v - vals[0]) for v in vals]
    denom = exps[0] + exps[1] + exps[2] + exps[3]

    member = jnp.zeros((tm, LANES), F32)
    for hit in onehots:
        member = member + jnp.where(hit & routed, 1.0, 0.0)
    r_i = lax.broadcasted_iota(I32, (tm, tm), 0)
    c_i = lax.broadcasted_iota(I32, (tm, tm), 1)
    lower = jnp.where(c_i < r_i, 1.0, 0.0).astype(BF16)
    before = jnp.dot(lower, member.astype(BF16), preferred_element_type=F32) + carry_ref[...]
    carry_ref[...] = carry_ref[...] + jnp.sum(member, axis=0, keepdims=True)

    route = jnp.zeros((tm, LANES), I32)
    gates = jnp.zeros((tm, LANES), F32)
    for k in range(TOP_K):
        rank = jnp.sum(jnp.where(onehots[k], before, 0.0), axis=1, keepdims=True)
        route = jnp.where(lane == k, idxs[k], route)
        route = jnp.where(lane == TOP_K + k, rank.astype(I32), route)
        gates = jnp.where(lane == k, exps[k] / denom, gates)
    route_ref[...] = route
    gate_ref[...] = gates
    cnt_ref[...] = jnp.broadcast_to(carry_ref[...], cnt_ref.shape)


def _merge_router_call(o_a, o_b, gates, h0, wua, wub, wo, g_ffn, wr, br, tp, front, tm):
    rows, d = h0.shape
    nt = rows // tm
    once = pl.Buffered(1)
    const = lambda shape: pl.BlockSpec(shape, lambda i: (0, 0), pipeline_mode=once)
    return pl.pallas_call(
        functools.partial(_merge_router_kernel, tp=tp, front=front),
        grid=(nt,),
        in_specs=[pl.BlockSpec((tm, A_WIDTH), lambda i: (i, 0)),
                  pl.BlockSpec((tm, B_V_WIDTH), lambda i: (i, 0)),
                  pl.BlockSpec((tm, d), lambda i: (i, 0)),
                  pl.BlockSpec((tm, d), lambda i: (i, 1)),
                  pl.BlockSpec((tm, d), lambda i: (i, 0)),
                  const((A_WIDTH, d)), const((B_V_WIDTH, d)), const((d, d)),
                  const((1, d)), const((d, LANES)), const((1, LANES))],
        out_specs=[pl.BlockSpec((tm, d), lambda i: (i, 0)),
                   pl.BlockSpec((tm, d // 2), lambda i: (i, 0)),
                   pl.BlockSpec((tm, LANES), lambda i: (i, 0)),
                   pl.BlockSpec((tm, LANES), lambda i: (i, 0)),
                   pl.BlockSpec((8, LANES), lambda i: (0, 0))],
        out_shape=[jax.ShapeDtypeStruct((rows, d), F32),
                   jax.ShapeDtypeStruct((rows, d // 2), jnp.uint32),
                   jax.ShapeDtypeStruct((rows, LANES), I32),
                   jax.ShapeDtypeStruct((rows, LANES), F32),
                   jax.ShapeDtypeStruct((8, LANES), F32)],
        scratch_shapes=[pltpu.VMEM((1, LANES), F32)],
        compiler_params=_params(("arbitrary",)),
    )(o_a, o_b, gates, gates, h0, wua, wub, wo, g_ffn, wr, br)


def _dispatch_kernel(dest_ref, hfp_ref, xb_in_ref, xb_ref, sem):
    del xb_in_ref
    tm = hfp_ref.shape[0]

    def row_copy(r, k):
        d = dest_ref[r * TOP_K + k]
        return pltpu.make_async_copy(hfp_ref.at[pl.ds(r, 1), :], xb_ref.at[pl.ds(d, 1), :], sem)

    def issue(r, carry):
        for k in range(TOP_K):
            row_copy(r, k).start()
        return carry

    lax.fori_loop(0, tm, issue, 0, unroll=8)

    def drain(r, carry):
        for k in range(TOP_K):
            row_copy(r, k).wait()
        return carry

    lax.fori_loop(0, tm, drain, 0, unroll=8)


def _dispatch_call(dest_flat, hfp, n_rows_total, tm):
    rows, half = hfp.shape
    xb0 = jnp.zeros((n_rows_total, half), jnp.uint32)
    return pl.pallas_call(
        _dispatch_kernel,
        grid=(rows // tm,),
        in_specs=[pl.BlockSpec((tm * TOP_K,), lambda i: (i,), memory_space=pltpu.SMEM),
                  pl.BlockSpec((tm, half), lambda i: (i, 0)),
                  pl.BlockSpec(memory_space=pl.ANY)],
        out_specs=pl.BlockSpec(memory_space=pl.ANY),
        out_shape=jax.ShapeDtypeStruct((n_rows_total, half), jnp.uint32),
        scratch_shapes=[pltpu.SemaphoreType.DMA(())],
        input_output_aliases={2: 0},
        compiler_params=_params(("arbitrary",)),
    )(dest_flat, hfp, xb0)


def _w1_prep_kernel(w_ref, o_ref):
    n = 2 * LANES
    s_i = lax.broadcasted_iota(I32, (n, n), 0)
    j_i = lax.broadcasted_iota(I32, (n, n), 1)
    src = jnp.where(j_i < LANES, 2 * j_i, 2 * (j_i - LANES) + 1)
    perm = jnp.where(s_i == src, 1.0, 0.0).astype(BF16)
    for c in range(w_ref.shape[2] // n):
        blk = w_ref[0, :, c * n:(c + 1) * n].astype(BF16)
        o_ref[0, :, c * n:(c + 1) * n] = jnp.dot(blk, perm, preferred_element_type=F32).astype(BF16)


def _w1_prep_call(w1l, tk):
    n_exp, d, two_ff = w1l.shape
    return pl.pallas_call(
        _w1_prep_kernel,
        grid=(n_exp, d // tk),
        in_specs=[pl.BlockSpec((1, tk, two_ff), lambda e, k: (e, k, 0))],
        out_specs=pl.BlockSpec((1, tk, two_ff), lambda e, k: (e, k, 0)),
        out_shape=jax.ShapeDtypeStruct((n_exp, d, two_ff), BF16),
        compiler_params=_params(("parallel", "parallel")),
    )(w1l)


def _expert_kernel(be_ref, nu_ref, xb_ref, w1_ref, b1_ref, w2_ref, b2_ref, y_ref):
    i = pl.program_id(0)
    d_ff = w2_ref.shape[1]

    @pl.when(i < nu_ref[0])
    def _():
        packed = xb_ref[...]
        hi = pltpu.bitcast(packed & jnp.uint32(0xFFFF0000), F32)
        lo = pltpu.bitcast(packed << 16, F32)
        x = jnp.concatenate([hi, lo], axis=1).astype(BF16)
        u = jnp.dot(x, w1_ref[0], preferred_element_type=F32) + b1_ref[0]
        n_grp = d_ff // LANES
        glu = jnp.concatenate([u[:, 2 * c * LANES:(2 * c + 1) * LANES] for c in range(n_grp)], axis=1)
        lin = jnp.concatenate([u[:, (2 * c + 1) * LANES:(2 * c + 2) * LANES] for c in range(n_grp)], axis=1)
        x_glu = jnp.minimum(glu, SWIGLU_LIMIT)
        x_lin = jnp.clip(lin, -SWIGLU_LIMIT, SWIGLU_LIMIT)
        act = x_glu * jax.nn.sigmoid(SWIGLU_ALPHA * x_glu) * (x_lin + 1.0)
        y_ref[...] = jnp.dot(act.astype(BF16), w2_ref[0], preferred_element_type=F32) + b2_ref[0]

    @pl.when(i >= nu_ref[0])
    def _():
        y_ref[...] = jnp.zeros(y_ref.shape, F32)


def _expert_call(block_e, n_used, xb, w1p, b1p, w2b, b2, n_blocks):
    n_exp, d, two_ff = w1p.shape
    d_ff = two_ff // 2
    half = xb.shape[1]

    def blk(i, be, nu):
        return jnp.minimum(i, nu[0] - 1)

    grid_spec = pltpu.PrefetchScalarGridSpec(
        num_scalar_prefetch=2,
        grid=(n_blocks,),
        in_specs=[pl.BlockSpec((ROW_BLOCK, half), lambda i, be, nu: (blk(i, be, nu), 0)),
                  pl.BlockSpec((1, d, two_ff), lambda i, be, nu: (be[blk(i, be, nu)], 0, 0)),
                  pl.BlockSpec((1, 1, two_ff), lambda i, be, nu: (be[blk(i, be, nu)], 0, 0)),
                  pl.BlockSpec((1, d_ff, d), lambda i, be, nu: (be[blk(i, be, nu)], 0, 0)),
                  pl.BlockSpec((1, 1, d), lambda i, be, nu: (be[blk(i, be, nu)], 0, 0))],
        out_specs=pl.BlockSpec((ROW_BLOCK, d), lambda i, be, nu: (i, 0)),
    )
    return pl.pallas_call(
        _expert_kernel,
        grid_spec=grid_spec,
        out_shape=jax.ShapeDtypeStruct((n_blocks * ROW_BLOCK, d), F32),
        compiler_params=_params(("arbitrary",), vmem=60 * 1024 * 1024),
    )(block_e, n_used, xb, w1p, b1p.reshape(n_exp, 1, two_ff), w2b, b2.reshape(n_exp, 1, d))


def _combine_kernel(dest_ref, h1_ref, gate_ref, g_ref, yb_ref, o_ref, ybuf_ref, sem):
    tm = h1_ref.shape[0]

    def row_copy(r, k):
        d = dest_ref[r * TOP_K + k]
        return pltpu.make_async_copy(yb_ref.at[pl.ds(d, 1), :], ybuf_ref.at[k, pl.ds(r, 1), :], sem)

    def issue(r, carry):
        for k in range(TOP_K):
            row_copy(r, k).start()
        return carry

    lax.fori_loop(0, tm, issue, 0, unroll=8)

    def drain(r, carry):
        for k in range(TOP_K):
            row_copy(r, k).wait()
        return carry

    lax.fori_loop(0, tm, drain, 0, unroll=8)

    gate = gate_ref[...]
    h2 = h1_ref[...]
    for k in range(TOP_K):
        h2 = h2 + gate[:, k:k + 1] * ybuf_ref[k]
    ms = jnp.mean(h2 * h2, axis=-1, keepdims=True)
    o_ref[...] = h2 * lax.rsqrt(ms + RMS_EPS) * g_ref[...]


def _combine_call(dest_flat, h1, gate, g_final, yb, bsz, s_len, tp):
    d = h1.shape[1]
    tm = ROW_BLOCK
    ns = s_len // tm
    nb = tp // tm
    skip = (tp - s_len) // tm

    def src(i):
        return (i // ns) * nb + skip + i % ns

    return pl.pallas_call(
        _combine_kernel,
        grid=(bsz * ns,),
        in_specs=[pl.BlockSpec((tm * TOP_K,), lambda i: (src(i),), memory_space=pltpu.SMEM),
                  pl.BlockSpec((tm, d), lambda i: (src(i), 0)),
                  pl.BlockSpec((tm, LANES), lambda i: (src(i), 0)),
                  pl.BlockSpec((1, d), lambda i: (0, 0)),
                  pl.BlockSpec(memory_space=pl.ANY)],
        out_specs=pl.BlockSpec((tm, d), lambda i: (i, 0)),
        out_shape=jax.ShapeDtypeStruct((bsz * s_len, d), F32),
        scratch_shapes=[pltpu.VMEM((TOP_K, tm, d), F32), pltpu.SemaphoreType.DMA(())],
        compiler_params=_params(("arbitrary",)),
    )(dest_flat, h1, gate, g_final.reshape(1, d), yb)


def _rope_tables(pos, dim, reps):
    inv = 1.0 / (ROPE_THETA ** (jnp.arange(0, dim, 2, dtype=F32) / dim))
    ang = pos[:, None] * inv[None, :]
    cos = jnp.concatenate([jnp.cos(ang), jnp.cos(ang)], axis=-1)
    sin = jnp.concatenate([-jnp.sin(ang), jnp.sin(ang)], axis=-1)
    return jnp.tile(cos, (1, reps)), jnp.tile(sin, (1, reps))


def kernel(x, meta_tokens, norm_mix_g, w_in, lambda_q1, lambda_k1, lambda_q2, lambda_k2, subln_g, w_up_a, w_up_b, w_out, norm_ffn_g, w_router, b_router, w1, b1, w2, b2, norm_final_g):
    bsz, s_len, d = x.shape
    n_meta = meta_tokens.shape[0]
    depth = w_in.shape[0]
    n_exp = w_router.shape[-1]
    d_ff = w2.shape[-2]
    t_len = s_len + n_meta
    tp = -(-t_len // ROW_BLOCK) * ROW_BLOCK
    front = tp - t_len
    topk = min(TOPK_MAX, t_len // 4)
    rows = bsz * tp
    assert s_len % ROW_BLOCK == 0 and n_exp <= LANES and d % (2 * LANES) == 0 and d_ff % LANES == 0

    tm_proj = _pick_tile(tp, (640, 512, 256, 128))
    tt = _pick_tile(tp, (640, 512, 256, 128))
    tm_merge = _pick_tile(rows, (256, 128))

    pos = jnp.arange(tp, dtype=F32) - float(front)
    cos_h, sin_h = _rope_tables(pos, HEAD_DIM, 1)
    cos_i, sin_i = _rope_tables(pos, IDX_DIM, LANES // IDX_DIM)

    meta = jnp.broadcast_to(meta_tokens[None].astype(x.dtype), (bsz, n_meta, d))
    h = jnp.concatenate([jnp.zeros((bsz, front, d), x.dtype), meta, x], axis=1).reshape(rows, d)

    o_qa, o_ka, o_va = 0, A_WIDTH, 2 * A_WIDTH
    o_qi = 3 * A_WIDTH
    o_ki = o_qi + IDX_HEADS * IDX_DIM
    o_wi = o_ki + IDX_DIM
    o_qb = o_wi + IDX_HEADS
    o_kb = o_qb + B_QK_WIDTH
    o_vb = o_kb + B_QK_WIDTH
    o_ga = o_vb + B_V_WIDTH
    o_gb = o_ga + d

    for l in range(depth):
        lambda_init = 0.8 - 0.6 * math.exp(-0.3 * l)
        wl = w_in[l]
        col = lambda o, n: wl[:, o:o + n]
        zeros_half = jnp.zeros((d, LANES - IDX_DIM), wl.dtype)
        w_rope = jnp.concatenate([col(o_qa, A_WIDTH), col(o_ka, A_WIDTH),
                                  col(o_qb, B_QK_WIDTH), col(o_kb, B_QK_WIDTH)], axis=1).astype(BF16)
        w_val = jnp.concatenate([col(o_va, A_WIDTH), col(o_vb, B_V_WIDTH)], axis=1).astype(BF16)
        w_gate = jnp.concatenate([col(o_ga, d), col(o_gb, d)], axis=1).astype(BF16)
        w_idx = jnp.concatenate([col(o_qi, IDX_HEADS * IDX_DIM), col(o_ki, IDX_DIM), zeros_half,
                                 zeros_half, col(o_ki, IDX_DIM)], axis=1).astype(BF16)
        w_wi = jnp.concatenate([col(o_wi, IDX_HEADS), jnp.zeros((d, LANES - IDX_HEADS), wl.dtype)],
                               axis=1).astype(BF16)

        hn = _rmsnorm_call(h, norm_mix_g[l], tm_proj)
        q_scale = HEAD_DIM ** -0.5 * math.log2(math.e)
        col_scale = jnp.concatenate([jnp.full((A_WIDTH,), q_scale, F32), jnp.ones((A_WIDTH,), F32),
                                     jnp.full((B_QK_WIDTH,), q_scale, F32), jnp.ones((B_QK_WIDTH,), F32)])
        qk = _proj_call(hn, w_rope, BF16, "rope", tm_proj, 1024, cos_h, sin_h, HEAD_DIM // 2, tp=tp,
                        col_scale=col_scale.reshape(1, -1))
        val = _proj_call(hn, w_val, BF16, "none", tm_proj, 1024)
        gates = _proj_call(hn, w_gate, BF16, "sigmoid", tm_proj, _pick_tile(2 * d, (1024, 512, 256)))
        idx_qk = _proj_call(hn, w_idx, BF16, "rope", tm_proj, (IDX_HEADS * IDX_DIM + 2 * LANES) // 2,
                            cos_i, sin_i, IDX_DIM // 2, tp=tp)
        wi = _proj_call(hn, w_wi, F32, "scale", tm_proj, LANES,
                        scale=IDX_HEADS ** -0.5 * IDX_DIM ** -0.5)

        n_qcols = IDX_HEADS * IDX_DIM
        nc = tp // LANES
        k_cat = jnp.stack([idx_qk[:, n_qcols:n_qcols + LANES].reshape(bsz, nc, LANES, LANES),
                           idx_qk[:, n_qcols + LANES:].reshape(bsz, nc, LANES, LANES)], axis=2)
        k_cat = jnp.pad(k_cat, ((0, 0), (0, SCORE_CHUNKS - 1), (0, 0), (0, 0), (0, 0))).reshape(
            bsz * 2 * (tp + (SCORE_CHUNKS - 1) * LANES), LANES)
        mask = _indexer_call(idx_qk, k_cat, wi, bsz, tp, front, topk)
        o_a = _attn_a_call(qk, val, mask, bsz, tp, tt)
        o_b = _attn_b_call(qk, val, lambda_q1[l], lambda_k1[l], lambda_q2[l], lambda_k2[l], subln_g[l],
                           bsz, tp, tt, front, lambda_init)

        wr = jnp.concatenate([w_router[l], jnp.zeros((d, LANES - n_exp), F32)], axis=1).astype(BF16)
        br = jnp.concatenate([b_router[l], jnp.full((LANES - n_exp,), NEG_BIG, F32)]).reshape(1, LANES)
        h1, hfp, route, gate, cnt = _merge_router_call(
            o_a, o_b, gates, h, w_up_a[l].astype(BF16), w_up_b[l].astype(BF16), w_out[l].astype(BF16),
            norm_ffn_g[l].reshape(1, d), wr, br, tp, front, tm_merge)

        n_assign = bsz * t_len * TOP_K
        n_blocks = (n_assign + n_exp * (ROW_BLOCK - 1) + ROW_BLOCK - 1) // ROW_BLOCK
        n_rows = n_blocks * ROW_BLOCK
        counts = cnt[0, :n_exp].astype(I32)
        padded = (counts + ROW_BLOCK - 1) // ROW_BLOCK * ROW_BLOCK
        pad_end = jnp.cumsum(padded)
        pad_start = pad_end - padded
        n_used = (pad_end[-1:] // ROW_BLOCK).astype(I32)
        blk_start = jnp.arange(n_blocks, dtype=I32) * ROW_BLOCK
        block_e = jnp.minimum(jnp.sum((pad_end[None, :] <= blk_start[:, None]).astype(I32), axis=1),
                              n_exp - 1).astype(I32)
        top_e = route[:, :TOP_K]
        rank = route[:, TOP_K:2 * TOP_K]
        row_id = jnp.arange(rows, dtype=I32)
        is_pad = (row_id % tp) < front
        n_trash = bsz * front * TOP_K
        trash = (n_rows + ((row_id // tp) * front + row_id % tp)[:, None] * TOP_K
                 + jnp.arange(TOP_K, dtype=I32)[None, :])
        dest = jnp.where(is_pad[:, None], trash, pad_start[top_e] + rank)
        dest_flat = dest.reshape(-1).astype(I32)
        n_rows_total = n_rows + -(-max(n_trash, 1) // 8) * 8

        xb = _dispatch_call(dest_flat, hfp, n_rows_total, tm_merge)

        w1p = _w1_prep_call(w1[l], _pick_tile(d, (256, 128)))
        b1p = b1[l].reshape(n_exp, d_ff // LANES, LANES, 2).transpose(0, 1, 3, 2).reshape(n_exp, 2 * d_ff)
        yb = _expert_call(block_e, n_used, xb, w1p, b1p, w2[l].astype(BF16), b2[l], n_blocks)

        if l + 1 < depth:
            raise NotImplementedError("only the single-layer configuration is implemented")
        out = _combine_call(dest_flat, h1, gate, norm_final_g, yb, bsz, s_len, tp)
    return out.reshape(bsz, s_len, d)
```

```python
import functools
import math

import jax
import jax.numpy as jnp
import numpy as np
from jax import lax
from jax.experimental import pallas as pl
from jax.experimental.pallas import tpu as pltpu

HEAD_DIM = 128
A_HEADS = 8
IDX_HEADS = 16
IDX_DIM = 64
TOPK_MAX = 256
B_HEADS = 4
TOP_K = 4
SWIGLU_LIMIT = 7.0
SWIGLU_ALPHA = 1.702
ROPE_THETA = 10000.0
RMS_EPS = 1e-5
ROW_BLOCK = 128
LANES = 128
NEG_BIG = -1e30
VMEM_LIMIT = 56 * 1024 * 1024
SCORE_CHUNKS = 4
EXP_UNIT = 1 << 23
SPEC_BUCKETS = 4
SPEC_MIN_CHUNKS = 6

A_WIDTH = A_HEADS * HEAD_DIM
B_QK_WIDTH = B_HEADS * 2 * HEAD_DIM
B_V_WIDTH = B_HEADS * 2 * HEAD_DIM

F32 = jnp.float32
BF16 = jnp.bfloat16
I32 = jnp.int32

KEY_NEG_INF = int(np.array(0xFF800000, np.uint32).view(np.int32)) ^ 0x7FFFFFFF
INT_MIN = -(2 ** 31)


def _pick_tile(n, candidates):
    for c in candidates:
        if n % c == 0:
            return c
    raise ValueError(f"no tile for {n}")


def _params(sem, vmem=VMEM_LIMIT):
    return pltpu.CompilerParams(dimension_semantics=sem, vmem_limit_bytes=vmem)


def _embed_norm_kernel(head_ref, x_ref, g_ref, h_ref, hn_ref, *, blocks_per_seq, head_blocks):
    r = pl.program_id(0) % blocks_per_seq
    x = jnp.where(r < head_blocks, head_ref[...], x_ref[...])
    h_ref[...] = x
    ms = jnp.mean(x * x, axis=-1, keepdims=True)
    hn_ref[...] = (x * lax.rsqrt(ms + RMS_EPS) * g_ref[...]).astype(hn_ref.dtype)


def _embed_norm_call(head, x2d, g, bsz, tp):
    d = x2d.shape[1]
    nb = tp // ROW_BLOCK
    hb = head.shape[0] // ROW_BLOCK
    xb = x2d.shape[0] // bsz // ROW_BLOCK
    return pl.pallas_call(
        functools.partial(_embed_norm_kernel, blocks_per_seq=nb, head_blocks=hb),
        grid=(bsz * nb,),
        in_specs=[pl.BlockSpec((ROW_BLOCK, d), lambda i: (jnp.minimum(i % nb, hb - 1), 0)),
                  pl.BlockSpec((ROW_BLOCK, d), lambda i: ((i // nb) * xb + jnp.maximum(i % nb - hb, 0), 0)),
                  pl.BlockSpec((1, d), lambda i: (0, 0))],
        out_specs=[pl.BlockSpec((ROW_BLOCK, d), lambda i: (i, 0)),
                   pl.BlockSpec((ROW_BLOCK, d), lambda i: (i, 0))],
        out_shape=[jax.ShapeDtypeStruct((bsz * tp, d), F32),
                   jax.ShapeDtypeStruct((bsz * tp, d), BF16)],
        compiler_params=_params(("parallel",)),
    )(head, x2d, g.reshape(1, d))


def _proj_kernel(*refs, mode, rot_half, scale):
    if mode == "rope":
        x_ref, w_ref, cos_ref, sin_ref, cs_ref, o_ref = refs
    else:
        x_ref, w_ref, o_ref = refs
    acc = jnp.dot(x_ref[...], w_ref[...], preferred_element_type=F32)
    if mode == "rope":
        acc = acc * cs_ref[...]
        cos = cos_ref[...]
        sin = sin_ref[...]
        tn = acc.shape[1]
        lane = lax.broadcasted_iota(I32, (acc.shape[0], LANES), 1)
        outs = []
        for c in range(tn // LANES):
            xh = acc[:, c * LANES:(c + 1) * LANES]
            if rot_half == LANES // 2:
                rot = pltpu.roll(xh, LANES // 2, 1)
            else:
                fwd = pltpu.roll(xh, LANES - rot_half, 1)
                bwd = pltpu.roll(xh, rot_half, 1)
                rot = jnp.where((lane % (2 * rot_half)) < rot_half, fwd, bwd)
            outs.append(xh * cos + rot * sin)
        acc = outs[0] if len(outs) == 1 else jnp.concatenate(outs, axis=1)
    elif mode == "sigmoid":
        acc = jax.nn.sigmoid(acc)
    elif mode == "scale":
        acc = acc * scale
    o_ref[...] = acc.astype(o_ref.dtype)


def _proj_call(xn, w, out_dtype, mode, tm, tn, cos=None, sin=None, rot_half=0, scale=1.0, tp=None,
               col_scale=None):
    rows, d = xn.shape
    n = w.shape[1]
    in_specs = [pl.BlockSpec((tm, d), lambda i, j: (i, 0)),
                pl.BlockSpec((d, tn), lambda i, j: (0, j))]
    args = [xn, w]
    if mode == "rope":
        nt = tp // tm
        if col_scale is None:
            col_scale = jnp.ones((1, n), F32)
        in_specs += [pl.BlockSpec((tm, LANES), lambda i, j: (i % nt, 0)),
                     pl.BlockSpec((tm, LANES), lambda i, j: (i % nt, 0)),
                     pl.BlockSpec((1, tn), lambda i, j: (0, j))]
        args += [cos, sin, col_scale]
    return pl.pallas_call(
        functools.partial(_proj_kernel, mode=mode, rot_half=rot_half, scale=scale),
        grid=(rows // tm, n // tn),
        in_specs=in_specs,
        out_specs=pl.BlockSpec((tm, tn), lambda i, j: (i, j)),
        out_shape=jax.ShapeDtypeStruct((rows, n), out_dtype),
        compiler_params=_params(("parallel", "arbitrary")),
    )(*args)


def _indexer_kernel(qi_ref, kc_ref, wi_ref, mask_ref, keys_ref, qs_ref, p_ref,
                    *, front, topk, n_chunks_total):
    i = pl.program_id(1)
    n_chunk = i + 1
    n_pair = (n_chunk + 1) // 2
    tq = qi_ref.shape[0]
    n_hp = IDX_HEADS // 2
    q_idx = i * tq + lax.broadcasted_iota(I32, (LANES, tq), 1)
    k_off = lax.broadcasted_iota(I32, (LANES, tq), 0)

    w_t = wi_ref[...].T
    for p in range(n_hp):
        qs_ref[p * tq:(p + 1) * tq, :] = qi_ref[:, p * LANES:(p + 1) * LANES]

    nt_dims = (((1,), (1,)), ((), ()))

    def score_body(c4, carry):
        for u in range(SCORE_CHUNKS):
            c = SCORE_CHUNKS * c4 + u
            off = pl.multiple_of(c * LANES, LANES)
            kc = kc_ref[pl.ds(pl.multiple_of(c * 2 * LANES, 2 * LANES), 2 * LANES), :]
            acc = jnp.zeros((LANES, tq), F32)
            for g in range(n_hp // 2):
                qg = qs_ref[2 * g * tq:(2 * g + 2) * tq, :]
                st = lax.dot_general(kc, qg, nt_dims, preferred_element_type=F32)
                for v in range(2):
                    h0 = 2 * (2 * g + v)
                    sp = st[:, v * tq:(v + 1) * tq]
                    acc = acc + w_t[h0:h0 + 1, :] * jnp.maximum(sp[:LANES], 0.0)
                    acc = acc + w_t[h0 + 1:h0 + 2, :] * jnp.maximum(sp[LANES:], 0.0)
            k_idx = off + k_off
            valid = (k_idx <= q_idx) & (k_idx >= front)
            acc = jnp.where(acc == 0.0, 0.0, acc)
            score = jnp.where(valid, acc, -jnp.inf)
            bits = pltpu.bitcast(score, I32)
            key = jnp.where(bits < 0, bits ^ 0x7FFFFFFF, bits)
            keys_ref[pl.ds(off, LANES), :] = key
            carry = jnp.maximum(carry, key)
        return carry

    key_max = lax.fori_loop(0, (n_chunk + SCORE_CHUNKS - 1) // SCORE_CHUNKS, score_body,
                            jnp.full((LANES, tq), INT_MIN, I32))

    def count(pred_fn):
        def body(c2, cnt):
            off = pl.multiple_of(c2 * 2 * LANES, 2 * LANES)
            k0 = keys_ref[pl.ds(off, LANES), :]
            k1 = keys_ref[pl.ds(off + LANES, LANES), :]
            cnt = cnt + jnp.where(pred_fn(k0, off + k_off), 1, 0)
            return cnt + jnp.where(pred_fn(k1, off + LANES + k_off), 1, 0)
        cnt = lax.fori_loop(0, n_pair, body, jnp.zeros((LANES, tq), I32))
        return jnp.sum(cnt, axis=0, keepdims=True)

    bits_per_check = 4

    def bit_cond(state):
        it, _, cnt_t = state
        return (it < 32) & (jnp.max(jnp.where(cnt_t != topk, 1, 0)) > 0)

    def bit_body(state):
        it, t, cnt_t = state
        for u in range(bits_per_check):
            trial = t + jnp.left_shift(jnp.int32(1), 31 - u - it)
            cnt = count(lambda k, _: k >= trial)
            take = cnt >= topk
            t = jnp.where(take, trial, t)
            cnt_t = jnp.where(take, cnt, cnt_t)
        return it + bits_per_check, t, cnt_t

    full_search = (jnp.int32(0), jnp.full((1, tq), INT_MIN, I32), jnp.full((1, tq), -1, I32))

    def bucket_start():
        base = jnp.max(key_max, axis=0, keepdims=True) & jnp.int32(-EXP_UNIT)
        found = jnp.zeros((1, tq), I32)
        t_b = jnp.full((1, tq), INT_MIN, I32)
        c_b = jnp.full((1, tq), -1, I32)
        for d in range(SPEC_BUCKETS):
            start = base - d * EXP_UNIT
            cnt = count(lambda k, _: k >= start)
            first = (cnt >= topk) & (found == 0)
            t_b = jnp.where(first, start, t_b)
            c_b = jnp.where(first, cnt, c_b)
            found = jnp.where(cnt >= topk, 1, found)
        usable = (found > 0) & (base < jnp.int32(0x7F800000))
        every = jnp.min(jnp.where(usable, 1, 0)) > 0
        return (jnp.where(every, jnp.int32(8), jnp.int32(0)), jnp.where(every, t_b, INT_MIN),
                jnp.where(every, c_b, -1))

    state0 = lax.cond(n_chunk >= SPEC_MIN_CHUNKS, bucket_start, lambda: full_search)
    _, thr, cnt_ge = lax.while_loop(bit_cond, bit_body, state0)

    finite = thr > KEY_NEG_INF
    tie_rows = finite & (cnt_ge > topk)
    p_ref[...] = jnp.full(p_ref.shape, (n_chunks_total + 1) * LANES, I32)

    @pl.when(jnp.max(jnp.where(tie_rows, 1, 0)) > 0)
    def _():
        cnt_gt = count(lambda k, _: k > thr)
        need = topk - cnt_gt

        def idx_body(it, p):
            trial = p + jnp.left_shift(jnp.int32(1), 14 - it)
            cnt = count(lambda k, idx: (k == thr) & (idx < trial))
            return jnp.where(cnt < need, trial, p)

        p = lax.fori_loop(0, 15, idx_body, jnp.zeros((1, tq), I32))
        p = jnp.where(tie_rows, p, (n_chunks_total + 1) * LANES)
        p_ref[...] = jnp.broadcast_to(p, p_ref.shape)

    p_lim = p_ref[0:1, :]
    eye = jnp.where(lax.broadcasted_iota(I32, (tq, tq), 0) == lax.broadcasted_iota(I32, (tq, tq), 1),
                    1.0, 0.0).astype(BF16)

    per_trip = 4

    def write_body(c4, carry):
        for u in range(per_trip):
            c = jnp.minimum(per_trip * c4 + u, n_chunk - 1)
            off = pl.multiple_of(c * LANES, LANES)
            k = keys_ref[pl.ds(off, LANES), :]
            k_idx = off + k_off
            sel = (k > thr) | ((k == thr) & (k_idx <= p_lim))
            valid = (k_idx <= q_idx) & (k_idx >= front)
            sel_t = jnp.where(sel & valid, 1.0, 0.0).astype(BF16)
            sel_qk = lax.dot_general(eye, sel_t, nt_dims, preferred_element_type=F32)
            mask_ref[:, pl.ds(off, LANES)] = sel_qk.astype(jnp.int8)
        return carry

    lax.fori_loop(0, (n_chunk + per_trip - 1) // per_trip, write_body, 0)

    def zero_body(c, carry):
        off = pl.multiple_of(c * LANES, LANES)
        mask_ref[:, pl.ds(off, LANES)] = jnp.zeros((tq, LANES), jnp.int8)
        return carry

    lax.fori_loop(n_chunk, n_chunks_total, zero_body, 0)


def _indexer_call(idx_qk, k_cat, wi, bsz, tp, front, topk):
    tq = ROW_BLOCK
    nq = tp // tq
    n_qcols = IDX_HEADS * IDX_DIM
    return pl.pallas_call(
        functools.partial(_indexer_kernel, front=front, topk=topk, n_chunks_total=nq),
        grid=(bsz, nq),
        in_specs=[pl.BlockSpec((tq, n_qcols), lambda b, i: (b * nq + i, 0)),
                  pl.BlockSpec((2 * (tp + (SCORE_CHUNKS - 1) * LANES), LANES), lambda b, i: (b, 0)),
                  pl.BlockSpec((tq, LANES), lambda b, i: (b * nq + i, 0))],
        out_specs=pl.BlockSpec((tq, tp), lambda b, i: (b * nq + i, 0)),
        out_shape=jax.ShapeDtypeStruct((bsz * tp, tp), jnp.int8),
        scratch_shapes=[pltpu.VMEM((tp + (SCORE_CHUNKS - 1) * LANES, tq), I32),
                        pltpu.VMEM((n_qcols // LANES * tq, LANES), BF16),
                        pltpu.VMEM((8, tq), I32)],
        compiler_params=_params(("parallel", "arbitrary")),
    )(idx_qk, k_cat, wi)


def _causal_tile_pairs(nb):
    pairs = [(i, j) for i in range(nb) for j in range(i + 1)]
    return (jnp.array([p[0] for p in pairs], I32), jnp.array([p[1] for p in pairs], I32))


def _attn_a_kernel(qt_ref, kt_ref, q_ref, k_ref, v_ref, mask_ref, o_ref, m_ref, l_ref, s_ref, p_ref,
                   *acc_refs):
    t = pl.program_id(1)
    i = qt_ref[t]
    j = kt_ref[t]
    tt = q_ref.shape[0]
    nt_dims = (((1,), (1,)), ((), ()))

    @pl.when(j == 0)
    def _():
        m_ref[...] = jnp.full(m_ref.shape, NEG_BIG, F32)
        l_ref[...] = jnp.zeros(l_ref.shape, F32)
        for acc_ref in acc_refs:
            acc_ref[...] = jnp.zeros(acc_ref.shape, F32)

    def group(rows, kw):
        lane = lax.broadcasted_iota(I32, (ROW_BLOCK, LANES), 1)
        bias = jnp.where(mask_ref[rows, :kw].astype(I32) != 0, 0.0, NEG_BIG)
        m_all = m_ref[rows, :]
        m_loc = m_all
        for h in range(A_HEADS):
            sl = slice(h * HEAD_DIM, (h + 1) * HEAD_DIM)
            s = lax.dot_general(q_ref[rows, sl], k_ref[:kw, sl], nt_dims, preferred_element_type=F32) + bias
            s_ref[h, :, :kw] = s
            m_loc = jnp.where(lane == h, jnp.max(s, axis=1, keepdims=True), m_loc)
        m_new = jnp.maximum(m_all, m_loc)
        alpha = jnp.exp2(m_all - m_new)
        l_out = alpha * l_ref[rows, :]
        for h in range(A_HEADS):
            p = jnp.exp2(s_ref[h, :, :kw] - m_new[:, h:h + 1])
            p_ref[h, :, :kw] = p.astype(BF16)
            l_out = l_out + jnp.where(lane == h, jnp.sum(p, axis=1, keepdims=True), 0.0)
        for h in range(A_HEADS):
            sl = slice(h * HEAD_DIM, (h + 1) * HEAD_DIM)
            acc_refs[h][rows, :] = alpha[:, h:h + 1] * acc_refs[h][rows, :] + jnp.dot(
                p_ref[h, :, :kw], v_ref[:kw, sl], preferred_element_type=F32)
        m_ref[rows, :] = m_new
        l_ref[rows, :] = l_out

    @pl.when(j < i)
    def _():
        def body(r, carry):
            group(pl.ds(pl.multiple_of(r * ROW_BLOCK, ROW_BLOCK), ROW_BLOCK), tt)
            return carry

        lax.fori_loop(0, tt // ROW_BLOCK, body, 0)

    @pl.when(j == i)
    def _():
        for r in range(tt // ROW_BLOCK):
            group(slice(r * ROW_BLOCK, (r + 1) * ROW_BLOCK), (r + 1) * ROW_BLOCK)
        l_all = l_ref[...]
        for h in range(A_HEADS):
            sl = slice(h * HEAD_DIM, (h + 1) * HEAD_DIM)
            o_ref[:, sl] = (acc_refs[h][...] / l_all[:, h:h + 1]).astype(o_ref.dtype)


def _attn_a_call(qk, v, mask, bsz, tp, tt):
    nb = tp // tt
    w = A_WIDTH
    q_tile, k_tile = _causal_tile_pairs(nb)
    grid_spec = pltpu.PrefetchScalarGridSpec(
        num_scalar_prefetch=2,
        grid=(bsz, q_tile.shape[0]),
        in_specs=[pl.BlockSpec((tt, w), lambda b, t, qt, kt: (b * nb + qt[t], 0)),
                  pl.BlockSpec((tt, w), lambda b, t, qt, kt: (b * nb + kt[t], 1)),
                  pl.BlockSpec((tt, w), lambda b, t, qt, kt: (b * nb + kt[t], 0)),
                  pl.BlockSpec((tt, tt), lambda b, t, qt, kt: (b * nb + qt[t], kt[t]))],
        out_specs=pl.BlockSpec((tt, w), lambda b, t, qt, kt: (b * nb + qt[t], 0)),
        scratch_shapes=[pltpu.VMEM((tt, LANES), F32), pltpu.VMEM((tt, LANES), F32),
                        pltpu.VMEM((A_HEADS, ROW_BLOCK, tt), F32),
                        pltpu.VMEM((A_HEADS, ROW_BLOCK, tt), BF16)]
        + [pltpu.VMEM((tt, HEAD_DIM), F32) for _ in range(A_HEADS)],
    )
    return pl.pallas_call(
        _attn_a_kernel,
        grid_spec=grid_spec,
        out_shape=jax.ShapeDtypeStruct((bsz * tp, w), BF16),
        compiler_params=_params(("parallel", "arbitrary")),
    )(q_tile, k_tile, qk, qk, v, mask)


def _attn_b_kernel(qt_ref, kt_ref, q_ref, k_ref, v_ref, lq1_ref, lk1_ref, lq2_ref, lk2_ref, g_ref, o_ref,
                   m_ref, l_ref, s_ref, p_ref, acc_ref, *, front, lambda_init):
    t = pl.program_id(1)
    i = qt_ref[t]
    j = kt_ref[t]
    tt = q_ref.shape[0]
    nt_dims = (((1,), (1,)), ((), ()))
    dv = 2 * HEAD_DIM
    n_set = 2 * B_HEADS

    @pl.when(j == 0)
    def _():
        m_ref[...] = jnp.full(m_ref.shape, NEG_BIG, F32)
        l_ref[...] = jnp.zeros(l_ref.shape, F32)
        acc_ref[...] = jnp.zeros(acc_ref.shape, F32)

    def group(rows, r0, kw, masked):
        lane = lax.broadcasted_iota(I32, (ROW_BLOCK, LANES), 1)
        if masked:
            q_idx = i * tt + r0 + lax.broadcasted_iota(I32, (ROW_BLOCK, kw), 0)
            k_idx = j * tt + lax.broadcasted_iota(I32, (ROW_BLOCK, kw), 1)
            bias = jnp.where((k_idx <= q_idx) & (k_idx >= front), 0.0, NEG_BIG)
        m_all = m_ref[rows, :]
        m_loc = m_all
        for c in range(n_set):
            sl = slice(c * HEAD_DIM, (c + 1) * HEAD_DIM)
            s = lax.dot_general(q_ref[rows, sl], k_ref[:kw, sl], nt_dims, preferred_element_type=F32)
            if masked:
                s = s + bias
            s_ref[c, :, :kw] = s
            m_loc = jnp.where(lane == c, jnp.max(s, axis=1, keepdims=True), m_loc)
        m_new = jnp.maximum(m_all, m_loc)
        alpha = jnp.exp2(m_all - m_new)
        l_out = alpha * l_ref[rows, :]
        for c in range(n_set):
            p = jnp.exp2(s_ref[c, :, :kw] - m_new[:, c:c + 1])
            p_ref[c, :, :kw] = p.astype(BF16)
            l_out = l_out + jnp.where(lane == c, jnp.sum(p, axis=1, keepdims=True), 0.0)
        for c in range(n_set):
            vh = v_ref[:kw, (c // 2) * dv:(c // 2 + 1) * dv]
            acc_ref[c, rows, :] = alpha[:, c:c + 1] * acc_ref[c, rows, :] + jnp.dot(
                p_ref[c, :, :kw], vh, preferred_element_type=F32)
        m_ref[rows, :] = m_new
        l_ref[rows, :] = l_out

    def full_tile(masked):
        def body(r, carry):
            r0 = pl.multiple_of(r * ROW_BLOCK, ROW_BLOCK)
            group(pl.ds(r0, ROW_BLOCK), r0, tt, masked)
            return carry

        lax.fori_loop(0, tt // ROW_BLOCK, body, 0)

    pl.when((j == 0) & (j < i))(functools.partial(full_tile, True))
    pl.when((j > 0) & (j < i))(functools.partial(full_tile, False))

    @pl.when(j == i)
    def _():
        for r in range(tt // ROW_BLOCK):
            group(slice(r * ROW_BLOCK, (r + 1) * ROW_BLOCK), r * ROW_BLOCK, (r + 1) * ROW_BLOCK, True)
        lam = (jnp.exp(jnp.sum(lq1_ref[...] * lk1_ref[...], axis=1, keepdims=True))
               - jnp.exp(jnp.sum(lq2_ref[...] * lk2_ref[...], axis=1, keepdims=True))
               + lambda_init)
        g = g_ref[...]
        l_all = l_ref[...]
        for h in range(B_HEADS):
            o = (acc_ref[2 * h] / l_all[:, 2 * h:2 * h + 1]
                 - lam * (acc_ref[2 * h + 1] / l_all[:, 2 * h + 1:2 * h + 2]))
            ms = jnp.mean(o * o, axis=-1, keepdims=True)
            y = o * lax.rsqrt(ms + RMS_EPS) * g
            o_ref[:, h * dv:(h + 1) * dv] = (y * (1.0 - lambda_init)).astype(o_ref.dtype)


def _attn_b_call(qk, v, lq1, lk1, lq2, lk2, subln_g, bsz, tp, tt, front, lambda_init):
    nb = tp // tt
    w = B_QK_WIDTH
    dv = 2 * HEAD_DIM
    vec = lambda a: a.reshape(1, -1).astype(F32)
    small = lambda n: pl.BlockSpec((1, n), lambda b, t, qt, kt: (0, 0))
    q_tile, k_tile = _causal_tile_pairs(nb)
    grid_spec = pltpu.PrefetchScalarGridSpec(
        num_scalar_prefetch=2,
        grid=(bsz, q_tile.shape[0]),
        in_specs=[pl.BlockSpec((tt, w), lambda b, t, qt, kt: (b * nb + qt[t], 2)),
                  pl.BlockSpec((tt, w), lambda b, t, qt, kt: (b * nb + kt[t], 3)),
                  pl.BlockSpec((tt, B_V_WIDTH), lambda b, t, qt, kt: (b * nb + kt[t], 1)),
                  small(HEAD_DIM), small(HEAD_DIM), small(HEAD_DIM), small(HEAD_DIM), small(dv)],
        out_specs=pl.BlockSpec((tt, B_V_WIDTH), lambda b, t, qt, kt: (b * nb + qt[t], 0)),
        scratch_shapes=[pltpu.VMEM((tt, LANES), F32),
                        pltpu.VMEM((tt, LANES), F32),
                        pltpu.VMEM((2 * B_HEADS, ROW_BLOCK, tt), F32),
                        pltpu.VMEM((2 * B_HEADS, ROW_BLOCK, tt), BF16),
                        pltpu.VMEM((2 * B_HEADS, tt, dv), F32)],
    )
    return pl.pallas_call(
        functools.partial(_attn_b_kernel, front=front, lambda_init=lambda_init),
        grid_spec=grid_spec,
        out_shape=jax.ShapeDtypeStruct((bsz * tp, B_V_WIDTH), BF16),
        compiler_params=_params(("parallel", "arbitrary")),
    )(q_tile, k_tile, qk, qk, v, vec(lq1), vec(lk1), vec(lq2), vec(lk2), vec(subln_g))


def _merge_router_kernel(oa_ref, ob_ref, ga_ref, gb_ref, h_ref, wua_ref, wub_ref, wo_ref, g_ref,
                         wr_ref, br_ref, h1_ref, hfp_ref, route_ref, gate_ref, cnt_ref, carry_ref,
                         *, tp, front):
    i = pl.program_id(0)
    tm = oa_ref.shape[0]
    d = h_ref.shape[1]

    @pl.when(i == 0)
    def _():
        carry_ref[...] = jnp.zeros(carry_ref.shape, F32)

    ua = jnp.dot(oa_ref[...], wua_ref[...], preferred_element_type=F32)
    ub = jnp.dot(ob_ref[...], wub_ref[...], preferred_element_type=F32)
    merged = ga_ref[...].astype(F32) * ua + gb_ref[...].astype(F32) * ub
    mix = jnp.dot(merged.astype(BF16), wo_ref[...], preferred_element_type=F32)
    h1 = h_ref[...] + mix
    h1_ref[...] = h1

    ms = jnp.mean(h1 * h1, axis=-1, keepdims=True)
    hf = (h1 * lax.rsqrt(ms + RMS_EPS) * g_ref[...]).astype(BF16)

    hf_bits = pltpu.bitcast(hf.astype(F32), jnp.uint32)
    hfp_ref[...] = (hf_bits[:, :d // 2] & jnp.uint32(0xFFFF0000)) | (hf_bits[:, d // 2:] >> 16)

    logits = jnp.dot(hf, wr_ref[...], preferred_element_type=F32) + br_ref[...]
    lane = lax.broadcasted_iota(I32, (tm, LANES), 1)
    row = (i * tm + lax.broadcasted_iota(I32, (tm, 1), 0)) % tp
    routed = row >= front

    work = logits
    vals, idxs, onehots = [], [], []
    for _ in range(TOP_K):
        v = jnp.max(work, axis=1, keepdims=True)
        e = jnp.min(jnp.where(work == v, lane, LANES), axis=1, keepdims=True)
        hit = lane == e
        vals.append(v)
        idxs.append(e)
        onehots.append(hit)
        work = jnp.where(hit, -jnp.inf, work)

    exps = [jnp.exp(v - vals[0]) for v in vals]
    denom = exps[0] + exps[1] + exps[2] + exps[3]

    member = jnp.zeros((tm, LANES), F32)
    for hit in onehots:
        member = member + jnp.where(hit & routed, 1.0, 0.0)
    r_i = lax.broadcasted_iota(I32, (tm, tm), 0)
    c_i = lax.broadcasted_iota(I32, (tm, tm), 1)
    lower = jnp.where(c_i < r_i, 1.0, 0.0).astype(BF16)
    before = jnp.dot(lower, member.astype(BF16), preferred_element_type=F32) + carry_ref[...]
    carry_ref[...] = carry_ref[...] + jnp.sum(member, axis=0, keepdims=True)

    route = jnp.zeros((tm, LANES), I32)
    gates = jnp.zeros((tm, LANES), F32)
    for k in range(TOP_K):
        rank = jnp.sum(jnp.where(onehots[k], before, 0.0), axis=1, keepdims=True)
        route = jnp.where(lane == k, idxs[k], route)
        route = jnp.where(lane == TOP_K + k, rank.astype(I32), route)
        gates = jnp.where(lane == k, exps[k] / denom, gates)
    route_ref[...] = route
    gate_ref[...] = gates
    cnt_ref[...] = jnp.broadcast_to(carry_ref[...], cnt_ref.shape)


def _merge_router_call(o_a, o_b, gates, h0, wua, wub, wo, g_ffn, wr, br, tp, front, tm):
    rows, d = h0.shape
    nt = rows // tm
    once = pl.Buffered(1)
    const = lambda shape: pl.BlockSpec(shape, lambda i: (0, 0), pipeline_mode=once)
    return pl.pallas_call(
        functools.partial(_merge_router_kernel, tp=tp, front=front),
        grid=(nt,),
        in_specs=[pl.BlockSpec((tm, A_WIDTH), lambda i: (i, 0)),
                  pl.BlockSpec((tm, B_V_WIDTH), lambda i: (i, 0)),
                  pl.BlockSpec((tm, d), lambda i: (i, 0)),
                  pl.BlockSpec((tm, d), lambda i: (i, 1)),
                  pl.BlockSpec((tm, d), lambda i: (i, 0)),
                  const((A_WIDTH, d)), const((B_V_WIDTH, d)), const((d, d)),
                  const((1, d)), const((d, LANES)), const((1, LANES))],
        out_specs=[pl.BlockSpec((tm, d), lambda i: (i, 0)),
                   pl.BlockSpec((tm, d // 2), lambda i: (i, 0)),
                   pl.BlockSpec((tm, LANES), lambda i: (i, 0)),
                   pl.BlockSpec((tm, LANES), lambda i: (i, 0)),
                   pl.BlockSpec((8, LANES), lambda i: (0, 0))],
        out_shape=[jax.ShapeDtypeStruct((rows, d), F32),
                   jax.ShapeDtypeStruct((rows, d // 2), jnp.uint32),
                   jax.ShapeDtypeStruct((rows, LANES), I32),
                   jax.ShapeDtypeStruct((rows, LANES), F32),
                   jax.ShapeDtypeStruct((8, LANES), F32)],
        scratch_shapes=[pltpu.VMEM((1, LANES), F32)],
        compiler_params=_params(("arbitrary",)),
    )(o_a, o_b, gates, gates, h0, wua, wub, wo, g_ffn, wr, br)


def _dispatch_kernel(dest_ref, hfp_ref, xb_in_ref, xb_ref, sem):
    del xb_in_ref
    tm = hfp_ref.shape[0]

    def row_copy(r, k):
        d = dest_ref[r * TOP_K + k]
        return pltpu.make_async_copy(hfp_ref.at[pl.ds(r, 1), :], xb_ref.at[pl.ds(d, 1), :], sem)

    def issue(r, carry):
        for k in range(TOP_K):
            row_copy(r, k).start()
        return carry

    lax.fori_loop(0, tm, issue, 0, unroll=8)

    def drain(r, carry):
        for k in range(TOP_K):
            row_copy(r, k).wait()
        return carry

    lax.fori_loop(0, tm, drain, 0, unroll=8)


def _dispatch_call(dest_flat, hfp, n_rows_total, tm):
    rows, half = hfp.shape
    xb0 = jnp.zeros((n_rows_total, half), jnp.uint32)
    return pl.pallas_call(
        _dispatch_kernel,
        grid=(rows // tm,),
        in_specs=[pl.BlockSpec((tm * TOP_K,), lambda i: (i,), memory_space=pltpu.SMEM),
                  pl.BlockSpec((tm, half), lambda i: (i, 0)),
                  pl.BlockSpec(memory_space=pl.ANY)],
        out_specs=pl.BlockSpec(memory_space=pl.ANY),
        out_shape=jax.ShapeDtypeStruct((n_rows_total, half), jnp.uint32),
        scratch_shapes=[pltpu.SemaphoreType.DMA(())],
        input_output_aliases={2: 0},
        compiler_params=_params(("arbitrary",)),
    )(dest_flat, hfp, xb0)


def _w1_prep_kernel(w_ref, o_ref):
    n = 2 * LANES
    s_i = lax.broadcasted_iota(I32, (n, n), 0)
    j_i = lax.broadcasted_iota(I32, (n, n), 1)
    src = jnp.where(j_i < LANES, 2 * j_i, 2 * (j_i - LANES) + 1)
    perm = jnp.where(s_i == src, 1.0, 0.0).astype(BF16)
    for c in range(w_ref.shape[2] // n):
        blk = w_ref[0, :, c * n:(c + 1) * n].astype(BF16)
        o_ref[0, :, c * n:(c + 1) * n] = jnp.dot(blk, perm, preferred_element_type=F32).astype(BF16)


def _w1_prep_call(w1l, tk):
    n_exp, d, two_ff = w1l.shape
    return pl.pallas_call(
        _w1_prep_kernel,
        grid=(n_exp, d // tk),
        in_specs=[pl.BlockSpec((1, tk, two_ff), lambda e, k: (e, k, 0))],
        out_specs=pl.BlockSpec((1, tk, two_ff), lambda e, k: (e, k, 0)),
        out_shape=jax.ShapeDtypeStruct((n_exp, d, two_ff), BF16),
        compiler_params=_params(("parallel", "parallel")),
    )(w1l)


def _expert_kernel(be_ref, nu_ref, xb_ref, w1_ref, b1_ref, w2_ref, b2_ref, y_ref):
    i = pl.program_id(0)
    d_ff = w2_ref.shape[1]

    @pl.when(i < nu_ref[0])
    def _():
        packed = xb_ref[...]
        hi = pltpu.bitcast(packed & jnp.uint32(0xFFFF0000), F32)
        lo = pltpu.bitcast(packed << 16, F32)
        x = jnp.concatenate([hi, lo], axis=1).astype(BF16)
        u = jnp.dot(x, w1_ref[0], preferred_element_type=F32) + b1_ref[0]
        n_grp = d_ff // LANES
        glu = jnp.concatenate([u[:, 2 * c * LANES:(2 * c + 1) * LANES] for c in range(n_grp)], axis=1)
        lin = jnp.concatenate([u[:, (2 * c + 1) * LANES:(2 * c + 2) * LANES] for c in range(n_grp)], axis=1)
        x_glu = jnp.minimum(glu, SWIGLU_LIMIT)
        x_lin = jnp.clip(lin, -SWIGLU_LIMIT, SWIGLU_LIMIT)
        act = x_glu * jax.nn.sigmoid(SWIGLU_ALPHA * x_glu) * (x_lin + 1.0)
        y_ref[...] = jnp.dot(act.astype(BF16), w2_ref[0], preferred_element_type=F32) + b2_ref[0]

    @pl.when(i >= nu_ref[0])
    def _():
        y_ref[...] = jnp.zeros(y_ref.shape, F32)


def _expert_call(block_e, n_used, xb, w1p, b1p, w2b, b2, n_blocks):
    n_exp, d, two_ff = w1p.shape
    d_ff = two_ff // 2
    half = xb.shape[1]

    def blk(i, be, nu):
        return jnp.minimum(i, nu[0] - 1)

    grid_spec = pltpu.PrefetchScalarGridSpec(
        num_scalar_prefetch=2,
        grid=(n_blocks,),
        in_specs=[pl.BlockSpec((ROW_BLOCK, half), lambda i, be, nu: (blk(i, be, nu), 0)),
                  pl.BlockSpec((1, d, two_ff), lambda i, be, nu: (be[blk(i, be, nu)], 0, 0)),
                  pl.BlockSpec((1, 1, two_ff), lambda i, be, nu: (be[blk(i, be, nu)], 0, 0)),
                  pl.BlockSpec((1, d_ff, d), lambda i, be, nu: (be[blk(i, be, nu)], 0, 0)),
                  pl.BlockSpec((1, 1, d), lambda i, be, nu: (be[blk(i, be, nu)], 0, 0))],
        out_specs=pl.BlockSpec((ROW_BLOCK, d), lambda i, be, nu: (i, 0)),
    )
    return pl.pallas_call(
        _expert_kernel,
        grid_spec=grid_spec,
        out_shape=jax.ShapeDtypeStruct((n_blocks * ROW_BLOCK, d), F32),
        compiler_params=_params(("arbitrary",), vmem=60 * 1024 * 1024),
    )(block_e, n_used, xb, w1p, b1p.reshape(n_exp, 1, two_ff), w2b, b2.reshape(n_exp, 1, d))


def _combine_kernel(dest_ref, h1_ref, gate_ref, g_ref, yb_ref, o_ref, ybuf_ref, sem):
    tm = h1_ref.shape[0]

    def row_copy(r, k):
        d = dest_ref[r * TOP_K + k]
        return pltpu.make_async_copy(yb_ref.at[pl.ds(d, 1), :], ybuf_ref.at[k, pl.ds(r, 1), :], sem)

    def issue(r, carry):
        for k in range(TOP_K):
            row_copy(r, k).start()
        return carry

    lax.fori_loop(0, tm, issue, 0, unroll=8)

    def drain(r, carry):
        for k in range(TOP_K):
            row_copy(r, k).wait()
        return carry

    lax.fori_loop(0, tm, drain, 0, unroll=8)

    gate = gate_ref[...]
    h2 = h1_ref[...]
    for k in range(TOP_K):
        h2 = h2 + gate[:, k:k + 1] * ybuf_ref[k]
    ms = jnp.mean(h2 * h2, axis=-1, keepdims=True)
    o_ref[...] = h2 * lax.rsqrt(ms + RMS_EPS) * g_ref[...]


def _combine_call(dest_flat, h1, gate, g_final, yb, bsz, s_len, tp):
    d = h1.shape[1]
    tm = ROW_BLOCK
    ns = s_len // tm
    nb = tp // tm
    skip = (tp - s_len) // tm

    def src(i):
        return (i // ns) * nb + skip + i % ns

    return pl.pallas_call(
        _combine_kernel,
        grid=(bsz * ns,),
        in_specs=[pl.BlockSpec((tm * TOP_K,), lambda i: (src(i),), memory_space=pltpu.SMEM),
                  pl.BlockSpec((tm, d), lambda i: (src(i), 0)),
                  pl.BlockSpec((tm, LANES), lambda i: (src(i), 0)),
                  pl.BlockSpec((1, d), lambda i: (0, 0)),
                  pl.BlockSpec(memory_space=pl.ANY)],
        out_specs=pl.BlockSpec((tm, d), lambda i: (i, 0)),
        out_shape=jax.ShapeDtypeStruct((bsz * s_len, d), F32),
        scratch_shapes=[pltpu.VMEM((TOP_K, tm, d), F32), pltpu.SemaphoreType.DMA(())],
        compiler_params=_params(("arbitrary",)),
    )(dest_flat, h1, gate, g_final.reshape(1, d), yb)


def _rope_tables(pos, dim, reps):
    inv = 1.0 / (ROPE_THETA ** (jnp.arange(0, dim, 2, dtype=F32) / dim))
    ang = pos[:, None] * inv[None, :]
    cos = jnp.concatenate([jnp.cos(ang), jnp.cos(ang)], axis=-1)
    sin = jnp.concatenate([-jnp.sin(ang), jnp.sin(ang)], axis=-1)
    return jnp.tile(cos, (1, reps)), jnp.tile(sin, (1, reps))


def kernel(x, meta_tokens, norm_mix_g, w_in, lambda_q1, lambda_k1, lambda_q2, lambda_k2, subln_g, w_up_a, w_up_b, w_out, norm_ffn_g, w_router, b_router, w1, b1, w2, b2, norm_final_g):
    bsz, s_len, d = x.shape
    n_meta = meta_tokens.shape[0]
    depth = w_in.shape[0]
    n_exp = w_router.shape[-1]
    d_ff = w2.shape[-2]
    t_len = s_len + n_meta
    tp = -(-t_len // ROW_BLOCK) * ROW_BLOCK
    front = tp - t_len
    topk = min(TOPK_MAX, t_len // 4)
    rows = bsz * tp
    assert s_len % ROW_BLOCK == 0 and n_exp <= LANES and d % (2 * LANES) == 0 and d_ff % LANES == 0

    tm_proj = _pick_tile(tp, (640, 512, 256, 128))
    tt = _pick_tile(tp, (640, 512, 256, 128))
    tm_merge = _pick_tile(rows, (256, 128))

    pos = jnp.arange(tp, dtype=F32) - float(front)
    cos_h, sin_h = _rope_tables(pos, HEAD_DIM, 1)
    cos_i, sin_i = _rope_tables(pos, IDX_DIM, LANES // IDX_DIM)

    head = jnp.concatenate([jnp.zeros((front, d), x.dtype), meta_tokens.astype(x.dtype)], axis=0)

    o_qa, o_ka, o_va = 0, A_WIDTH, 2 * A_WIDTH
    o_qi = 3 * A_WIDTH
    o_ki = o_qi + IDX_HEADS * IDX_DIM
    o_wi = o_ki + IDX_DIM
    o_qb = o_wi + IDX_HEADS
    o_kb = o_qb + B_QK_WIDTH
    o_vb = o_kb + B_QK_WIDTH
    o_ga = o_vb + B_V_WIDTH
    o_gb = o_ga + d

    for l in range(depth):
        lambda_init = 0.8 - 0.6 * math.exp(-0.3 * l)
        wl = w_in[l]
        col = lambda o, n: wl[:, o:o + n]
        zeros_half = jnp.zeros((d, LANES - IDX_DIM), wl.dtype)
        w_rope = jnp.concatenate([col(o_qa, A_WIDTH), col(o_ka, A_WIDTH),
                                  col(o_qb, B_QK_WIDTH), col(o_kb, B_QK_WIDTH)], axis=1).astype(BF16)
        w_val = jnp.concatenate([col(o_va, A_WIDTH), col(o_vb, B_V_WIDTH)], axis=1).astype(BF16)
        w_gate = jnp.concatenate([col(o_ga, d), col(o_gb, d)], axis=1).astype(BF16)
        w_idx = jnp.concatenate([col(o_qi, IDX_HEADS * IDX_DIM), col(o_ki, IDX_DIM), zeros_half,
                                 zeros_half, col(o_ki, IDX_DIM)], axis=1).astype(BF16)
        w_wi = jnp.concatenate([col(o_wi, IDX_HEADS), jnp.zeros((d, LANES - IDX_HEADS), wl.dtype)],
                               axis=1).astype(BF16)

        h, hn = _embed_norm_call(head, x.reshape(bsz * s_len, d), norm_mix_g[l], bsz, tp)
        q_scale = HEAD_DIM ** -0.5 * math.log2(math.e)
        col_scale = jnp.concatenate([jnp.full((A_WIDTH,), q_scale, F32), jnp.ones((A_WIDTH,), F32),
                                     jnp.full((B_QK_WIDTH,), q_scale, F32), jnp.ones((B_QK_WIDTH,), F32)])
        qk = _proj_call(hn, w_rope, BF16, "rope", tm_proj, 1024, cos_h, sin_h, HEAD_DIM // 2, tp=tp,
                        col_scale=col_scale.reshape(1, -1))
        val = _proj_call(hn, w_val, BF16, "none", tm_proj, 1024)
        gates = _proj_call(hn, w_gate, BF16, "sigmoid", tm_proj, _pick_tile(2 * d, (1024, 512, 256)))
        idx_qk = _proj_call(hn, w_idx, BF16, "rope", tm_proj, (IDX_HEADS * IDX_DIM + 2 * LANES) // 2,
                            cos_i, sin_i, IDX_DIM // 2, tp=tp)
        wi = _proj_call(hn, w_wi, F32, "scale", tm_proj, LANES,
                        scale=IDX_HEADS ** -0.5 * IDX_DIM ** -0.5)

        n_qcols = IDX_HEADS * IDX_DIM
        nc = tp // LANES
        k_cat = jnp.stack([idx_qk[:, n_qcols:n_qcols + LANES].reshape(bsz, nc, LANES, LANES),
                           idx_qk[:, n_qcols + LANES:].reshape(bsz, nc, LANES, LANES)], axis=2)
        k_cat = jnp.pad(k_cat, ((0, 0), (0, SCORE_CHUNKS - 1), (0, 0), (0, 0), (0, 0))).reshape(
            bsz * 2 * (tp + (SCORE_CHUNKS - 1) * LANES), LANES)
        mask = _indexer_call(idx_qk, k_cat, wi, bsz, tp, front, topk)
        o_a = _attn_a_call(qk, val, mask, bsz, tp, tt)
        o_b = _attn_b_call(qk, val, lambda_q1[l], lambda_k1[l], lambda_q2[l], lambda_k2[l], subln_g[l],
                           bsz, tp, tt, front, lambda_init)

        wr = jnp.concatenate([w_router[l], jnp.zeros((d, LANES - n_exp), F32)], axis=1).astype(BF16)
        br = jnp.concatenate([b_router[l], jnp.full((LANES - n_exp,), NEG_BIG, F32)]).reshape(1, LANES)
        h1, hfp, route, gate, cnt = _merge_router_call(
            o_a, o_b, gates, h, w_up_a[l].astype(BF16), w_up_b[l].astype(BF16), w_out[l].astype(BF16),
            norm_ffn_g[l].reshape(1, d), wr, br, tp, front, tm_merge)

        n_assign = bsz * t_len * TOP_K
        n_blocks = (n_assign + n_exp * (ROW_BLOCK - 1) + ROW_BLOCK - 1) // ROW_BLOCK
        n_rows = n_blocks * ROW_BLOCK
        counts = cnt[0, :n_exp].astype(I32)
        padded = (counts + ROW_BLOCK - 1) // ROW_BLOCK * ROW_BLOCK
        pad_end = jnp.cumsum(padded)
        pad_start = pad_end - padded
        n_used = (pad_end[-1:] // ROW_BLOCK).astype(I32)
        blk_start = jnp.arange(n_blocks, dtype=I32) * ROW_BLOCK
        block_e = jnp.minimum(jnp.sum((pad_end[None, :] <= blk_start[:, None]).astype(I32), axis=1),
                              n_exp - 1).astype(I32)
        top_e = route[:, :TOP_K]
        rank = route[:, TOP_K:2 * TOP_K]
        row_id = jnp.arange(rows, dtype=I32)
        is_pad = (row_id % tp) < front
        n_trash = bsz * front * TOP_K
        trash = (n_rows + ((row_id // tp) * front + row_id % tp)[:, None] * TOP_K
                 + jnp.arange(TOP_K, dtype=I32)[None, :])
        dest = jnp.where(is_pad[:, None], trash, pad_start[top_e] + rank)
        dest_flat = dest.reshape(-1).astype(I32)
        n_rows_total = n_rows + -(-max(n_trash, 1) // 8) * 8

        xb = _dispatch_call(dest_flat, hfp, n_rows_total, tm_merge)

        w1p = _w1_prep_call(w1[l], _pick_tile(d, (256, 128)))
        b1p = b1[l].reshape(n_exp, d_ff // LANES, LANES, 2).transpose(0, 1, 3, 2).reshape(n_exp, 2 * d_ff)
        yb = _expert_call(block_e, n_used, xb, w1p, b1p, w2[l].astype(BF16), b2[l], n_blocks)

        if l + 1 < depth:
            raise NotImplementedError("only the single-layer configuration is implemented")
        out = _combine_call(dest_flat, h1, gate, norm_final_g, yb, bsz, s_len, tp)
    return out.reshape(bsz, s_len, d)
```

```python
import functools
import math

import jax
import jax.numpy as jnp
import numpy as np
from jax import lax
from jax.experimental import pallas as pl
from jax.experimental.pallas import tpu as pltpu

HEAD_DIM = 128
A_HEADS = 8
IDX_HEADS = 16
IDX_DIM = 64
TOPK_MAX = 256
B_HEADS = 4
TOP_K = 4
SWIGLU_LIMIT = 7.0
SWIGLU_ALPHA = 1.702
ROPE_THETA = 10000.0
RMS_EPS = 1e-5
ROW_BLOCK = 128
LANES = 128
NEG_BIG = -1e30
VMEM_LIMIT = 56 * 1024 * 1024
SCORE_CHUNKS = 4
EXP_UNIT = 1 << 23
SPEC_BUCKETS = 4
SPEC_MIN_CHUNKS = 6

A_WIDTH = A_HEADS * HEAD_DIM
B_QK_WIDTH = B_HEADS * 2 * HEAD_DIM
B_V_WIDTH = B_HEADS * 2 * HEAD_DIM

F32 = jnp.float32
BF16 = jnp.bfloat16
I32 = jnp.int32

KEY_NEG_INF = int(np.array(0xFF800000, np.uint32).view(np.int32)) ^ 0x7FFFFFFF
INT_MIN = -(2 ** 31)


def _pick_tile(n, candidates):
    for c in candidates:
        if n % c == 0:
            return c
    raise ValueError(f"no tile for {n}")


def _params(sem, vmem=VMEM_LIMIT):
    return pltpu.CompilerParams(dimension_semantics=sem, vmem_limit_bytes=vmem)


def _embed_norm_kernel(head_ref, x_ref, g_ref, h_ref, hn_ref, *, blocks_per_seq, head_blocks):
    r = pl.program_id(0) % blocks_per_seq
    x = jnp.where(r < head_blocks, head_ref[...], x_ref[...])
    h_ref[...] = x
    ms = jnp.mean(x * x, axis=-1, keepdims=True)
    hn_ref[...] = (x * lax.rsqrt(ms + RMS_EPS) * g_ref[...]).astype(hn_ref.dtype)


def _embed_norm_call(head, x2d, g, bsz, tp):
    d = x2d.shape[1]
    nb = tp // ROW_BLOCK
    hb = head.shape[0] // ROW_BLOCK
    xb = x2d.shape[0] // bsz // ROW_BLOCK
    return pl.pallas_call(
        functools.partial(_embed_norm_kernel, blocks_per_seq=nb, head_blocks=hb),
        grid=(bsz * nb,),
        in_specs=[pl.BlockSpec((ROW_BLOCK, d), lambda i: (jnp.minimum(i % nb, hb - 1), 0)),
                  pl.BlockSpec((ROW_BLOCK, d), lambda i: ((i // nb) * xb + jnp.maximum(i % nb - hb, 0), 0)),
                  pl.BlockSpec((1, d), lambda i: (0, 0))],
        out_specs=[pl.BlockSpec((ROW_BLOCK, d), lambda i: (i, 0)),
                   pl.BlockSpec((ROW_BLOCK, d), lambda i: (i, 0))],
        out_shape=[jax.ShapeDtypeStruct((bsz * tp, d), F32),
                   jax.ShapeDtypeStruct((bsz * tp, d), BF16)],
        compiler_params=_params(("parallel",)),
    )(head, x2d, g.reshape(1, d))


def _proj_kernel(*refs, mode, rot_half, scale):
    if mode == "rope":
        x_ref, w_ref, cos_ref, sin_ref, cs_ref, o_ref = refs
    else:
        x_ref, w_ref, o_ref = refs
    acc = jnp.dot(x_ref[...], w_ref[...], preferred_element_type=F32)
    if mode == "rope":
        acc = acc * cs_ref[...]
        cos = cos_ref[...]
        sin = sin_ref[...]
        tn = acc.shape[1]
        lane = lax.broadcasted_iota(I32, (acc.shape[0], LANES), 1)
        outs = []
        for c in range(tn // LANES):
            xh = acc[:, c * LANES:(c + 1) * LANES]
            if rot_half == LANES // 2:
                rot = pltpu.roll(xh, LANES // 2, 1)
            else:
                fwd = pltpu.roll(xh, LANES - rot_half, 1)
                bwd = pltpu.roll(xh, rot_half, 1)
                rot = jnp.where((lane % (2 * rot_half)) < rot_half, fwd, bwd)
            outs.append(xh * cos + rot * sin)
        acc = outs[0] if len(outs) == 1 else jnp.concatenate(outs, axis=1)
    elif mode == "sigmoid":
        acc = jax.nn.sigmoid(acc)
    elif mode == "scale":
        acc = acc * scale
    o_ref[...] = acc.astype(o_ref.dtype)


def _proj_call(xn, w, out_dtype, mode, tm, tn, cos=None, sin=None, rot_half=0, scale=1.0, tp=None,
               col_scale=None):
    rows, d = xn.shape
    n = w.shape[1]
    in_specs = [pl.BlockSpec((tm, d), lambda i, j: (i, 0)),
                pl.BlockSpec((d, tn), lambda i, j: (0, j))]
    args = [xn, w]
    if mode == "rope":
        nt = tp // tm
        if col_scale is None:
            col_scale = jnp.ones((1, n), F32)
        in_specs += [pl.BlockSpec((tm, LANES), lambda i, j: (i % nt, 0)),
                     pl.BlockSpec((tm, LANES), lambda i, j: (i % nt, 0)),
                     pl.BlockSpec((1, tn), lambda i, j: (0, j))]
        args += [cos, sin, col_scale]
    return pl.pallas_call(
        functools.partial(_proj_kernel, mode=mode, rot_half=rot_half, scale=scale),
        grid=(rows // tm, n // tn),
        in_specs=in_specs,
        out_specs=pl.BlockSpec((tm, tn), lambda i, j: (i, j)),
        out_shape=jax.ShapeDtypeStruct((rows, n), out_dtype),
        compiler_params=_params(("parallel", "arbitrary")),
    )(*args)


def _indexer_kernel(qi_ref, kc_ref, wi_ref, mask_ref, keys_ref, qs_ref, p_ref,
                    *, front, topk, n_chunks_total):
    i = pl.program_id(1)
    n_chunk = i + 1
    n_pair = (n_chunk + 1) // 2
    tq = qi_ref.shape[0]
    n_hp = IDX_HEADS // 2
    q_idx = i * tq + lax.broadcasted_iota(I32, (LANES, tq), 1)
    k_off = lax.broadcasted_iota(I32, (LANES, tq), 0)

    w_t = wi_ref[...].T
    for p in range(n_hp):
        qs_ref[p * tq:(p + 1) * tq, :] = qi_ref[:, p * LANES:(p + 1) * LANES]

    nt_dims = (((1,), (1,)), ((), ()))

    def score_body(c4, carry):
        for u in range(SCORE_CHUNKS):
            c = SCORE_CHUNKS * c4 + u
            off = pl.multiple_of(c * LANES, LANES)
            kc = kc_ref[pl.ds(pl.multiple_of(c * 2 * LANES, 2 * LANES), 2 * LANES), :]
            acc = jnp.zeros((LANES, tq), F32)
            for g in range(n_hp // 2):
                qg = qs_ref[2 * g * tq:(2 * g + 2) * tq, :]
                st = lax.dot_general(kc, qg, nt_dims, preferred_element_type=F32)
                for v in range(2):
                    h0 = 2 * (2 * g + v)
                    sp = st[:, v * tq:(v + 1) * tq]
                    acc = acc + w_t[h0:h0 + 1, :] * jnp.maximum(sp[:LANES], 0.0)
                    acc = acc + w_t[h0 + 1:h0 + 2, :] * jnp.maximum(sp[LANES:], 0.0)
            k_idx = off + k_off
            valid = (k_idx <= q_idx) & (k_idx >= front)
            acc = jnp.where(acc == 0.0, 0.0, acc)
            score = jnp.where(valid, acc, -jnp.inf)
            bits = pltpu.bitcast(score, I32)
            key = jnp.where(bits < 0, bits ^ 0x7FFFFFFF, bits)
            keys_ref[pl.ds(off, LANES), :] = key
            carry = jnp.maximum(carry, key)
        return carry

    key_max = lax.fori_loop(0, (n_chunk + SCORE_CHUNKS - 1) // SCORE_CHUNKS, score_body,
                            jnp.full((LANES, tq), INT_MIN, I32))

    def count(pred_fn):
        def body(c2, cnt):
            off = pl.multiple_of(c2 * 2 * LANES, 2 * LANES)
            k0 = keys_ref[pl.ds(off, LANES), :]
            k1 = keys_ref[pl.ds(off + LANES, LANES), :]
            cnt = cnt + jnp.where(pred_fn(k0, off + k_off), 1, 0)
            return cnt + jnp.where(pred_fn(k1, off + LANES + k_off), 1, 0)
        cnt = lax.fori_loop(0, n_pair, body, jnp.zeros((LANES, tq), I32))
        return jnp.sum(cnt, axis=0, keepdims=True)

    bits_per_check = 4

    def bit_cond(state):
        it, _, cnt_t = state
        return (it < 32) & (jnp.max(jnp.where(cnt_t != topk, 1, 0)) > 0)

    def bit_body(state):
        it, t, cnt_t = state
        for u in range(bits_per_check):
            trial = t + jnp.left_shift(jnp.int32(1), 31 - u - it)
            cnt = count(lambda k, _: k >= trial)
            take = cnt >= topk
            t = jnp.where(take, trial, t)
            cnt_t = jnp.where(take, cnt, cnt_t)
        return it + bits_per_check, t, cnt_t

    full_search = (jnp.int32(0), jnp.full((1, tq), INT_MIN, I32), jnp.full((1, tq), -1, I32))

    def bucket_start():
        base = jnp.max(key_max, axis=0, keepdims=True) & jnp.int32(-EXP_UNIT)
        found = jnp.zeros((1, tq), I32)
        t_b = jnp.full((1, tq), INT_MIN, I32)
        c_b = jnp.full((1, tq), -1, I32)
        for d in range(SPEC_BUCKETS):
            start = base - d * EXP_UNIT
            cnt = count(lambda k, _: k >= start)
            first = (cnt >= topk) & (found == 0)
            t_b = jnp.where(first, start, t_b)
            c_b = jnp.where(first, cnt, c_b)
            found = jnp.where(cnt >= topk, 1, found)
        usable = (found > 0) & (base < jnp.int32(0x7F800000))
        every = jnp.min(jnp.where(usable, 1, 0)) > 0
        return (jnp.where(every, jnp.int32(8), jnp.int32(0)), jnp.where(every, t_b, INT_MIN),
                jnp.where(every, c_b, -1))

    state0 = lax.cond(n_chunk >= SPEC_MIN_CHUNKS, bucket_start, lambda: full_search)
    _, thr, cnt_ge = lax.while_loop(bit_cond, bit_body, state0)

    finite = thr > KEY_NEG_INF
    tie_rows = finite & (cnt_ge > topk)
    p_ref[...] = jnp.full(p_ref.shape, (n_chunks_total + 1) * LANES, I32)

    @pl.when(jnp.max(jnp.where(tie_rows, 1, 0)) > 0)
    def _():
        cnt_gt = count(lambda k, _: k > thr)
        need = topk - cnt_gt

        def idx_body(it, p):
            trial = p + jnp.left_shift(jnp.int32(1), 14 - it)
            cnt = count(lambda k, idx: (k == thr) & (idx < trial))
            return jnp.where(cnt < need, trial, p)

        p = lax.fori_loop(0, 15, idx_body, jnp.zeros((1, tq), I32))
        p = jnp.where(tie_rows, p, (n_chunks_total + 1) * LANES)
        p_ref[...] = jnp.broadcast_to(p, p_ref.shape)

    p_lim = p_ref[0:1, :]
    eye = jnp.where(lax.broadcasted_iota(I32, (tq, tq), 0) == lax.broadcasted_iota(I32, (tq, tq), 1),
                    1.0, 0.0).astype(BF16)

    per_trip = 4

    def write_body(c4, carry):
        for u in range(per_trip):
            c = jnp.minimum(per_trip * c4 + u, n_chunk - 1)
            off = pl.multiple_of(c * LANES, LANES)
            k = keys_ref[pl.ds(off, LANES), :]
            k_idx = off + k_off
            sel = (k > thr) | ((k == thr) & (k_idx <= p_lim))
            valid = (k_idx <= q_idx) & (k_idx >= front)
            sel_t = jnp.where(sel & valid, 1.0, 0.0).astype(BF16)
            sel_qk = lax.dot_general(eye, sel_t, nt_dims, preferred_element_type=F32)
            mask_ref[:, pl.ds(off, LANES)] = sel_qk.astype(jnp.int8)
        return carry

    lax.fori_loop(0, (n_chunk + per_trip - 1) // per_trip, write_body, 0)

    def zero_body(c, carry):
        off = pl.multiple_of(c * LANES, LANES)
        mask_ref[:, pl.ds(off, LANES)] = jnp.zeros((tq, LANES), jnp.int8)
        return carry

    lax.fori_loop(n_chunk, n_chunks_total, zero_body, 0)


def _indexer_call(idx_qk, k_cat, wi, bsz, tp, front, topk):
    tq = ROW_BLOCK
    nq = tp // tq
    n_qcols = IDX_HEADS * IDX_DIM
    return pl.pallas_call(
        functools.partial(_indexer_kernel, front=front, topk=topk, n_chunks_total=nq),
        grid=(bsz, nq),
        in_specs=[pl.BlockSpec((tq, n_qcols), lambda b, i: (b * nq + i, 0)),
                  pl.BlockSpec((2 * (tp + (SCORE_CHUNKS - 1) * LANES), LANES), lambda b, i: (b, 0)),
                  pl.BlockSpec((tq, LANES), lambda b, i: (b * nq + i, 0))],
        out_specs=pl.BlockSpec((tq, tp), lambda b, i: (b * nq + i, 0)),
        out_shape=jax.ShapeDtypeStruct((bsz * tp, tp), jnp.int8),
        scratch_shapes=[pltpu.VMEM((tp + (SCORE_CHUNKS - 1) * LANES, tq), I32),
                        pltpu.VMEM((n_qcols // LANES * tq, LANES), BF16),
                        pltpu.VMEM((8, tq), I32)],
        compiler_params=_params(("parallel", "arbitrary")),
    )(idx_qk, k_cat, wi)


def _causal_tile_pairs(nb):
    pairs = [(i, j) for i in range(nb) for j in range(i + 1)]
    return (jnp.array([p[0] for p in pairs], I32), jnp.array([p[1] for p in pairs], I32))


def _attn_a_kernel(qt_ref, kt_ref, q_ref, k_ref, v_ref, mask_ref, o_ref, m_ref, l_ref, s_ref, p_ref,
                   *acc_refs):
    t = pl.program_id(1)
    i = qt_ref[t]
    j = kt_ref[t]
    tt = q_ref.shape[0]
    nt_dims = (((1,), (1,)), ((), ()))

    @pl.when(j == 0)
    def _():
        m_ref[...] = jnp.full(m_ref.shape, NEG_BIG, F32)
        l_ref[...] = jnp.zeros(l_ref.shape, F32)
        for acc_ref in acc_refs:
            acc_ref[...] = jnp.zeros(acc_ref.shape, F32)

    def group(rows, kw):
        lane = lax.broadcasted_iota(I32, (ROW_BLOCK, LANES), 1)
        bias = jnp.where(mask_ref[rows, :kw].astype(I32) != 0, 0.0, NEG_BIG)
        m_all = m_ref[rows, :]
        m_loc = m_all
        for h in range(A_HEADS):
            sl = slice(h * HEAD_DIM, (h + 1) * HEAD_DIM)
            s = lax.dot_general(q_ref[rows, sl], k_ref[:kw, sl], nt_dims, preferred_element_type=F32) + bias
            s_ref[h, :, :kw] = s
            m_loc = jnp.where(lane == h, jnp.max(s, axis=1, keepdims=True), m_loc)
        m_new = jnp.maximum(m_all, m_loc)
        alpha = jnp.exp2(m_all - m_new)
        l_out = alpha * l_ref[rows, :]
        for h in range(A_HEADS):
            p = jnp.exp2(s_ref[h, :, :kw] - m_new[:, h:h + 1])
            p_ref[h, :, :kw] = p.astype(BF16)
            l_out = l_out + jnp.where(lane == h, jnp.sum(p, axis=1, keepdims=True), 0.0)
        for h in range(A_HEADS):
            sl = slice(h * HEAD_DIM, (h + 1) * HEAD_DIM)
            acc_refs[h][rows, :] = alpha[:, h:h + 1] * acc_refs[h][rows, :] + jnp.dot(
                p_ref[h, :, :kw], v_ref[:kw, sl], preferred_element_type=F32)
        m_ref[rows, :] = m_new
        l_ref[rows, :] = l_out

    @pl.when(j < i)
    def _():
        def body(r, carry):
            group(pl.ds(pl.multiple_of(r * ROW_BLOCK, ROW_BLOCK), ROW_BLOCK), tt)
            return carry

        lax.fori_loop(0, tt // ROW_BLOCK, body, 0)

    @pl.when(j == i)
    def _():
        for r in range(tt // ROW_BLOCK):
            group(slice(r * ROW_BLOCK, (r + 1) * ROW_BLOCK), (r + 1) * ROW_BLOCK)
        l_all = l_ref[...]
        for h in range(A_HEADS):
            sl = slice(h * HEAD_DIM, (h + 1) * HEAD_DIM)
            o_ref[:, sl] = (acc_refs[h][...] / l_all[:, h:h + 1]).astype(o_ref.dtype)


def _attn_a_call(qk, v, mask, bsz, tp, tt):
    nb = tp // tt
    w = A_WIDTH
    q_tile, k_tile = _causal_tile_pairs(nb)
    grid_spec = pltpu.PrefetchScalarGridSpec(
        num_scalar_prefetch=2,
        grid=(bsz, q_tile.shape[0]),
        in_specs=[pl.BlockSpec((tt, w), lambda b, t, qt, kt: (b * nb + qt[t], 0)),
                  pl.BlockSpec((tt, w), lambda b, t, qt, kt: (b * nb + kt[t], 1)),
                  pl.BlockSpec((tt, w), lambda b, t, qt, kt: (b * nb + kt[t], 0)),
                  pl.BlockSpec((tt, tt), lambda b, t, qt, kt: (b * nb + qt[t], kt[t]))],
        out_specs=pl.BlockSpec((tt, w), lambda b, t, qt, kt: (b * nb + qt[t], 0)),
        scratch_shapes=[pltpu.VMEM((tt, LANES), F32), pltpu.VMEM((tt, LANES), F32),
                        pltpu.VMEM((A_HEADS, ROW_BLOCK, tt), F32),
                        pltpu.VMEM((A_HEADS, ROW_BLOCK, tt), BF16)]
        + [pltpu.VMEM((tt, HEAD_DIM), F32) for _ in range(A_HEADS)],
    )
    return pl.pallas_call(
        _attn_a_kernel,
        grid_spec=grid_spec,
        out_shape=jax.ShapeDtypeStruct((bsz * tp, w), BF16),
        compiler_params=_params(("parallel", "arbitrary")),
    )(q_tile, k_tile, qk, qk, v, mask)


def _attn_b_kernel(qt_ref, kt_ref, q_ref, k_ref, v_ref, lq1_ref, lk1_ref, lq2_ref, lk2_ref, g_ref, o_ref,
                   m_ref, l_ref, s_ref, p_ref, acc_ref, *, front, lambda_init):
    t = pl.program_id(1)
    i = qt_ref[t]
    j = kt_ref[t]
    tt = q_ref.shape[0]
    nt_dims = (((1,), (1,)), ((), ()))
    dv = 2 * HEAD_DIM
    n_set = 2 * B_HEADS

    @pl.when(j == 0)
    def _():
        m_ref[...] = jnp.full(m_ref.shape, NEG_BIG, F32)
        l_ref[...] = jnp.zeros(l_ref.shape, F32)
        acc_ref[...] = jnp.zeros(acc_ref.shape, F32)

    def group(rows, r0, kw, masked):
        lane = lax.broadcasted_iota(I32, (ROW_BLOCK, LANES), 1)
        if masked:
            q_idx = i * tt + r0 + lax.broadcasted_iota(I32, (ROW_BLOCK, kw), 0)
            k_idx = j * tt + lax.broadcasted_iota(I32, (ROW_BLOCK, kw), 1)
            bias = jnp.where((k_idx <= q_idx) & (k_idx >= front), 0.0, NEG_BIG)
        m_all = m_ref[rows, :]
        m_loc = m_all
        for c in range(n_set):
            sl = slice(c * HEAD_DIM, (c + 1) * HEAD_DIM)
            s = lax.dot_general(q_ref[rows, sl], k_ref[:kw, sl], nt_dims, preferred_element_type=F32)
            if masked:
                s = s + bias
            s_ref[c, :, :kw] = s
            m_loc = jnp.where(lane == c, jnp.max(s, axis=1, keepdims=True), m_loc)
        m_new = jnp.maximum(m_all, m_loc)
        alpha = jnp.exp2(m_all - m_new)
        l_out = alpha * l_ref[rows, :]
        for c in range(n_set):
            p = jnp.exp2(s_ref[c, :, :kw] - m_new[:, c:c + 1])
            p_ref[c, :, :kw] = p.astype(BF16)
            l_out = l_out + jnp.where(lane == c, jnp.sum(p, axis=1, keepdims=True), 0.0)
        for c in range(n_set):
            vh = v_ref[:kw, (c // 2) * dv:(c // 2 + 1) * dv]
            acc_ref[c, rows, :] = alpha[:, c:c + 1] * acc_ref[c, rows, :] + jnp.dot(
                p_ref[c, :, :kw], vh, preferred_element_type=F32)
        m_ref[rows, :] = m_new
        l_ref[rows, :] = l_out

    def full_tile(masked):
        def body(r, carry):
            r0 = pl.multiple_of(r * ROW_BLOCK, ROW_BLOCK)
            group(pl.ds(r0, ROW_BLOCK), r0, tt, masked)
            return carry

        lax.fori_loop(0, tt // ROW_BLOCK, body, 0)

    pl.when((j == 0) & (j < i))(functools.partial(full_tile, True))
    pl.when((j > 0) & (j < i))(functools.partial(full_tile, False))

    @pl.when(j == i)
    def _():
        for r in range(tt // ROW_BLOCK):
            group(slice(r * ROW_BLOCK, (r + 1) * ROW_BLOCK), r * ROW_BLOCK, (r + 1) * ROW_BLOCK, True)
        lam = (jnp.exp(jnp.sum(lq1_ref[...] * lk1_ref[...], axis=1, keepdims=True))
               - jnp.exp(jnp.sum(lq2_ref[...] * lk2_ref[...], axis=1, keepdims=True))
               + lambda_init)
        g = g_ref[...]
        l_all = l_ref[...]
        for h in range(B_HEADS):
            o = (acc_ref[2 * h] / l_all[:, 2 * h:2 * h + 1]
                 - lam * (acc_ref[2 * h + 1] / l_all[:, 2 * h + 1:2 * h + 2]))
            ms = jnp.mean(o * o, axis=-1, keepdims=True)
            y = o * lax.rsqrt(ms + RMS_EPS) * g
            o_ref[:, h * dv:(h + 1) * dv] = (y * (1.0 - lambda_init)).astype(o_ref.dtype)


def _attn_b_call(qk, v, lq1, lk1, lq2, lk2, subln_g, bsz, tp, tt, front, lambda_init):
    nb = tp // tt
    w = B_QK_WIDTH
    dv = 2 * HEAD_DIM
    vec = lambda a: a.reshape(1, -1).astype(F32)
    small = lambda n: pl.BlockSpec((1, n), lambda b, t, qt, kt: (0, 0))
    q_tile, k_tile = _causal_tile_pairs(nb)
    grid_spec = pltpu.PrefetchScalarGridSpec(
        num_scalar_prefetch=2,
        grid=(bsz, q_tile.shape[0]),
        in_specs=[pl.BlockSpec((tt, w), lambda b, t, qt, kt: (b * nb + qt[t], 2)),
                  pl.BlockSpec((tt, w), lambda b, t, qt, kt: (b * nb + kt[t], 3)),
                  pl.BlockSpec((tt, B_V_WIDTH), lambda b, t, qt, kt: (b * nb + kt[t], 1)),
                  small(HEAD_DIM), small(HEAD_DIM), small(HEAD_DIM), small(HEAD_DIM), small(dv)],
        out_specs=pl.BlockSpec((tt, B_V_WIDTH), lambda b, t, qt, kt: (b * nb + qt[t], 0)),
        scratch_shapes=[pltpu.VMEM((tt, LANES), F32),
                        pltpu.VMEM((tt, LANES), F32),
                        pltpu.VMEM((2 * B_HEADS, ROW_BLOCK, tt), F32),
                        pltpu.VMEM((2 * B_HEADS, ROW_BLOCK, tt), BF16),
                        pltpu.VMEM((2 * B_HEADS, tt, dv), F32)],
    )
    return pl.pallas_call(
        functools.partial(_attn_b_kernel, front=front, lambda_init=lambda_init),
        grid_spec=grid_spec,
        out_shape=jax.ShapeDtypeStruct((bsz * tp, B_V_WIDTH), BF16),
        compiler_params=_params(("parallel", "arbitrary")),
    )(q_tile, k_tile, qk, qk, v, vec(lq1), vec(lk1), vec(lq2), vec(lk2), vec(subln_g))


def _merge_router_kernel(oa_ref, ob_ref, ga_ref, gb_ref, h_ref, wua_ref, wub_ref, wo_ref, g_ref,
                         wr_ref, br_ref, h1_ref, hfp_ref, route_ref, gate_ref, cnt_ref, carry_ref,
                         *, tp, front):
    i = pl.program_id(0)
    tm = oa_ref.shape[0]
    d = h_ref.shape[1]

    @pl.when(i == 0)
    def _():
        carry_ref[...] = jnp.zeros(carry_ref.shape, F32)

    ua = jnp.dot(oa_ref[...], wua_ref[...], preferred_element_type=F32)
    ub = jnp.dot(ob_ref[...], wub_ref[...], preferred_element_type=F32)
    merged = ga_ref[...].astype(F32) * ua + gb_ref[...].astype(F32) * ub
    mix = jnp.dot(merged.astype(BF16), wo_ref[...], preferred_element_type=F32)
    h1 = h_ref[...] + mix
    h1_ref[...] = h1

    ms = jnp.mean(h1 * h1, axis=-1, keepdims=True)
    hf = (h1 * lax.rsqrt(ms + RMS_EPS) * g_ref[...]).astype(BF16)

    hf_bits = pltpu.bitcast(hf.astype(F32), jnp.uint32)
    hfp_ref[...] = (hf_bits[:, :d // 2] & jnp.uint32(0xFFFF0000)) | (hf_bits[:, d // 2:] >> 16)

    logits = jnp.dot(hf, wr_ref[...], preferred_element_type=F32) + br_ref[...]
    lane = lax.broadcasted_iota(I32, (tm, LANES), 1)
    row = (i * tm + lax.broadcasted_iota(I32, (tm, 1), 0)) % tp
    routed = row >= front

    work = logits
    vals, idxs, onehots = [], [], []
    for _ in range(TOP_K):
        v = jnp.max(work, axis=1, keepdims=True)
        e = jnp.min(jnp.where(work == v, lane, LANES), axis=1, keepdims=True)
        hit = lane == e
        vals.append(v)
        idxs.append(e)
        onehots.append(hit)
        work = jnp.where(hit, -jnp.inf, work)

    exps = [jnp.exp(v - vals[0]) for v in vals]
    denom = exps[0] + exps[1] + exps[2] + exps[3]

    member = jnp.zeros((tm, LANES), F32)
    for hit in onehots:
        member = member + jnp.where(hit & routed, 1.0, 0.0)
    r_i = lax.broadcasted_iota(I32, (tm, tm), 0)
    c_i = lax.broadcasted_iota(I32, (tm, tm), 1)
    lower = jnp.where(c_i < r_i, 1.0, 0.0).astype(BF16)
    before = jnp.dot(lower, member.astype(BF16), preferred_element_type=F32) + carry_ref[...]
    carry_ref[...] = carry_ref[...] + jnp.sum(member, axis=0, keepdims=True)

    route = jnp.zeros((tm, LANES), I32)
    gates = jnp.zeros((tm, LANES), F32)
    for k in range(TOP_K):
        rank = jnp.sum(jnp.where(onehots[k], before, 0.0), axis=1, keepdims=True)
        route = jnp.where(lane == k, idxs[k], route)
        route = jnp.where(lane == TOP_K + k, rank.astype(I32), route)
        gates = jnp.where(lane == k, exps[k] / denom, gates)
    route_ref[...] = route
    gate_ref[...] = gates
    cnt_ref[...] = jnp.broadcast_to(carry_ref[...], cnt_ref.shape)


def _merge_router_call(o_a, o_b, gates, h0, wua, wub, wo, g_ffn, wr, br, tp, front, tm):
    rows, d = h0.shape
    nt = rows // tm
    once = pl.Buffered(1)
    const = lambda shape: pl.BlockSpec(shape, lambda i: (0, 0), pipeline_mode=once)
    return pl.pallas_call(
        functools.partial(_merge_router_kernel, tp=tp, front=front),
        grid=(nt,),
        in_specs=[pl.BlockSpec((tm, A_WIDTH), lambda i: (i, 0)),
                  pl.BlockSpec((tm, B_V_WIDTH), lambda i: (i, 0)),
                  pl.BlockSpec((tm, d), lambda i: (i, 0)),
                  pl.BlockSpec((tm, d), lambda i: (i, 1)),
                  pl.BlockSpec((tm, d), lambda i: (i, 0)),
                  const((A_WIDTH, d)), const((B_V_WIDTH, d)), const((d, d)),
                  const((1, d)), const((d, LANES)), const((1, LANES))],
        out_specs=[pl.BlockSpec((tm, d), lambda i: (i, 0)),
                   pl.BlockSpec((tm, d // 2), lambda i: (i, 0)),
                   pl.BlockSpec((tm, LANES), lambda i: (i, 0)),
                   pl.BlockSpec((tm, LANES), lambda i: (i, 0)),
                   pl.BlockSpec((8, LANES), lambda i: (0, 0))],
        out_shape=[jax.ShapeDtypeStruct((rows, d), F32),
                   jax.ShapeDtypeStruct((rows, d // 2), jnp.uint32),
                   jax.ShapeDtypeStruct((rows, LANES), I32),
                   jax.ShapeDtypeStruct((rows, LANES), F32),
                   jax.ShapeDtypeStruct((8, LANES), F32)],
        scratch_shapes=[pltpu.VMEM((1, LANES), F32)],
        compiler_params=_params(("arbitrary",)),
    )(o_a, o_b, gates, gates, h0, wua, wub, wo, g_ffn, wr, br)


def _dispatch_kernel(dest_ref, hfp_ref, xb_in_ref, xb_ref, sem):
    del xb_in_ref
    tm = hfp_ref.shape[0]

    def row_copy(r, k):
        d = dest_ref[r * TOP_K + k]
        return pltpu.make_async_copy(hfp_ref.at[pl.ds(r, 1), :], xb_ref.at[pl.ds(d, 1), :], sem)

    def issue(r, carry):
        for k in range(TOP_K):
            row_copy(r, k).start()
        return carry

    lax.fori_loop(0, tm, issue, 0, unroll=8)

    def drain(r, carry):
        for k in range(TOP_K):
            row_copy(r, k).wait()
        return carry

    lax.fori_loop(0, tm, drain, 0, unroll=8)


def _dispatch_call(dest_flat, hfp, n_rows_total, tm):
    rows, half = hfp.shape
    xb0 = jnp.zeros((n_rows_total, half), jnp.uint32)
    return pl.pallas_call(
        _dispatch_kernel,
        grid=(rows // tm,),
        in_specs=[pl.BlockSpec((tm * TOP_K,), lambda i: (i,), memory_space=pltpu.SMEM),
                  pl.BlockSpec((tm, half), lambda i: (i, 0)),
                  pl.BlockSpec(memory_space=pl.ANY)],
        out_specs=pl.BlockSpec(memory_space=pl.ANY),
        out_shape=jax.ShapeDtypeStruct((n_rows_total, half), jnp.uint32),
        scratch_shapes=[pltpu.SemaphoreType.DMA(())],
        input_output_aliases={2: 0},
        compiler_params=_params(("arbitrary",)),
    )(dest_flat, hfp, xb0)


def _w1_prep_kernel(w_ref, o_ref):
    n = 2 * LANES
    s_i = lax.broadcasted_iota(I32, (n, n), 0)
    j_i = lax.broadcasted_iota(I32, (n, n), 1)
    src = jnp.where(j_i < LANES, 2 * j_i, 2 * (j_i - LANES) + 1)
    perm = jnp.where(s_i == src, 1.0, 0.0).astype(BF16)
    for c in range(w_ref.shape[2] // n):
        blk = w_ref[0, :, c * n:(c + 1) * n].astype(BF16)
        o_ref[0, :, c * n:(c + 1) * n] = jnp.dot(blk, perm, preferred_element_type=F32).astype(BF16)


def _w1_prep_call(w1l, tk):
    n_exp, d, two_ff = w1l.shape
    return pl.pallas_call(
        _w1_prep_kernel,
        grid=(n_exp, d // tk),
        in_specs=[pl.BlockSpec((1, tk, two_ff), lambda e, k: (e, k, 0))],
        out_specs=pl.BlockSpec((1, tk, two_ff), lambda e, k: (e, k, 0)),
        out_shape=jax.ShapeDtypeStruct((n_exp, d, two_ff), BF16),
        compiler_params=_params(("parallel", "parallel")),
    )(w1l)


def _expert_kernel(be_ref, nu_ref, xb_ref, w1_ref, b1_ref, w2_ref, b2_ref, y_ref):
    i = pl.program_id(0)
    d_ff = w2_ref.shape[1]

    @pl.when(i < nu_ref[0])
    def _():
        packed = xb_ref[...]
        hi = pltpu.bitcast(packed & jnp.uint32(0xFFFF0000), F32)
        lo = pltpu.bitcast(packed << 16, F32)
        x = jnp.concatenate([hi, lo], axis=1).astype(BF16)
        u = jnp.dot(x, w1_ref[0], preferred_element_type=F32) + b1_ref[0]
        n_grp = d_ff // LANES
        glu = jnp.concatenate([u[:, 2 * c * LANES:(2 * c + 1) * LANES] for c in range(n_grp)], axis=1)
        lin = jnp.concatenate([u[:, (2 * c + 1) * LANES:(2 * c + 2) * LANES] for c in range(n_grp)], axis=1)
        x_glu = jnp.minimum(glu, SWIGLU_LIMIT)
        x_lin = jnp.clip(lin, -SWIGLU_LIMIT, SWIGLU_LIMIT)
        act = x_glu * jax.nn.sigmoid(SWIGLU_ALPHA * x_glu) * (x_lin + 1.0)
        y_ref[...] = jnp.dot(act.astype(BF16), w2_ref[0], preferred_element_type=F32) + b2_ref[0]

    @pl.when(i >= nu_ref[0])
    def _():
        y_ref[...] = jnp.zeros(y_ref.shape, F32)


def _expert_call(block_e, n_used, xb, w1p, b1p, w2b, b2, n_blocks):
    n_exp, d, two_ff = w1p.shape
    d_ff = two_ff // 2
    half = xb.shape[1]

    def blk(i, be, nu):
        return jnp.minimum(i, nu[0] - 1)

    grid_spec = pltpu.PrefetchScalarGridSpec(
        num_scalar_prefetch=2,
        grid=(n_blocks,),
        in_specs=[pl.BlockSpec((ROW_BLOCK, half), lambda i, be, nu: (blk(i, be, nu), 0)),
                  pl.BlockSpec((1, d, two_ff), lambda i, be, nu: (be[blk(i, be, nu)], 0, 0)),
                  pl.BlockSpec((1, 1, two_ff), lambda i, be, nu: (be[blk(i, be, nu)], 0, 0)),
                  pl.BlockSpec((1, d_ff, d), lambda i, be, nu: (be[blk(i, be, nu)], 0, 0)),
                  pl.BlockSpec((1, 1, d), lambda i, be, nu: (be[blk(i, be, nu)], 0, 0))],
        out_specs=pl.BlockSpec((ROW_BLOCK, d), lambda i, be, nu: (i, 0)),
    )
    return pl.pallas_call(
        _expert_kernel,
        grid_spec=grid_spec,
        out_shape=jax.ShapeDtypeStruct((n_blocks * ROW_BLOCK, d), F32),
        compiler_params=_params(("arbitrary",), vmem=60 * 1024 * 1024),
    )(block_e, n_used, xb, w1p, b1p.reshape(n_exp, 1, two_ff), w2b, b2.reshape(n_exp, 1, d))


def _combine_kernel(dest_ref, dest_next_ref, h1_ref, gate_ref, g_ref, yb_ref, o_ref, ybuf_ref, sems):
    i = pl.program_id(0)
    n = pl.num_programs(0)
    tm = h1_ref.shape[0]
    slot = i % 2

    def row_copy(idx_ref, buf, r, k):
        d = idx_ref[r * TOP_K + k]
        return pltpu.make_async_copy(yb_ref.at[pl.ds(d, 1), :], ybuf_ref.at[buf, k, pl.ds(r, 1), :],
                                     sems.at[buf])

    def issue(idx_ref, buf):
        def body(r, carry):
            for k in range(TOP_K):
                row_copy(idx_ref, buf, r, k).start()
            return carry

        lax.fori_loop(0, tm, body, 0, unroll=8)

    @pl.when(i == 0)
    def _():
        issue(dest_ref, slot)

    @pl.when(i + 1 < n)
    def _():
        issue(dest_next_ref, 1 - slot)

    def drain(r, carry):
        for k in range(TOP_K):
            row_copy(dest_ref, slot, r, k).wait()
        return carry

    lax.fori_loop(0, tm, drain, 0, unroll=8)

    gate = gate_ref[...]
    h2 = h1_ref[...]
    for k in range(TOP_K):
        h2 = h2 + gate[:, k:k + 1] * ybuf_ref[slot, k]
    ms = jnp.mean(h2 * h2, axis=-1, keepdims=True)
    o_ref[...] = h2 * lax.rsqrt(ms + RMS_EPS) * g_ref[...]


def _combine_call(dest_flat, h1, gate, g_final, yb, bsz, s_len, tp):
    d = h1.shape[1]
    tm = ROW_BLOCK
    ns = s_len // tm
    nb = tp // tm
    skip = (tp - s_len) // tm

    def src(i):
        return (i // ns) * nb + skip + i % ns

    def src_next(i):
        return src(jnp.minimum(i + 1, bsz * ns - 1))

    return pl.pallas_call(
        _combine_kernel,
        grid=(bsz * ns,),
        in_specs=[pl.BlockSpec((tm * TOP_K,), lambda i: (src(i),), memory_space=pltpu.SMEM),
                  pl.BlockSpec((tm * TOP_K,), lambda i: (src_next(i),), memory_space=pltpu.SMEM),
                  pl.BlockSpec((tm, d), lambda i: (src(i), 0)),
                  pl.BlockSpec((tm, LANES), lambda i: (src(i), 0)),
                  pl.BlockSpec((1, d), lambda i: (0, 0)),
                  pl.BlockSpec(memory_space=pl.ANY)],
        out_specs=pl.BlockSpec((tm, d), lambda i: (i, 0)),
        out_shape=jax.ShapeDtypeStruct((bsz * s_len, d), F32),
        scratch_shapes=[pltpu.VMEM((2, TOP_K, tm, d), F32), pltpu.SemaphoreType.DMA((2,))],
        compiler_params=_params(("arbitrary",)),
    )(dest_flat, dest_flat, h1, gate, g_final.reshape(1, d), yb)


def _rope_tables(pos, dim, reps):
    inv = 1.0 / (ROPE_THETA ** (jnp.arange(0, dim, 2, dtype=F32) / dim))
    ang = pos[:, None] * inv[None, :]
    cos = jnp.concatenate([jnp.cos(ang), jnp.cos(ang)], axis=-1)
    sin = jnp.concatenate([-jnp.sin(ang), jnp.sin(ang)], axis=-1)
    return jnp.tile(cos, (1, reps)), jnp.tile(sin, (1, reps))


def kernel(x, meta_tokens, norm_mix_g, w_in, lambda_q1, lambda_k1, lambda_q2, lambda_k2, subln_g, w_up_a, w_up_b, w_out, norm_ffn_g, w_router, b_router, w1, b1, w2, b2, norm_final_g):
    bsz, s_len, d = x.shape
    n_meta = meta_tokens.shape[0]
    depth = w_in.shape[0]
    n_exp = w_router.shape[-1]
    d_ff = w2.shape[-2]
    t_len = s_len + n_meta
    tp = -(-t_len // ROW_BLOCK) * ROW_BLOCK
    front = tp - t_len
    topk = min(TOPK_MAX, t_len // 4)
    rows = bsz * tp
    assert s_len % ROW_BLOCK == 0 and n_exp <= LANES and d % (2 * LANES) == 0 and d_ff % LANES == 0

    tm_proj = _pick_tile(tp, (640, 512, 256, 128))
    tt = _pick_tile(tp, (640, 512, 256, 128))
    tm_merge = _pick_tile(rows, (256, 128))

    pos = jnp.arange(tp, dtype=F32) - float(front)
    cos_h, sin_h = _rope_tables(pos, HEAD_DIM, 1)
    cos_i, sin_i = _rope_tables(pos, IDX_DIM, LANES // IDX_DIM)

    head = jnp.concatenate([jnp.zeros((front, d), x.dtype), meta_tokens.astype(x.dtype)], axis=0)

    o_qa, o_ka, o_va = 0, A_WIDTH, 2 * A_WIDTH
    o_qi = 3 * A_WIDTH
    o_ki = o_qi + IDX_HEADS * IDX_DIM
    o_wi = o_ki + IDX_DIM
    o_qb = o_wi + IDX_HEADS
    o_kb = o_qb + B_QK_WIDTH
    o_vb = o_kb + B_QK_WIDTH
    o_ga = o_vb + B_V_WIDTH
    o_gb = o_ga + d

    for l in range(depth):
        lambda_init = 0.8 - 0.6 * math.exp(-0.3 * l)
        wl = w_in[l]
        col = lambda o, n: wl[:, o:o + n]
        zeros_half = jnp.zeros((d, LANES - IDX_DIM), wl.dtype)
        w_rope = jnp.concatenate([col(o_qa, A_WIDTH), col(o_ka, A_WIDTH),
                                  col(o_qb, B_QK_WIDTH), col(o_kb, B_QK_WIDTH)], axis=1).astype(BF16)
        w_val = jnp.concatenate([col(o_va, A_WIDTH), col(o_vb, B_V_WIDTH)], axis=1).astype(BF16)
        w_gate = jnp.concatenate([col(o_ga, d), col(o_gb, d)], axis=1).astype(BF16)
        w_idx = jnp.concatenate([col(o_qi, IDX_HEADS * IDX_DIM), col(o_ki, IDX_DIM), zeros_half,
                                 zeros_half, col(o_ki, IDX_DIM)], axis=1).astype(BF16)
        w_wi = jnp.concatenate([col(o_wi, IDX_HEADS), jnp.zeros((d, LANES - IDX_HEADS), wl.dtype)],
                               axis=1).astype(BF16)

        h, hn = _embed_norm_call(head, x.reshape(bsz * s_len, d), norm_mix_g[l], bsz, tp)
        q_scale = HEAD_DIM ** -0.5 * math.log2(math.e)
        col_scale = jnp.concatenate([jnp.full((A_WIDTH,), q_scale, F32), jnp.ones((A_WIDTH,), F32),
                                     jnp.full((B_QK_WIDTH,), q_scale, F32), jnp.ones((B_QK_WIDTH,), F32)])
        qk = _proj_call(hn, w_rope, BF16, "rope", tm_proj, 1024, cos_h, sin_h, HEAD_DIM // 2, tp=tp,
                        col_scale=col_scale.reshape(1, -1))
        val = _proj_call(hn, w_val, BF16, "none", tm_proj, 1024)
        gates = _proj_call(hn, w_gate, BF16, "sigmoid", tm_proj, _pick_tile(2 * d, (1024, 512, 256)))
        idx_qk = _proj_call(hn, w_idx, BF16, "rope", tm_proj, (IDX_HEADS * IDX_DIM + 2 * LANES) // 2,
                            cos_i, sin_i, IDX_DIM // 2, tp=tp)
        wi = _proj_call(hn, w_wi, F32, "scale", tm_proj, LANES,
                        scale=IDX_HEADS ** -0.5 * IDX_DIM ** -0.5)

        n_qcols = IDX_HEADS * IDX_DIM
        nc = tp // LANES
        k_cat = jnp.stack([idx_qk[:, n_qcols:n_qcols + LANES].reshape(bsz, nc, LANES, LANES),
                           idx_qk[:, n_qcols + LANES:].reshape(bsz, nc, LANES, LANES)], axis=2)
        k_cat = jnp.pad(k_cat, ((0, 0), (0, SCORE_CHUNKS - 1), (0, 0), (0, 0), (0, 0))).reshape(
            bsz * 2 * (tp + (SCORE_CHUNKS - 1) * LANES), LANES)
        mask = _indexer_call(idx_qk, k_cat, wi, bsz, tp, front, topk)
        o_a = _attn_a_call(qk, val, mask, bsz, tp, tt)
        o_b = _attn_b_call(qk, val, lambda_q1[l], lambda_k1[l], lambda_q2[l], lambda_k2[l], subln_g[l],
                           bsz, tp, tt, front, lambda_init)

        wr = jnp.concatenate([w_router[l], jnp.zeros((d, LANES - n_exp), F32)], axis=1).astype(BF16)
        br = jnp.concatenate([b_router[l], jnp.full((LANES - n_exp,), NEG_BIG, F32)]).reshape(1, LANES)
        h1, hfp, route, gate, cnt = _merge_router_call(
            o_a, o_b, gates, h, w_up_a[l].astype(BF16), w_up_b[l].astype(BF16), w_out[l].astype(BF16),
            norm_ffn_g[l].reshape(1, d), wr, br, tp, front, tm_merge)

        n_assign = bsz * t_len * TOP_K
        n_blocks = (n_assign + n_exp * (ROW_BLOCK - 1) + ROW_BLOCK - 1) // ROW_BLOCK
        n_rows = n_blocks * ROW_BLOCK
        counts = cnt[0, :n_exp].astype(I32)
        padded = (counts + ROW_BLOCK - 1) // ROW_BLOCK * ROW_BLOCK
        pad_end = jnp.cumsum(padded)
        pad_start = pad_end - padded
        n_used = (pad_end[-1:] // ROW_BLOCK).astype(I32)
        blk_start = jnp.arange(n_blocks, dtype=I32) * ROW_BLOCK
        block_e = jnp.minimum(jnp.sum((pad_end[None, :] <= blk_start[:, None]).astype(I32), axis=1),
                              n_exp - 1).astype(I32)
        top_e = route[:, :TOP_K]
        rank = route[:, TOP_K:2 * TOP_K]
        row_id = jnp.arange(rows, dtype=I32)
        is_pad = (row_id % tp) < front
        n_trash = bsz * front * TOP_K
        trash = (n_rows + ((row_id // tp) * front + row_id % tp)[:, None] * TOP_K
                 + jnp.arange(TOP_K, dtype=I32)[None, :])
        dest = jnp.where(is_pad[:, None], trash, pad_start[top_e] + rank)
        dest_flat = dest.reshape(-1).astype(I32)
        n_rows_total = n_rows + -(-max(n_trash, 1) // 8) * 8

        xb = _dispatch_call(dest_flat, hfp, n_rows_total, tm_merge)

        w1p = _w1_prep_call(w1[l], _pick_tile(d, (256, 128)))
        b1p = b1[l].reshape(n_exp, d_ff // LANES, LANES, 2).transpose(0, 1, 3, 2).reshape(n_exp, 2 * d_ff)
        yb = _expert_call(block_e, n_used, xb, w1p, b1p, w2[l].astype(BF16), b2[l], n_blocks)

        if l + 1 < depth:
            raise NotImplementedError("only the single-layer configuration is implemented")
        out = _combine_call(dest_flat, h1, gate, norm_final_g, yb, bsz, s_len, tp)
    return out.reshape(bsz, s_len, d)
```

```python
import functools
import math

import jax
import jax.numpy as jnp
import numpy as np
from jax import lax
from jax.experimental import pallas as pl
from jax.experimental.pallas import tpu as pltpu

HEAD_DIM = 128
A_HEADS = 8
IDX_HEADS = 16
IDX_DIM = 64
TOPK_MAX = 256
B_HEADS = 4
TOP_K = 4
SWIGLU_LIMIT = 7.0
SWIGLU_ALPHA = 1.702
ROPE_THETA = 10000.0
RMS_EPS = 1e-5
ROW_BLOCK = 128
LANES = 128
NEG_BIG = -1e30
VMEM_LIMIT = 56 * 1024 * 1024
SCORE_CHUNKS = 4
EXP_UNIT = 1 << 23
SPEC_BUCKETS = 4
SPEC_MIN_CHUNKS = 6

A_WIDTH = A_HEADS * HEAD_DIM
B_QK_WIDTH = B_HEADS * 2 * HEAD_DIM
B_V_WIDTH = B_HEADS * 2 * HEAD_DIM

F32 = jnp.float32
BF16 = jnp.bfloat16
I32 = jnp.int32

KEY_NEG_INF = int(np.array(0xFF800000, np.uint32).view(np.int32)) ^ 0x7FFFFFFF
INT_MIN = -(2 ** 31)


def _pick_tile(n, candidates):
    for c in candidates:
        if n % c == 0:
            return c
    raise ValueError(f"no tile for {n}")


def _params(sem, vmem=VMEM_LIMIT):
    return pltpu.CompilerParams(dimension_semantics=sem, vmem_limit_bytes=vmem)


def _embed_norm_kernel(head_ref, x_ref, g_ref, h_ref, hn_ref, *, blocks_per_seq, head_blocks):
    r = pl.program_id(0) % blocks_per_seq
    x = jnp.where(r < head_blocks, head_ref[...], x_ref[...])
    h_ref[...] = x
    ms = jnp.mean(x * x, axis=-1, keepdims=True)
    hn_ref[...] = (x * lax.rsqrt(ms + RMS_EPS) * g_ref[...]).astype(hn_ref.dtype)


def _embed_norm_call(head, x2d, g, bsz, tp):
    d = x2d.shape[1]
    nb = tp // ROW_BLOCK
    hb = head.shape[0] // ROW_BLOCK
    xb = x2d.shape[0] // bsz // ROW_BLOCK
    return pl.pallas_call(
        functools.partial(_embed_norm_kernel, blocks_per_seq=nb, head_blocks=hb),
        grid=(bsz * nb,),
        in_specs=[pl.BlockSpec((ROW_BLOCK, d), lambda i: (jnp.minimum(i % nb, hb - 1), 0)),
                  pl.BlockSpec((ROW_BLOCK, d), lambda i: ((i // nb) * xb + jnp.maximum(i % nb - hb, 0), 0)),
                  pl.BlockSpec((1, d), lambda i: (0, 0))],
        out_specs=[pl.BlockSpec((ROW_BLOCK, d), lambda i: (i, 0)),
                   pl.BlockSpec((ROW_BLOCK, d), lambda i: (i, 0))],
        out_shape=[jax.ShapeDtypeStruct((bsz * tp, d), F32),
                   jax.ShapeDtypeStruct((bsz * tp, d), BF16)],
        compiler_params=_params(("parallel",)),
    )(head, x2d, g.reshape(1, d))


def _proj_kernel(*refs, mode, rot_half, scale):
    if mode == "rope":
        x_ref, w_ref, cos_ref, sin_ref, cs_ref, o_ref = refs
    else:
        x_ref, w_ref, o_ref = refs
    acc = jnp.dot(x_ref[...], w_ref[...], preferred_element_type=F32)
    if mode == "rope":
        acc = acc * cs_ref[...]
        cos = cos_ref[...]
        sin = sin_ref[...]
        tn = acc.shape[1]
        lane = lax.broadcasted_iota(I32, (acc.shape[0], LANES), 1)
        outs = []
        for c in range(tn // LANES):
            xh = acc[:, c * LANES:(c + 1) * LANES]
            if rot_half == LANES // 2:
                rot = pltpu.roll(xh, LANES // 2, 1)
            else:
                fwd = pltpu.roll(xh, LANES - rot_half, 1)
                bwd = pltpu.roll(xh, rot_half, 1)
                rot = jnp.where((lane % (2 * rot_half)) < rot_half, fwd, bwd)
            outs.append(xh * cos + rot * sin)
        acc = outs[0] if len(outs) == 1 else jnp.concatenate(outs, axis=1)
    elif mode == "sigmoid":
        acc = jax.nn.sigmoid(acc)
    elif mode == "scale":
        acc = acc * scale
    o_ref[...] = acc.astype(o_ref.dtype)


def _proj_call(xn, w, out_dtype, mode, tm, tn, cos=None, sin=None, rot_half=0, scale=1.0, tp=None,
               col_scale=None):
    rows, d = xn.shape
    n = w.shape[1]
    in_specs = [pl.BlockSpec((tm, d), lambda i, j: (i, 0)),
                pl.BlockSpec((d, tn), lambda i, j: (0, j))]
    args = [xn, w]
    if mode == "rope":
        nt = tp // tm
        if col_scale is None:
            col_scale = jnp.ones((1, n), F32)
        in_specs += [pl.BlockSpec((tm, LANES), lambda i, j: (i % nt, 0)),
                     pl.BlockSpec((tm, LANES), lambda i, j: (i % nt, 0)),
                     pl.BlockSpec((1, tn), lambda i, j: (0, j))]
        args += [cos, sin, col_scale]
    return pl.pallas_call(
        functools.partial(_proj_kernel, mode=mode, rot_half=rot_half, scale=scale),
        grid=(rows // tm, n // tn),
        in_specs=in_specs,
        out_specs=pl.BlockSpec((tm, tn), lambda i, j: (i, j)),
        out_shape=jax.ShapeDtypeStruct((rows, n), out_dtype),
        compiler_params=_params(("parallel", "arbitrary")),
    )(*args)


def _indexer_kernel(qi_ref, kc_ref, wi_ref, mask_ref, keys_ref, qs_ref, p_ref,
                    *, front, topk, n_chunks_total):
    i = pl.program_id(1)
    n_chunk = i + 1
    n_pair = (n_chunk + 1) // 2
    tq = qi_ref.shape[0]
    n_hp = IDX_HEADS // 2
    q_idx = i * tq + lax.broadcasted_iota(I32, (LANES, tq), 1)
    k_off = lax.broadcasted_iota(I32, (LANES, tq), 0)

    w_t = wi_ref[...].T
    for p in range(n_hp):
        qs_ref[p * tq:(p + 1) * tq, :] = qi_ref[:, p * LANES:(p + 1) * LANES]

    nt_dims = (((1,), (1,)), ((), ()))

    def score_body(c4, carry):
        for u in range(SCORE_CHUNKS):
            c = SCORE_CHUNKS * c4 + u
            off = pl.multiple_of(c * LANES, LANES)
            kc = kc_ref[pl.ds(pl.multiple_of(c * 2 * LANES, 2 * LANES), 2 * LANES), :]
            acc = jnp.zeros((LANES, tq), F32)
            for g in range(n_hp // 2):
                qg = qs_ref[2 * g * tq:(2 * g + 2) * tq, :]
                st = lax.dot_general(kc, qg, nt_dims, preferred_element_type=F32)
                for v in range(2):
                    h0 = 2 * (2 * g + v)
                    sp = st[:, v * tq:(v + 1) * tq]
                    acc = acc + w_t[h0:h0 + 1, :] * jnp.maximum(sp[:LANES], 0.0)
                    acc = acc + w_t[h0 + 1:h0 + 2, :] * jnp.maximum(sp[LANES:], 0.0)
            k_idx = off + k_off
            valid = (k_idx <= q_idx) & (k_idx >= front)
            acc = jnp.where(acc == 0.0, 0.0, acc)
            score = jnp.where(valid, acc, -jnp.inf)
            bits = pltpu.bitcast(score, I32)
            key = jnp.where(bits < 0, bits ^ 0x7FFFFFFF, bits)
            keys_ref[pl.ds(off, LANES), :] = key
            carry = jnp.maximum(carry, key)
        return carry

    key_max = lax.fori_loop(0, (n_chunk + SCORE_CHUNKS - 1) // SCORE_CHUNKS, score_body,
                            jnp.full((LANES, tq), INT_MIN, I32))

    def count(pred_fn):
        def body(c2, cnt):
            off = pl.multiple_of(c2 * 2 * LANES, 2 * LANES)
            k0 = keys_ref[pl.ds(off, LANES), :]
            k1 = keys_ref[pl.ds(off + LANES, LANES), :]
            cnt = cnt + jnp.where(pred_fn(k0, off + k_off), 1, 0)
            return cnt + jnp.where(pred_fn(k1, off + LANES + k_off), 1, 0)
        cnt = lax.fori_loop(0, n_pair, body, jnp.zeros((LANES, tq), I32))
        return jnp.sum(cnt, axis=0, keepdims=True)

    bits_per_check = 4

    def bit_cond(state):
        it, _, cnt_t = state
        return (it < 32) & (jnp.max(jnp.where(cnt_t != topk, 1, 0)) > 0)

    def bit_body(state):
        it, t, cnt_t = state
        for u in range(bits_per_check):
            trial = t + jnp.left_shift(jnp.int32(1), 31 - u - it)
            cnt = count(lambda k, _: k >= trial)
            take = cnt >= topk
            t = jnp.where(take, trial, t)
            cnt_t = jnp.where(take, cnt, cnt_t)
        return it + bits_per_check, t, cnt_t

    full_search = (jnp.int32(0), jnp.full((1, tq), INT_MIN, I32), jnp.full((1, tq), -1, I32))

    def bucket_start():
        base = jnp.max(key_max, axis=0, keepdims=True) & jnp.int32(-EXP_UNIT)
        found = jnp.zeros((1, tq), I32)
        t_b = jnp.full((1, tq), INT_MIN, I32)
        c_b = jnp.full((1, tq), -1, I32)
        for d in range(SPEC_BUCKETS):
            start = base - d * EXP_UNIT
            cnt = count(lambda k, _: k >= start)
            first = (cnt >= topk) & (found == 0)
            t_b = jnp.where(first, start, t_b)
            c_b = jnp.where(first, cnt, c_b)
            found = jnp.where(cnt >= topk, 1, found)
        usable = (found > 0) & (base < jnp.int32(0x7F800000))
        every = jnp.min(jnp.where(usable, 1, 0)) > 0
        return (jnp.where(every, jnp.int32(8), jnp.int32(0)), jnp.where(every, t_b, INT_MIN),
                jnp.where(every, c_b, -1))

    state0 = lax.cond(n_chunk >= SPEC_MIN_CHUNKS, bucket_start, lambda: full_search)
    _, thr, cnt_ge = lax.while_loop(bit_cond, bit_body, state0)

    finite = thr > KEY_NEG_INF
    tie_rows = finite & (cnt_ge > topk)
    p_ref[...] = jnp.full(p_ref.shape, (n_chunks_total + 1) * LANES, I32)

    @pl.when(jnp.max(jnp.where(tie_rows, 1, 0)) > 0)
    def _():
        cnt_gt = count(lambda k, _: k > thr)
        need = topk - cnt_gt

        def idx_body(it, p):
            trial = p + jnp.left_shift(jnp.int32(1), 14 - it)
            cnt = count(lambda k, idx: (k == thr) & (idx < trial))
            return jnp.where(cnt < need, trial, p)

        p = lax.fori_loop(0, 15, idx_body, jnp.zeros((1, tq), I32))
        p = jnp.where(tie_rows, p, (n_chunks_total + 1) * LANES)
        p_ref[...] = jnp.broadcast_to(p, p_ref.shape)

    p_lim = p_ref[0:1, :]
    eye = jnp.where(lax.broadcasted_iota(I32, (tq, tq), 0) == lax.broadcasted_iota(I32, (tq, tq), 1),
                    1.0, 0.0).astype(BF16)

    per_trip = 8

    def write_body(c4, carry):
        for u in range(per_trip):
            c = jnp.minimum(per_trip * c4 + u, n_chunk - 1)
            off = pl.multiple_of(c * LANES, LANES)
            k = keys_ref[pl.ds(off, LANES), :]
            k_idx = off + k_off
            sel = (k > thr) | ((k == thr) & (k_idx <= p_lim))
            valid = (k_idx <= q_idx) & (k_idx >= front)
            sel_t = jnp.where(sel & valid, 1.0, 0.0).astype(BF16)
            sel_qk = lax.dot_general(eye, sel_t, nt_dims, preferred_element_type=F32)
            mask_ref[:, pl.ds(off, LANES)] = sel_qk.astype(jnp.int8)
        return carry

    lax.fori_loop(0, (n_chunk + per_trip - 1) // per_trip, write_body, 0)

    def zero_body(c, carry):
        off = pl.multiple_of(c * LANES, LANES)
        mask_ref[:, pl.ds(off, LANES)] = jnp.zeros((tq, LANES), jnp.int8)
        return carry

    lax.fori_loop(n_chunk, n_chunks_total, zero_body, 0)


def _indexer_call(idx_qk, k_cat, wi, bsz, tp, front, topk):
    tq = ROW_BLOCK
    nq = tp // tq
    n_qcols = IDX_HEADS * IDX_DIM
    return pl.pallas_call(
        functools.partial(_indexer_kernel, front=front, topk=topk, n_chunks_total=nq),
        grid=(bsz, nq),
        in_specs=[pl.BlockSpec((tq, n_qcols), lambda b, i: (b * nq + i, 0)),
                  pl.BlockSpec((2 * (tp + (SCORE_CHUNKS - 1) * LANES), LANES), lambda b, i: (b, 0)),
                  pl.BlockSpec((tq, LANES), lambda b, i: (b * nq + i, 0))],
        out_specs=pl.BlockSpec((tq, tp), lambda b, i: (b * nq + i, 0)),
        out_shape=jax.ShapeDtypeStruct((bsz * tp, tp), jnp.int8),
        scratch_shapes=[pltpu.VMEM((tp + (SCORE_CHUNKS - 1) * LANES, tq), I32),
                        pltpu.VMEM((n_qcols // LANES * tq, LANES), BF16),
                        pltpu.VMEM((8, tq), I32)],
        compiler_params=_params(("parallel", "arbitrary")),
    )(idx_qk, k_cat, wi)


def _causal_tile_pairs(nb):
    pairs = [(i, j) for i in range(nb) for j in range(i + 1)]
    return (jnp.array([p[0] for p in pairs], I32), jnp.array([p[1] for p in pairs], I32))


def _attn_a_kernel(qt_ref, kt_ref, q_ref, k_ref, v_ref, mask_ref, o_ref, m_ref, l_ref, s_ref, p_ref,
                   *acc_refs):
    t = pl.program_id(1)
    i = qt_ref[t]
    j = kt_ref[t]
    tt = q_ref.shape[0]
    nt_dims = (((1,), (1,)), ((), ()))

    @pl.when(j == 0)
    def _():
        m_ref[...] = jnp.full(m_ref.shape, NEG_BIG, F32)
        l_ref[...] = jnp.zeros(l_ref.shape, F32)
        for acc_ref in acc_refs:
            acc_ref[...] = jnp.zeros(acc_ref.shape, F32)

    def group(rows, kw):
        lane = lax.broadcasted_iota(I32, (ROW_BLOCK, LANES), 1)
        bias = jnp.where(mask_ref[rows, :kw].astype(I32) != 0, 0.0, NEG_BIG)
        m_all = m_ref[rows, :]
        m_loc = m_all
        for h in range(A_HEADS):
            sl = slice(h * HEAD_DIM, (h + 1) * HEAD_DIM)
            s = lax.dot_general(q_ref[rows, sl], k_ref[:kw, sl], nt_dims, preferred_element_type=F32) + bias
            s_ref[h, :, :kw] = s
            m_loc = jnp.where(lane == h, jnp.max(s, axis=1, keepdims=True), m_loc)
        m_new = jnp.maximum(m_all, m_loc)
        alpha = jnp.exp2(m_all - m_new)
        l_out = alpha * l_ref[rows, :]
        for h in range(A_HEADS):
            p = jnp.exp2(s_ref[h, :, :kw] - m_new[:, h:h + 1])
            p_ref[h, :, :kw] = p.astype(BF16)
            l_out = l_out + jnp.where(lane == h, jnp.sum(p, axis=1, keepdims=True), 0.0)
        for h in range(A_HEADS):
            sl = slice(h * HEAD_DIM, (h + 1) * HEAD_DIM)
            acc_refs[h][rows, :] = alpha[:, h:h + 1] * acc_refs[h][rows, :] + jnp.dot(
                p_ref[h, :, :kw], v_ref[:kw, sl], preferred_element_type=F32)
        m_ref[rows, :] = m_new
        l_ref[rows, :] = l_out

    @pl.when(j < i)
    def _():
        def body(r, carry):
            group(pl.ds(pl.multiple_of(r * ROW_BLOCK, ROW_BLOCK), ROW_BLOCK), tt)
            return carry

        lax.fori_loop(0, tt // ROW_BLOCK, body, 0)

    @pl.when(j == i)
    def _():
        for r in range(tt // ROW_BLOCK):
            group(slice(r * ROW_BLOCK, (r + 1) * ROW_BLOCK), (r + 1) * ROW_BLOCK)
        l_all = l_ref[...]
        for h in range(A_HEADS):
            sl = slice(h * HEAD_DIM, (h + 1) * HEAD_DIM)
            o_ref[:, sl] = (acc_refs[h][...] / l_all[:, h:h + 1]).astype(o_ref.dtype)


def _attn_a_call(qk, v, mask, bsz, tp, tt):
    nb = tp // tt
    w = A_WIDTH
    q_tile, k_tile = _causal_tile_pairs(nb)
    grid_spec = pltpu.PrefetchScalarGridSpec(
        num_scalar_prefetch=2,
        grid=(bsz, q_tile.shape[0]),
        in_specs=[pl.BlockSpec((tt, w), lambda b, t, qt, kt: (b * nb + qt[t], 0)),
                  pl.BlockSpec((tt, w), lambda b, t, qt, kt: (b * nb + kt[t], 1)),
                  pl.BlockSpec((tt, w), lambda b, t, qt, kt: (b * nb + kt[t], 0)),
                  pl.BlockSpec((tt, tt), lambda b, t, qt, kt: (b * nb + qt[t], kt[t]))],
        out_specs=pl.BlockSpec((tt, w), lambda b, t, qt, kt: (b * nb + qt[t], 0)),
        scratch_shapes=[pltpu.VMEM((tt, LANES), F32), pltpu.VMEM((tt, LANES), F32),
                        pltpu.VMEM((A_HEADS, ROW_BLOCK, tt), F32),
                        pltpu.VMEM((A_HEADS, ROW_BLOCK, tt), BF16)]
        + [pltpu.VMEM((tt, HEAD_DIM), F32) for _ in range(A_HEADS)],
    )
    return pl.pallas_call(
        _attn_a_kernel,
        grid_spec=grid_spec,
        out_shape=jax.ShapeDtypeStruct((bsz * tp, w), BF16),
        compiler_params=_params(("parallel", "arbitrary")),
    )(q_tile, k_tile, qk, qk, v, mask)


def _attn_b_kernel(qt_ref, kt_ref, q_ref, k_ref, v_ref, lq1_ref, lk1_ref, lq2_ref, lk2_ref, g_ref, o_ref,
                   m_ref, l_ref, s_ref, p_ref, acc_ref, *, front, lambda_init):
    t = pl.program_id(1)
    i = qt_ref[t]
    j = kt_ref[t]
    tt = q_ref.shape[0]
    nt_dims = (((1,), (1,)), ((), ()))
    dv = 2 * HEAD_DIM
    n_set = 2 * B_HEADS

    @pl.when(j == 0)
    def _():
        m_ref[...] = jnp.full(m_ref.shape, NEG_BIG, F32)
        l_ref[...] = jnp.zeros(l_ref.shape, F32)
        acc_ref[...] = jnp.zeros(acc_ref.shape, F32)

    def group(rows, r0, kw, masked):
        lane = lax.broadcasted_iota(I32, (ROW_BLOCK, LANES), 1)
        if masked:
            q_idx = i * tt + r0 + lax.broadcasted_iota(I32, (ROW_BLOCK, kw), 0)
            k_idx = j * tt + lax.broadcasted_iota(I32, (ROW_BLOCK, kw), 1)
            bias = jnp.where((k_idx <= q_idx) & (k_idx >= front), 0.0, NEG_BIG)
        m_all = m_ref[rows, :]
        m_loc = m_all
        for c in range(n_set):
            sl = slice(c * HEAD_DIM, (c + 1) * HEAD_DIM)
            s = lax.dot_general(q_ref[rows, sl], k_ref[:kw, sl], nt_dims, preferred_element_type=F32)
            if masked:
                s = s + bias
            s_ref[c, :, :kw] = s
            m_loc = jnp.where(lane == c, jnp.max(s, axis=1, keepdims=True), m_loc)
        m_new = jnp.maximum(m_all, m_loc)
        alpha = jnp.exp2(m_all - m_new)
        l_out = alpha * l_ref[rows, :]
        for c in range(n_set):
            p = jnp.exp2(s_ref[c, :, :kw] - m_new[:, c:c + 1])
            p_ref[c, :, :kw] = p.astype(BF16)
            l_out = l_out + jnp.where(lane == c, jnp.sum(p, axis=1, keepdims=True), 0.0)
        for c in range(n_set):
            vh = v_ref[:kw, (c // 2) * dv:(c // 2 + 1) * dv]
            acc_ref[c, rows, :] = alpha[:, c:c + 1] * acc_ref[c, rows, :] + jnp.dot(
                p_ref[c, :, :kw], vh, preferred_element_type=F32)
        m_ref[rows, :] = m_new
        l_ref[rows, :] = l_out

    def full_tile(masked):
        def body(r, carry):
            r0 = pl.multiple_of(r * ROW_BLOCK, ROW_BLOCK)
            group(pl.ds(r0, ROW_BLOCK), r0, tt, masked)
            return carry

        lax.fori_loop(0, tt // ROW_BLOCK, body, 0)

    pl.when((j == 0) & (j < i))(functools.partial(full_tile, True))
    pl.when((j > 0) & (j < i))(functools.partial(full_tile, False))

    @pl.when(j == i)
    def _():
        for r in range(tt // ROW_BLOCK):
            group(slice(r * ROW_BLOCK, (r + 1) * ROW_BLOCK), r * ROW_BLOCK, (r + 1) * ROW_BLOCK, True)
        lam = (jnp.exp(jnp.sum(lq1_ref[...] * lk1_ref[...], axis=1, keepdims=True))
               - jnp.exp(jnp.sum(lq2_ref[...] * lk2_ref[...], axis=1, keepdims=True))
               + lambda_init)
        g = g_ref[...]
        l_all = l_ref[...]
        for h in range(B_HEADS):
            o = (acc_ref[2 * h] / l_all[:, 2 * h:2 * h + 1]
                 - lam * (acc_ref[2 * h + 1] / l_all[:, 2 * h + 1:2 * h + 2]))
            ms = jnp.mean(o * o, axis=-1, keepdims=True)
            y = o * lax.rsqrt(ms + RMS_EPS) * g
            o_ref[:, h * dv:(h + 1) * dv] = (y * (1.0 - lambda_init)).astype(o_ref.dtype)


def _attn_b_call(qk, v, lq1, lk1, lq2, lk2, subln_g, bsz, tp, tt, front, lambda_init):
    nb = tp // tt
    w = B_QK_WIDTH
    dv = 2 * HEAD_DIM
    vec = lambda a: a.reshape(1, -1).astype(F32)
    small = lambda n: pl.BlockSpec((1, n), lambda b, t, qt, kt: (0, 0))
    q_tile, k_tile = _causal_tile_pairs(nb)
    grid_spec = pltpu.PrefetchScalarGridSpec(
        num_scalar_prefetch=2,
        grid=(bsz, q_tile.shape[0]),
        in_specs=[pl.BlockSpec((tt, w), lambda b, t, qt, kt: (b * nb + qt[t], 2)),
                  pl.BlockSpec((tt, w), lambda b, t, qt, kt: (b * nb + kt[t], 3)),
                  pl.BlockSpec((tt, B_V_WIDTH), lambda b, t, qt, kt: (b * nb + kt[t], 1)),
                  small(HEAD_DIM), small(HEAD_DIM), small(HEAD_DIM), small(HEAD_DIM), small(dv)],
        out_specs=pl.BlockSpec((tt, B_V_WIDTH), lambda b, t, qt, kt: (b * nb + qt[t], 0)),
        scratch_shapes=[pltpu.VMEM((tt, LANES), F32),
                        pltpu.VMEM((tt, LANES), F32),
                        pltpu.VMEM((2 * B_HEADS, ROW_BLOCK, tt), F32),
                        pltpu.VMEM((2 * B_HEADS, ROW_BLOCK, tt), BF16),
                        pltpu.VMEM((2 * B_HEADS, tt, dv), F32)],
    )
    return pl.pallas_call(
        functools.partial(_attn_b_kernel, front=front, lambda_init=lambda_init),
        grid_spec=grid_spec,
        out_shape=jax.ShapeDtypeStruct((bsz * tp, B_V_WIDTH), BF16),
        compiler_params=_params(("parallel", "arbitrary")),
    )(q_tile, k_tile, qk, qk, v, vec(lq1), vec(lk1), vec(lq2), vec(lk2), vec(subln_g))


def _merge_router_kernel(oa_ref, ob_ref, ga_ref, gb_ref, h_ref, wua_ref, wub_ref, wo_ref, g_ref,
                         wr_ref, br_ref, h1_ref, hfp_ref, route_ref, gate_ref, cnt_ref, carry_ref,
                         *, tp, front):
    i = pl.program_id(0)
    tm = oa_ref.shape[0]
    d = h_ref.shape[1]

    @pl.when(i == 0)
    def _():
        carry_ref[...] = jnp.zeros(carry_ref.shape, F32)

    ua = jnp.dot(oa_ref[...], wua_ref[...], preferred_element_type=F32)
    ub = jnp.dot(ob_ref[...], wub_ref[...], preferred_element_type=F32)
    merged = ga_ref[...].astype(F32) * ua + gb_ref[...].astype(F32) * ub
    mix = jnp.dot(merged.astype(BF16), wo_ref[...], preferred_element_type=F32)
    h1 = h_ref[...] + mix
    h1_ref[...] = h1

    ms = jnp.mean(h1 * h1, axis=-1, keepdims=True)
    hf = (h1 * lax.rsqrt(ms + RMS_EPS) * g_ref[...]).astype(BF16)

    hf_bits = pltpu.bitcast(hf.astype(F32), jnp.uint32)
    hfp_ref[...] = (hf_bits[:, :d // 2] & jnp.uint32(0xFFFF0000)) | (hf_bits[:, d // 2:] >> 16)

    logits = jnp.dot(hf, wr_ref[...], preferred_element_type=F32) + br_ref[...]
    lane = lax.broadcasted_iota(I32, (tm, LANES), 1)
    row = (i * tm + lax.broadcasted_iota(I32, (tm, 1), 0)) % tp
    routed = row >= front

    work = logits
    vals, idxs, onehots = [], [], []
    for _ in range(TOP_K):
        v = jnp.max(work, axis=1, keepdims=True)
        e = jnp.min(jnp.where(work == v, lane, LANES), axis=1, keepdims=True)
        hit = lane == e
        vals.append(v)
        idxs.append(e)
        onehots.append(hit)
        work = jnp.where(hit, -jnp.inf, work)

    exps = [jnp.exp(v - vals[0]) for v in vals]
    denom = exps[0] + exps[1] + exps[2] + exps[3]

    member = jnp.zeros((tm, LANES), F32)
    for hit in onehots:
        member = member + jnp.where(hit & routed, 1.0, 0.0)
    r_i = lax.broadcasted_iota(I32, (tm, tm), 0)
    c_i = lax.broadcasted_iota(I32, (tm, tm), 1)
    lower = jnp.where(c_i < r_i, 1.0, 0.0).astype(BF16)
    before = jnp.dot(lower, member.astype(BF16), preferred_element_type=F32) + carry_ref[...]
    carry_ref[...] = carry_ref[...] + jnp.sum(member, axis=0, keepdims=True)

    route = jnp.zeros((tm, LANES), I32)
    gates = jnp.zeros((tm, LANES), F32)
    for k in range(TOP_K):
        rank = jnp.sum(jnp.where(onehots[k], before, 0.0), axis=1, keepdims=True)
        route = jnp.where(lane == k, idxs[k], route)
        route = jnp.where(lane == TOP_K + k, rank.astype(I32), route)
        gates = jnp.where(lane == k, exps[k] / denom, gates)
    route_ref[...] = route
    gate_ref[...] = gates
    cnt_ref[...] = jnp.broadcast_to(carry_ref[...], cnt_ref.shape)


def _merge_router_call(o_a, o_b, gates, h0, wua, wub, wo, g_ffn, wr, br, tp, front, tm):
    rows, d = h0.shape
    nt = rows // tm
    once = pl.Buffered(1)
    const = lambda shape: pl.BlockSpec(shape, lambda i: (0, 0), pipeline_mode=once)
    return pl.pallas_call(
        functools.partial(_merge_router_kernel, tp=tp, front=front),
        grid=(nt,),
        in_specs=[pl.BlockSpec((tm, A_WIDTH), lambda i: (i, 0)),
                  pl.BlockSpec((tm, B_V_WIDTH), lambda i: (i, 0)),
                  pl.BlockSpec((tm, d), lambda i: (i, 0)),
                  pl.BlockSpec((tm, d), lambda i: (i, 1)),
                  pl.BlockSpec((tm, d), lambda i: (i, 0)),
                  const((A_WIDTH, d)), const((B_V_WIDTH, d)), const((d, d)),
                  const((1, d)), const((d, LANES)), const((1, LANES))],
        out_specs=[pl.BlockSpec((tm, d), lambda i: (i, 0)),
                   pl.BlockSpec((tm, d // 2), lambda i: (i, 0)),
                   pl.BlockSpec((tm, LANES), lambda i: (i, 0)),
                   pl.BlockSpec((tm, LANES), lambda i: (i, 0)),
                   pl.BlockSpec((8, LANES), lambda i: (0, 0))],
        out_shape=[jax.ShapeDtypeStruct((rows, d), F32),
                   jax.ShapeDtypeStruct((rows, d // 2), jnp.uint32),
                   jax.ShapeDtypeStruct((rows, LANES), I32),
                   jax.ShapeDtypeStruct((rows, LANES), F32),
                   jax.ShapeDtypeStruct((8, LANES), F32)],
        scratch_shapes=[pltpu.VMEM((1, LANES), F32)],
        compiler_params=_params(("arbitrary",)),
    )(o_a, o_b, gates, gates, h0, wua, wub, wo, g_ffn, wr, br)


def _dispatch_kernel(dest_ref, hfp_ref, xb_in_ref, xb_ref, sem):
    del xb_in_ref
    tm = hfp_ref.shape[0]

    def row_copy(r, k):
        d = dest_ref[r * TOP_K + k]
        return pltpu.make_async_copy(hfp_ref.at[pl.ds(r, 1), :], xb_ref.at[pl.ds(d, 1), :], sem)

    def issue(r, carry):
        for k in range(TOP_K):
            row_copy(r, k).start()
        return carry

    lax.fori_loop(0, tm, issue, 0, unroll=8)

    def drain(r, carry):
        for k in range(TOP_K):
            row_copy(r, k).wait()
        return carry

    lax.fori_loop(0, tm, drain, 0, unroll=8)


def _dispatch_call(dest_flat, hfp, n_rows_total, tm):
    rows, half = hfp.shape
    xb0 = jnp.zeros((n_rows_total, half), jnp.uint32)
    return pl.pallas_call(
        _dispatch_kernel,
        grid=(rows // tm,),
        in_specs=[pl.BlockSpec((tm * TOP_K,), lambda i: (i,), memory_space=pltpu.SMEM),
                  pl.BlockSpec((tm, half), lambda i: (i, 0)),
                  pl.BlockSpec(memory_space=pl.ANY)],
        out_specs=pl.BlockSpec(memory_space=pl.ANY),
        out_shape=jax.ShapeDtypeStruct((n_rows_total, half), jnp.uint32),
        scratch_shapes=[pltpu.SemaphoreType.DMA(())],
        input_output_aliases={2: 0},
        compiler_params=_params(("arbitrary",)),
    )(dest_flat, hfp, xb0)


def _w1_prep_kernel(w_ref, o_ref):
    n = 2 * LANES
    s_i = lax.broadcasted_iota(I32, (n, n), 0)
    j_i = lax.broadcasted_iota(I32, (n, n), 1)
    src = jnp.where(j_i < LANES, 2 * j_i, 2 * (j_i - LANES) + 1)
    perm = jnp.where(s_i == src, 1.0, 0.0).astype(BF16)
    for c in range(w_ref.shape[2] // n):
        blk = w_ref[0, :, c * n:(c + 1) * n].astype(BF16)
        o_ref[0, :, c * n:(c + 1) * n] = jnp.dot(blk, perm, preferred_element_type=F32).astype(BF16)


def _w1_prep_call(w1l, tk):
    n_exp, d, two_ff = w1l.shape
    return pl.pallas_call(
        _w1_prep_kernel,
        grid=(n_exp, d // tk),
        in_specs=[pl.BlockSpec((1, tk, two_ff), lambda e, k: (e, k, 0))],
        out_specs=pl.BlockSpec((1, tk, two_ff), lambda e, k: (e, k, 0)),
        out_shape=jax.ShapeDtypeStruct((n_exp, d, two_ff), BF16),
        compiler_params=_params(("parallel", "parallel")),
    )(w1l)


def _expert_kernel(be_ref, nu_ref, xb_ref, w1_ref, b1_ref, w2_ref, b2_ref, y_ref):
    i = pl.program_id(0)
    d_ff = w2_ref.shape[1]

    @pl.when(i < nu_ref[0])
    def _():
        packed = xb_ref[...]
        hi = pltpu.bitcast(packed & jnp.uint32(0xFFFF0000), F32)
        lo = pltpu.bitcast(packed << 16, F32)
        x = jnp.concatenate([hi, lo], axis=1).astype(BF16)
        u = jnp.dot(x, w1_ref[0], preferred_element_type=F32) + b1_ref[0]
        n_grp = d_ff // LANES
        glu = jnp.concatenate([u[:, 2 * c * LANES:(2 * c + 1) * LANES] for c in range(n_grp)], axis=1)
        lin = jnp.concatenate([u[:, (2 * c + 1) * LANES:(2 * c + 2) * LANES] for c in range(n_grp)], axis=1)
        x_glu = jnp.minimum(glu, SWIGLU_LIMIT)
        x_lin = jnp.clip(lin, -SWIGLU_LIMIT, SWIGLU_LIMIT)
        act = x_glu * jax.nn.sigmoid(SWIGLU_ALPHA * x_glu) * (x_lin + 1.0)
        y_ref[...] = jnp.dot(act.astype(BF16), w2_ref[0], preferred_element_type=F32) + b2_ref[0]

    @pl.when(i >= nu_ref[0])
    def _():
        y_ref[...] = jnp.zeros(y_ref.shape, F32)


def _expert_call(block_e, n_used, xb, w1p, b1p, w2b, b2, n_blocks):
    n_exp, d, two_ff = w1p.shape
    d_ff = two_ff // 2
    half = xb.shape[1]

    def blk(i, be, nu):
        return jnp.minimum(i, nu[0] - 1)

    grid_spec = pltpu.PrefetchScalarGridSpec(
        num_scalar_prefetch=2,
        grid=(n_blocks,),
        in_specs=[pl.BlockSpec((ROW_BLOCK, half), lambda i, be, nu: (blk(i, be, nu), 0)),
                  pl.BlockSpec((1, d, two_ff), lambda i, be, nu: (be[blk(i, be, nu)], 0, 0)),
                  pl.BlockSpec((1, 1, two_ff), lambda i, be, nu: (be[blk(i, be, nu)], 0, 0)),
                  pl.BlockSpec((1, d_ff, d), lambda i, be, nu: (be[blk(i, be, nu)], 0, 0)),
                  pl.BlockSpec((1, 1, d), lambda i, be, nu: (be[blk(i, be, nu)], 0, 0))],
        out_specs=pl.BlockSpec((ROW_BLOCK, d), lambda i, be, nu: (i, 0)),
    )
    return pl.pallas_call(
        _expert_kernel,
        grid_spec=grid_spec,
        out_shape=jax.ShapeDtypeStruct((n_blocks * ROW_BLOCK, d), F32),
        compiler_params=_params(("arbitrary",), vmem=60 * 1024 * 1024),
    )(block_e, n_used, xb, w1p, b1p.reshape(n_exp, 1, two_ff), w2b, b2.reshape(n_exp, 1, d))


def _combine_kernel(dest_ref, dest_next_ref, h1_ref, gate_ref, g_ref, yb_ref, o_ref, ybuf_ref, sems):
    i = pl.program_id(0)
    n = pl.num_programs(0)
    tm = h1_ref.shape[0]
    slot = i % 2

    def row_copy(idx_ref, buf, r, k):
        d = idx_ref[r * TOP_K + k]
        return pltpu.make_async_copy(yb_ref.at[pl.ds(d, 1), :], ybuf_ref.at[buf, k, pl.ds(r, 1), :],
                                     sems.at[buf])

    def issue(idx_ref, buf):
        def body(r, carry):
            for k in range(TOP_K):
                row_copy(idx_ref, buf, r, k).start()
            return carry

        lax.fori_loop(0, tm, body, 0, unroll=8)

    @pl.when(i == 0)
    def _():
        issue(dest_ref, slot)

    @pl.when(i + 1 < n)
    def _():
        issue(dest_next_ref, 1 - slot)

    def drain(r, carry):
        for k in range(TOP_K):
            row_copy(dest_ref, slot, r, k).wait()
        return carry

    lax.fori_loop(0, tm, drain, 0, unroll=8)

    gate = gate_ref[...]
    h2 = h1_ref[...]
    for k in range(TOP_K):
        h2 = h2 + gate[:, k:k + 1] * ybuf_ref[slot, k]
    ms = jnp.mean(h2 * h2, axis=-1, keepdims=True)
    o_ref[...] = h2 * lax.rsqrt(ms + RMS_EPS) * g_ref[...]


def _combine_call(dest_flat, h1, gate, g_final, yb, bsz, s_len, tp):
    d = h1.shape[1]
    tm = ROW_BLOCK
    ns = s_len // tm
    nb = tp // tm
    skip = (tp - s_len) // tm

    def src(i):
        return (i // ns) * nb + skip + i % ns

    def src_next(i):
        return src(jnp.minimum(i + 1, bsz * ns - 1))

    return pl.pallas_call(
        _combine_kernel,
        grid=(bsz * ns,),
        in_specs=[pl.BlockSpec((tm * TOP_K,), lambda i: (src(i),), memory_space=pltpu.SMEM),
                  pl.BlockSpec((tm * TOP_K,), lambda i: (src_next(i),), memory_space=pltpu.SMEM),
                  pl.BlockSpec((tm, d), lambda i: (src(i), 0)),
                  pl.BlockSpec((tm, LANES), lambda i: (src(i), 0)),
                  pl.BlockSpec((1, d), lambda i: (0, 0)),
                  pl.BlockSpec(memory_space=pl.ANY)],
        out_specs=pl.BlockSpec((tm, d), lambda i: (i, 0)),
        out_shape=jax.ShapeDtypeStruct((bsz * s_len, d), F32),
        scratch_shapes=[pltpu.VMEM((2, TOP_K, tm, d), F32), pltpu.SemaphoreType.DMA((2,))],
        compiler_params=_params(("arbitrary",)),
    )(dest_flat, dest_flat, h1, gate, g_final.reshape(1, d), yb)


def _rope_tables(pos, dim, reps):
    inv = 1.0 / (ROPE_THETA ** (jnp.arange(0, dim, 2, dtype=F32) / dim))
    ang = pos[:, None] * inv[None, :]
    cos = jnp.concatenate([jnp.cos(ang), jnp.cos(ang)], axis=-1)
    sin = jnp.concatenate([-jnp.sin(ang), jnp.sin(ang)], axis=-1)
    return jnp.tile(cos, (1, reps)), jnp.tile(sin, (1, reps))


def kernel(x, meta_tokens, norm_mix_g, w_in, lambda_q1, lambda_k1, lambda_q2, lambda_k2, subln_g, w_up_a, w_up_b, w_out, norm_ffn_g, w_router, b_router, w1, b1, w2, b2, norm_final_g):
    bsz, s_len, d = x.shape
    n_meta = meta_tokens.shape[0]
    depth = w_in.shape[0]
    n_exp = w_router.shape[-1]
    d_ff = w2.shape[-2]
    t_len = s_len + n_meta
    tp = -(-t_len // ROW_BLOCK) * ROW_BLOCK
    front = tp - t_len
    topk = min(TOPK_MAX, t_len // 4)
    rows = bsz * tp
    assert s_len % ROW_BLOCK == 0 and n_exp <= LANES and d % (2 * LANES) == 0 and d_ff % LANES == 0

    tm_proj = _pick_tile(tp, (640, 512, 256, 128))
    tt = _pick_tile(tp, (640, 512, 256, 128))
    tm_merge = _pick_tile(rows, (256, 128))

    pos = jnp.arange(tp, dtype=F32) - float(front)
    cos_h, sin_h = _rope_tables(pos, HEAD_DIM, 1)
    cos_i, sin_i = _rope_tables(pos, IDX_DIM, LANES // IDX_DIM)

    head = jnp.concatenate([jnp.zeros((front, d), x.dtype), meta_tokens.astype(x.dtype)], axis=0)

    o_qa, o_ka, o_va = 0, A_WIDTH, 2 * A_WIDTH
    o_qi = 3 * A_WIDTH
    o_ki = o_qi + IDX_HEADS * IDX_DIM
    o_wi = o_ki + IDX_DIM
    o_qb = o_wi + IDX_HEADS
    o_kb = o_qb + B_QK_WIDTH
    o_vb = o_kb + B_QK_WIDTH
    o_ga = o_vb + B_V_WIDTH
    o_gb = o_ga + d

    for l in range(depth):
        lambda_init = 0.8 - 0.6 * math.exp(-0.3 * l)
        wl = w_in[l]
        col = lambda o, n: wl[:, o:o + n]
        zeros_half = jnp.zeros((d, LANES - IDX_DIM), wl.dtype)
        w_rope = jnp.concatenate([col(o_qa, A_WIDTH), col(o_ka, A_WIDTH),
                                  col(o_qb, B_QK_WIDTH), col(o_kb, B_QK_WIDTH)], axis=1).astype(BF16)
        w_val = jnp.concatenate([col(o_va, A_WIDTH), col(o_vb, B_V_WIDTH)], axis=1).astype(BF16)
        w_gate = jnp.concatenate([col(o_ga, d), col(o_gb, d)], axis=1).astype(BF16)
        w_idx = jnp.concatenate([col(o_qi, IDX_HEADS * IDX_DIM), col(o_ki, IDX_DIM), zeros_half,
                                 zeros_half, col(o_ki, IDX_DIM)], axis=1).astype(BF16)
        w_wi = jnp.concatenate([col(o_wi, IDX_HEADS), jnp.zeros((d, LANES - IDX_HEADS), wl.dtype)],
                               axis=1).astype(BF16)

        h, hn = _embed_norm_call(head, x.reshape(bsz * s_len, d), norm_mix_g[l], bsz, tp)
        q_scale = HEAD_DIM ** -0.5 * math.log2(math.e)
        col_scale = jnp.concatenate([jnp.full((A_WIDTH,), q_scale, F32), jnp.ones((A_WIDTH,), F32),
                                     jnp.full((B_QK_WIDTH,), q_scale, F32), jnp.ones((B_QK_WIDTH,), F32)])
        qk = _proj_call(hn, w_rope, BF16, "rope", tm_proj, 1024, cos_h, sin_h, HEAD_DIM // 2, tp=tp,
                        col_scale=col_scale.reshape(1, -1))
        val = _proj_call(hn, w_val, BF16, "none", tm_proj, 1024)
        gates = _proj_call(hn, w_gate, BF16, "sigmoid", tm_proj, _pick_tile(2 * d, (1024, 512, 256)))
        idx_qk = _proj_call(hn, w_idx, BF16, "rope", tm_proj, (IDX_HEADS * IDX_DIM + 2 * LANES) // 2,
                            cos_i, sin_i, IDX_DIM // 2, tp=tp)
        wi = _proj_call(hn, w_wi, F32, "scale", tm_proj, LANES,
                        scale=IDX_HEADS ** -0.5 * IDX_DIM ** -0.5)

        n_qcols = IDX_HEADS * IDX_DIM
        nc = tp // LANES
        k_cat = jnp.stack([idx_qk[:, n_qcols:n_qcols + LANES].reshape(bsz, nc, LANES, LANES),
                           idx_qk[:, n_qcols + LANES:].reshape(bsz, nc, LANES, LANES)], axis=2)
        k_cat = jnp.pad(k_cat, ((0, 0), (0, SCORE_CHUNKS - 1), (0, 0), (0, 0), (0, 0))).reshape(
            bsz * 2 * (tp + (SCORE_CHUNKS - 1) * LANES), LANES)
        mask = _indexer_call(idx_qk, k_cat, wi, bsz, tp, front, topk)
        o_a = _attn_a_call(qk, val, mask, bsz, tp, tt)
        o_b = _attn_b_call(qk, val, lambda_q1[l], lambda_k1[l], lambda_q2[l], lambda_k2[l], subln_g[l],
                           bsz, tp, tt, front, lambda_init)

        wr = jnp.concatenate([w_router[l], jnp.zeros((d, LANES - n_exp), F32)], axis=1).astype(BF16)
        br = jnp.concatenate([b_router[l], jnp.full((LANES - n_exp,), NEG_BIG, F32)]).reshape(1, LANES)
        h1, hfp, route, gate, cnt = _merge_router_call(
            o_a, o_b, gates, h, w_up_a[l].astype(BF16), w_up_b[l].astype(BF16), w_out[l].astype(BF16),
            norm_ffn_g[l].reshape(1, d), wr, br, tp, front, tm_merge)

        n_assign = bsz * t_len * TOP_K
        n_blocks = (n_assign + n_exp * (ROW_BLOCK - 1) + ROW_BLOCK - 1) // ROW_BLOCK
        n_rows = n_blocks * ROW_BLOCK
        counts = cnt[0, :n_exp].astype(I32)
        padded = (counts + ROW_BLOCK - 1) // ROW_BLOCK * ROW_BLOCK
        pad_end = jnp.cumsum(padded)
        pad_start = pad_end - padded
        n_used = (pad_end[-1:] // ROW_BLOCK).astype(I32)
        blk_start = jnp.arange(n_blocks, dtype=I32) * ROW_BLOCK
        block_e = jnp.minimum(jnp.sum((pad_end[None, :] <= blk_start[:, None]).astype(I32), axis=1),
                              n_exp - 1).astype(I32)
        top_e = route[:, :TOP_K]
        rank = route[:, TOP_K:2 * TOP_K]
        row_id = jnp.arange(rows, dtype=I32)
        is_pad = (row_id % tp) < front
        n_trash = bsz * front * TOP_K
        trash = (n_rows + ((row_id // tp) * front + row_id % tp)[:, None] * TOP_K
                 + jnp.arange(TOP_K, dtype=I32)[None, :])
        dest = jnp.where(is_pad[:, None], trash, pad_start[top_e] + rank)
        dest_flat = dest.reshape(-1).astype(I32)
        n_rows_total = n_rows + -(-max(n_trash, 1) // 8) * 8

        xb = _dispatch_call(dest_flat, hfp, n_rows_total, tm_merge)

        w1p = _w1_prep_call(w1[l], _pick_tile(d, (256, 128)))
        b1p = b1[l].reshape(n_exp, d_ff // LANES, LANES, 2).transpose(0, 1, 3, 2).reshape(n_exp, 2 * d_ff)
        yb = _expert_call(block_e, n_used, xb, w1p, b1p, w2[l].astype(BF16), b2[l], n_blocks)

        if l + 1 < depth:
            raise NotImplementedError("only the single-layer configuration is implemented")
        out = _combine_call(dest_flat, h1, gate, norm_final_g, yb, bsz, s_len, tp)
    return out.reshape(bsz, s_len, d)
```
